```python
import jax
import jax.numpy as jnp
from jax import lax

D_MODEL = 1024
BATCH = 4
SEQ = 8192
DEPTH = 2
DEC_BATCH = 32
DEC_SEQ = 1
PAST_LEN = 16384
PAGE_SIZE = 128

D_MIX = D_MODEL
M_HEADS = 4
M_WIDTH = D_MIX // 2
M_DIM = M_WIDTH // M_HEADS
M_CHUNK = 64
A_HEADS = 8
A_WIDTH = D_MIX - M_WIDTH
A_DIM = A_WIDTH // A_HEADS
A_KV = 2
A_REP = A_HEADS // A_KV
KV_WIDTH = A_KV * A_DIM
L_CMP = 32
D_CMP = 16
L_SEL = 64
N_TOP = 16
WINDOW = 512
Q_BLOCK = 128
N_BRANCH = 3
ROPE_THETA = 500000.0
ROPE_DIM = A_DIM // 4
D_FF = -(-8 * D_MODEL // (3 * 256)) * 256
EPS = 1e-6
NEG = -1e30
FORCE_SCORE = 1e4
N_KV_SLOTS = 4
SPLIT_SIZES = (M_WIDTH, M_WIDTH, M_WIDTH, M_WIDTH, M_HEADS, M_HEADS,
               A_WIDTH, KV_WIDTH, KV_WIDTH, KV_WIDTH, KV_WIDTH, KV_WIDTH, KV_WIDTH,
               N_BRANCH * A_HEADS)
D_IN = sum(SPLIT_SIZES)

kernel_name = 'hymba_mlstm_nsa_decoder_step'


def rmsnorm(x, g):
    xf = x.astype(jnp.float32)
    y = xf * lax.rsqrt(jnp.mean(xf * xf, axis=-1, keepdims=True) + EPS)
    return (y * g.astype(jnp.float32)).astype(x.dtype)


def rope(x, pos):
    half = ROPE_DIM // 2
    inv = ROPE_THETA ** (-jnp.arange(half, dtype=jnp.float32) / half)
    ang = pos.astype(jnp.float32)[:, None] * inv[None, :]
    cos = jnp.cos(ang)[:, None, :]
    sin = jnp.sin(ang)[:, None, :]
    xr = x[..., :ROPE_DIM].astype(jnp.float32)
    x1, x2 = xr[..., :half], xr[..., half:]
    rot = jnp.concatenate([x1 * cos - x2 * sin, x2 * cos + x1 * sin], axis=-1).astype(x.dtype)
    return jnp.concatenate([rot, x[..., ROPE_DIM:]], axis=-1)


def mix_inputs(xn, pos, wi, bif):
    B, T, _ = xn.shape
    proj = jnp.einsum('btd,de->bte', xn, wi)
    parts = []
    off = 0
    for size in SPLIT_SIZES:
        parts.append(proj[..., off:off + size])
        off += size
    mq, mk, mv, mo, mi, mf, aq, kcr, vcr, ksl, vsl, kw, vw, ag = parts
    heads_m = lambda a: a.reshape(B, T, M_HEADS, M_DIM).transpose(0, 2, 1, 3)
    kv = lambda a: a.reshape(B, T, A_KV, A_DIM)
    i_pre = (mi.astype(jnp.float32) + bif[0].astype(jnp.float32)).transpose(0, 2, 1)
    log_f = jax.nn.log_sigmoid(mf.astype(jnp.float32) + bif[1].astype(jnp.float32)).transpose(0, 2, 1)
    q = rope(aq.reshape(B, T, A_HEADS, A_DIM), pos).reshape(B, T, A_KV, A_REP, A_DIM)
    gates = jax.nn.sigmoid(ag.reshape(B, T, A_KV, A_REP, N_BRANCH))
    return (heads_m(mq) * (M_DIM ** -0.5), heads_m(mk), heads_m(mv), jax.nn.sigmoid(mo), i_pre, log_f,
            q, kv(kcr), kv(vcr), rope(kv(ksl), pos), kv(vsl), rope(kv(kw), pos), kv(vw), gates)


def mlstm_chunk(carry, q, k, v, i_pre, log_f):
    c_prev, n_prev, m_prev = carry
    q, k, v = (a.astype(jnp.float32) for a in (q, k, v))
    L = q.shape[2]
    b = jnp.cumsum(log_f, axis=-1)
    causal = jnp.tril(jnp.ones((L, L), dtype=bool))
    d = jnp.where(causal, b[..., :, None] - b[..., None, :] + i_pre[..., None, :], NEG)
    m_inter = b + m_prev[..., None]
    m_t = jnp.maximum(m_inter, jnp.max(d, axis=-1))
    w_inter = jnp.exp(m_inter - m_t)
    s = jnp.einsum('bhtd,bhsd->bhts', q, k) * jnp.exp(d - m_t[..., None])
    num = w_inter[..., None] * jnp.einsum('bhtd,bhde->bhte', q, c_prev) + jnp.einsum('bhts,bhse->bhte', s, v)
    den = w_inter * jnp.einsum('bhtd,bhd->bht', q, n_prev) + jnp.sum(s, axis=-1)
    h = num / jnp.maximum(jnp.abs(den), jnp.exp(-m_t))[..., None]
    m_new = m_t[..., -1]
    w_new = jnp.exp(b[..., -1:] - b + i_pre - m_new[..., None])
    decay = jnp.exp(m_inter[..., -1] - m_new)
    c_new = decay[..., None, None] * c_prev + jnp.einsum('bhs,bhsd,bhse->bhde', w_new, k, v)
    n_new = decay[..., None] * n_prev + jnp.einsum('bhs,bhsd->bhd', w_new, k)
    return (c_new, n_new, m_new), h


def mlstm_prompt(q, k, v, i_pre, log_f):
    B, NH, T, DK = q.shape
    nc = T // M_CHUNK

    def chunks(a):
        return jnp.moveaxis(a.reshape(B, NH, nc, M_CHUNK, *a.shape[3:]), 2, 0)

    carry0 = (jnp.zeros((B, NH, DK, M_DIM), jnp.float32), jnp.zeros((B, NH, DK), jnp.float32),
              jnp.full((B, NH), NEG, jnp.float32))
    carry, h = lax.scan(lambda c, xs: mlstm_chunk(c, *xs), carry0,
                        (chunks(q), chunks(k), chunks(v), chunks(i_pre), chunks(log_f)))
    return jnp.moveaxis(h, 0, 2).reshape(B, NH, T, M_DIM), carry


def mlstm_out(h, o_gate, g):
    B, NH, T, DV = h.shape
    h = h.transpose(0, 2, 1, 3)
    h = h * lax.rsqrt(jnp.mean(h * h, axis=-1, keepdims=True) + EPS)
    return (h.reshape(B, T, M_WIDTH) * g.astype(jnp.float32)).astype(o_gate.dtype) * o_gate


def compress(raw, w_cmp, b_cmp):
    B, T = raw.shape[:2]
    n_chunk = T // D_CMP
    ch = raw[:, :n_chunk * D_CMP].reshape(B, n_chunk, D_CMP, A_KV, A_DIM)
    first = jnp.einsum('bcrgd,rde->bcge', ch, w_cmp[:D_CMP])
    second = jnp.einsum('bcrgd,rde->bcge', ch, w_cmp[D_CMP:])
    return first[:, :-1] + second[:, 1:] + b_cmp


def compressed_kv(kcr, vcr, wck, bck, wcv, bcv):
    kc = compress(kcr, wck, bck)
    kc = rope(kc, jnp.arange(kc.shape[1]) * D_CMP)
    return kc, compress(vcr, wcv, bcv)


def to_blocks(k):
    B, T = k.shape[:2]
    n_sel = -(-T // L_SEL)
    k = jnp.pad(k, ((0, 0), (0, n_sel * L_SEL - T), (0, 0), (0, 0)))
    return k.reshape(B, n_sel, L_SEL, A_KV, A_DIM).transpose(0, 3, 1, 2, 4)


def overlap_matrix(n_cmp, n_sel):
    lo_c = jnp.arange(n_cmp) * D_CMP
    lo_s = jnp.arange(n_sel) * L_SEL
    ov = jnp.minimum(lo_c[:, None] + L_CMP, lo_s[None, :] + L_SEL) - jnp.maximum(lo_c[:, None], lo_s[None, :])
    return jnp.clip(ov, 0, None).astype(jnp.float32) / L_CMP


def masked_softmax(s, valid):
    p = jax.nn.softmax(jnp.where(valid, s.astype(jnp.float32), NEG), axis=-1)
    return jnp.where(valid, p, 0.0)


def nsa_block(q, q_pos, kc, vc, ks_blk, vs_blk, kw, vw, kw_pos, gates):
    B, Tq = q.shape[:2]
    dt = q.dtype
    scale = A_DIM ** -0.5
    n_cmp = kc.shape[1]
    n_sel = ks_blk.shape[2]
    n_top = min(N_TOP, n_sel)
    c_valid = (jnp.arange(n_cmp) * D_CMP + (L_CMP - 1))[None, :] <= q_pos[:, None]
    p_c = masked_softmax(jnp.einsum('bqgrd,bngd->bgrqn', q, kc) * scale, c_valid)
    o_c = jnp.einsum('bgrqn,bngd->bqgrd', p_c.astype(dt), vc)
    imp = jnp.einsum('bgrqn,ns->bgqs', p_c, overlap_matrix(n_cmp, n_sel))
    blk = jnp.arange(n_sel)[None, :]
    cur = (q_pos // L_SEL)[:, None]
    forced = (blk == 0) | (blk == cur) | (blk == cur - 1)
    started = blk * L_SEL <= q_pos[:, None]
    imp = jnp.where(forced, FORCE_SCORE, jnp.where(started, imp, -1.0))
    _, idx = lax.top_k(imp, n_top)
    gather = jax.vmap(jax.vmap(lambda kb, ix: kb[ix]))
    kg = gather(ks_blk, idx).reshape(B, A_KV, Tq, n_top * L_SEL, A_DIM)
    vg = gather(vs_blk, idx).reshape(B, A_KV, Tq, n_top * L_SEL, A_DIM)
    s_pos = (idx[..., None] * L_SEL + jnp.arange(L_SEL)).reshape(B, A_KV, Tq, n_top * L_SEL)
    s_valid = (s_pos <= q_pos[:, None])[:, :, None]
    p_s = masked_softmax(jnp.einsum('bqgrd,bgqkd->bgrqk', q, kg) * scale, s_valid)
    o_s = jnp.einsum('bgrqk,bgqkd->bqgrd', p_s.astype(dt), vg)
    rel = q_pos[:, None] - kw_pos[None, :]
    w_valid = (rel >= 0) & (rel < WINDOW) & (kw_pos[None, :] >= 0)
    p_w = masked_softmax(jnp.einsum('bqgrd,bkgd->bgrqk', q, kw) * scale, w_valid)
    o_w = jnp.einsum('bgrqk,bkgd->bqgrd', p_w.astype(dt), vw)
    return gates[..., 0:1] * o_c + gates[..., 1:2] * o_s + gates[..., 2:3] * o_w


def nsa_prompt(q, gates, kc, vc, ks_blk, vs_blk, kw, vw):
    B, T = q.shape[:2]
    nqb = T // Q_BLOCK
    pad = ((0, 0), (WINDOW, 0), (0, 0), (0, 0))
    kw_pad = jnp.pad(kw, pad)
    vw_pad = jnp.pad(vw, pad)

    def one(args):
        qb, gb, i = args
        q0 = i * Q_BLOCK
        q_pos = q0 + jnp.arange(Q_BLOCK)
        kwb = lax.dynamic_slice_in_dim(kw_pad, q0, WINDOW + Q_BLOCK, axis=1)
        vwb = lax.dynamic_slice_in_dim(vw_pad, q0, WINDOW + Q_BLOCK, axis=1)
        kw_pos = q0 - WINDOW + jnp.arange(WINDOW + Q_BLOCK)
        return nsa_block(qb, q_pos, kc, vc, ks_blk, vs_blk, kwb, vwb, kw_pos, gb)

    qs = q.reshape(B, nqb, Q_BLOCK, A_KV, A_REP, A_DIM).swapaxes(0, 1)
    gs = gates.reshape(B, nqb, Q_BLOCK, A_KV, A_REP, N_BRANCH).swapaxes(0, 1)
    out = lax.map(one, (qs, gs, jnp.arange(nqb)))
    return out.swapaxes(0, 1).reshape(B, T, A_KV, A_REP, A_DIM)


def merge(h_m, o_a, wo):
    B, T = o_a.shape[:2]
    mixed = jnp.concatenate([h_m, o_a.reshape(B, T, A_WIDTH).astype(h_m.dtype)], axis=-1)
    return jnp.einsum('bte,ed->btd', mixed, wo)


def swiglu(xn, wg, wu, wd):
    return jnp.einsum('btf,fd->btd', jax.nn.silu(jnp.einsum('btd,df->btf', xn, wg)) * jnp.einsum('btd,df->btf', xn, wu), wd)


def prompt_layer(x, nm, wi, bif, mg, wck, bck, wcv, bcv, wo, nf, wg, wu, wd):
    T = x.shape[1]
    pos = jnp.arange(T)
    mq, mk, mv, mo, mi, mf, q, kcr, vcr, ksl, vsl, kw, vw, gates = mix_inputs(rmsnorm(x, nm), pos, wi, bif)
    h, carry = mlstm_prompt(mq, mk, mv, mi, mf)
    kc, vc = compressed_kv(kcr, vcr, wck, bck, wcv, bcv)
    o_a = nsa_prompt(q, gates, kc, vc, to_blocks(ksl), to_blocks(vsl), kw, vw)
    x = x + merge(mlstm_out(h, mo, mg), o_a, wo)
    x = x + swiglu(rmsnorm(x, nf), wg, wu, wd)
    wb = min(WINDOW, T)
    rows = jnp.stack([kcr, vcr, ksl, vsl], axis=2)
    win = jnp.stack([kw[:, -wb:], vw[:, -wb:]], axis=1)
    return x, rows, win, carry


def sample_layer(x, pages, win, carry, nm, wi, bif, mg, wck, bck, wcv, bcv, wo, nf, wg, wu, wd):
    B, T = x.shape[:2]
    pos = PAST_LEN + jnp.arange(T)
    mq, mk, mv, mo, mi, mf, q, kcr, vcr, ksl, vsl, kw, vw, gates = mix_inputs(rmsnorm(x, nm), pos, wi, bif)
    carry, h = mlstm_chunk(carry, mq, mk, mv, mi, mf)
    past = pages.transpose(0, 2, 1, 3, 4, 5).reshape(B, N_KV_SLOTS, -1, A_KV, A_DIM)
    kc, vc = compressed_kv(jnp.concatenate([past[:, 0], kcr], axis=1), jnp.concatenate([past[:, 1], vcr], axis=1),
                           wck, bck, wcv, bcv)
    ks_blk = to_blocks(jnp.concatenate([past[:, 2], ksl], axis=1))
    vs_blk = to_blocks(jnp.concatenate([past[:, 3], vsl], axis=1))
    wb = win.shape[2]
    kw_all = jnp.concatenate([win[:, 0], kw], axis=1)
    vw_all = jnp.concatenate([win[:, 1], vw], axis=1)
    kw_pos = jnp.arange(kw_all.shape[1]) + (PAST_LEN - wb)
    o_a = nsa_block(q, pos, kc, vc, ks_blk, vs_blk, kw_all, vw_all, kw_pos, gates)
    x = x + merge(mlstm_out(h, mo, mg), o_a, wo)
    x = x + swiglu(rmsnorm(x, nf), wg, wu, wd)
    rows = jnp.stack([kcr, vcr, ksl, vsl], axis=2)
    new_win = jnp.stack([kw_all[:, -wb:], vw_all[:, -wb:]], axis=1)
    return x, rows, new_win, carry


def setup_inputs(seed: int = 0) -> dict:
    key = jax.random.key(seed)
    ks = jax.random.split(key, 26)
    f32 = jnp.float32
    nrm = lambda k, shape, scale=1.0: scale * jax.random.normal(k, shape, f32)
    n_pages = PAST_LEN // PAGE_SIZE
    n_used = DEC_BATCH * n_pages
    n_pool = n_used + n_used // 4
    wb = min(WINDOW, PAST_LEN)
    page_table = jax.random.permutation(ks[3], n_pool)[:n_used].reshape(DEC_BATCH, n_pages).astype(jnp.int32)
    b_if = jnp.stack([nrm(ks[10], (DEPTH, M_HEADS), 0.1),
                      jnp.linspace(3.0, 6.0, M_HEADS, dtype=f32)[None, :] + nrm(ks[11], (DEPTH, M_HEADS), 0.1)], axis=1)
    return {
        'x_prompt': nrm(ks[0], (BATCH, SEQ, D_MODEL)),
        'x_sample': nrm(ks[1], (DEC_BATCH, DEC_SEQ, D_MODEL)),
        'cache_nsa_kv': nrm(ks[2], (n_pool, DEPTH, N_KV_SLOTS, PAGE_SIZE, A_KV, A_DIM)),
        'page_table': page_table,
        'state_win_kv': nrm(ks[4], (DEC_BATCH, DEPTH, 2, wb, A_KV, A_DIM)),
        'state_mlstm_c': nrm(ks[5], (DEC_BATCH, DEPTH, M_HEADS, M_DIM, M_DIM), 0.5),
        'state_mlstm_n': nrm(ks[6], (DEC_BATCH, DEPTH, M_HEADS, M_DIM)),
        'state_mlstm_m': nrm(ks[7], (DEC_BATCH, DEPTH, M_HEADS)),
        'norm_mix': 1.0 + nrm(ks[8], (DEPTH, D_MODEL), 0.05),
        'w_in': nrm(ks[9], (DEPTH, D_MODEL, D_IN), D_MODEL ** -0.5),
        'b_if': b_if,
        'm_norm': 1.0 + nrm(ks[12], (DEPTH, M_WIDTH), 0.05),
        'w_cmp_k': nrm(ks[13], (DEPTH, L_CMP, A_DIM, A_DIM), (L_CMP * A_DIM) ** -0.5),
        'b_cmp_k': nrm(ks[14], (DEPTH, A_DIM), 0.02),
        'w_cmp_v': nrm(ks[15], (DEPTH, L_CMP, A_DIM, A_DIM), (L_CMP * A_DIM) ** -0.5),
        'b_cmp_v': nrm(ks[16], (DEPTH, A_DIM), 0.02),
        'w_out': nrm(ks[17], (DEPTH, D_MIX, D_MODEL), D_MIX ** -0.5),
        'norm_ffn': 1.0 + nrm(ks[18], (DEPTH, D_MODEL), 0.05),
        'w_gate': nrm(ks[19], (DEPTH, D_MODEL, D_FF), D_MODEL ** -0.5),
        'w_up': nrm(ks[20], (DEPTH, D_MODEL, D_FF), D_MODEL ** -0.5),
        'w_down': nrm(ks[21], (DEPTH, D_FF, D_MODEL), D_FF ** -0.5),
        'norm_final': 1.0 + nrm(ks[22], (D_MODEL,), 0.05),
    }


def reference(x_prompt, x_sample, cache_nsa_kv, page_table, state_win_kv, state_mlstm_c, state_mlstm_n, state_mlstm_m,
              norm_mix, w_in, b_if, m_norm, w_cmp_k, b_cmp_k, w_cmp_v, b_cmp_v, w_out, norm_ffn, w_gate, w_up, w_down,
              norm_final):
    xp, xs = x_prompt, x_sample
    rows_p, rows_s, win_p, win_s, c_p, c_s, n_p, n_s, m_p, m_s = ([] for _ in range(10))
    for l in range(DEPTH):
        params = (norm_mix[l], w_in[l], b_if[l], m_norm[l], w_cmp_k[l], b_cmp_k[l], w_cmp_v[l], b_cmp_v[l],
                  w_out[l], norm_ffn[l], w_gate[l], w_up[l], w_down[l])
        xp, rows, win, (c, n, m) = prompt_layer(xp, *params)
        rows_p.append(rows); win_p.append(win); c_p.append(c); n_p.append(n); m_p.append(m)
        pages = cache_nsa_kv[page_table, l]
        carry = (state_mlstm_c[:, l].astype(jnp.float32), state_mlstm_n[:, l].astype(jnp.float32),
                 state_mlstm_m[:, l].astype(jnp.float32))
        xs, rows, win, (c, n, m) = sample_layer(xs, pages, state_win_kv[:, l], carry, *params)
        rows_s.append(rows); win_s.append(win); c_s.append(c); n_s.append(n); m_s.append(m)
    y_prompt = rmsnorm(xp, norm_final)
    y_sample = rmsnorm(xs, norm_final)
    return (y_prompt, y_sample,
            jnp.stack(rows_p, axis=2), jnp.stack(rows_s, axis=2),
            jnp.stack(win_p, axis=1), jnp.stack(win_s, axis=1),
            jnp.stack(c_p, axis=1), jnp.stack(c_s, axis=1),
            jnp.stack(n_p, axis=1), jnp.stack(n_s, axis=1),
            jnp.stack(m_p, axis=1), jnp.stack(m_s, axis=1))
```

```python
import functools

import jax
import jax.numpy as jnp
import numpy as np
from jax import lax
from jax.experimental import pallas as pl
from jax.experimental.pallas import tpu as pltpu

F32 = jnp.float32
BF16 = jnp.bfloat16

M_HEADS = 4
M_DIM = 128
M_WIDTH = M_HEADS * M_DIM
A_HEADS = 8
A_DIM = 64
A_KV = 2
A_REP = A_HEADS // A_KV
A_WIDTH = A_HEADS * A_DIM
KV_WIDTH = A_KV * A_DIM
L_CMP = 32
D_CMP = 16
L_SEL = 64
N_TOP = 16
WINDOW = 512
Q_BLOCK = 128
N_BRANCH = 3
ROPE_THETA = 500000.0
ROPE_DIM = A_DIM // 4
EPS = 1e-6
NEG = -1e30
FORCE_SCORE = 1e4
LANES = 128
N_MISC = 2 * M_HEADS + N_BRANCH * A_HEADS
D_PERM = 4 * M_WIDTH + A_WIDTH + 6 * KV_WIDTH + LANES
VMEM_LIMIT = 56 * 1024 * 1024

_NT = (((1,), (1,)), ((), ()))
_TN = (((0,), (0,)), ((), ()))


def _params(*sem):
    return pltpu.CompilerParams(dimension_semantics=sem, vmem_limit_bytes=VMEM_LIMIT)


def _sigmoid(x):
    return 1.0 / (1.0 + jnp.exp(-x))


def _dot(a, b):
    return jnp.dot(a, b, preferred_element_type=F32)


def _dot_exact(a, b):
    return jnp.dot(a, b, preferred_element_type=F32, precision=lax.Precision.HIGHEST)


def _dot_nt(a, b):
    return lax.dot_general(a, b, _NT, preferred_element_type=F32)


def _rope128(v, c, s1, s2):
    half = ROPE_DIM // 2
    return v * c + pltpu.roll(v, LANES - half, 1) * s1 + pltpu.roll(v, half, 1) * s2


def _rope_tables(pos):
    half = ROPE_DIM // 2
    inv = ROPE_THETA ** (-jnp.arange(half, dtype=F32) / half)
    ang = pos.astype(F32)[:, None] * inv[None, :]
    cos, sin = jnp.cos(ang), jnp.sin(ang)
    n = pos.shape[0]
    one = jnp.ones((n, A_DIM - ROPE_DIM), F32)
    zero = jnp.zeros((n, A_DIM - ROPE_DIM), F32)
    zh = jnp.zeros((n, half), F32)
    c = jnp.concatenate([cos, cos, one], axis=1)
    s1 = jnp.concatenate([-sin, zh, zero], axis=1)
    s2 = jnp.concatenate([zh, sin, zero], axis=1)
    tile = lambda a: jnp.concatenate([a, a], axis=1)
    return tile(c), tile(s1), tile(s2)


def _inproj_kernel(x_ref, g_ref, w_ref, bias_ref, rc_ref, rs1_ref, rs2_ref,
                   om_ref, og_ref, oq_ref, orows_ref, owin_ref, omisc_ref, okv_ref):
    x = x_ref[...]
    ms = jnp.mean(x * x, axis=-1, keepdims=True)
    xn = ((x * lax.rsqrt(ms + EPS)) * g_ref[...]).astype(BF16)
    c, s1, s2 = rc_ref[...], rs1_ref[...], rs2_ref[...]
    lane = lax.broadcasted_iota(jnp.int32, (x.shape[0], LANES), 1)

    def proj(a, b):
        return _dot(xn, w_ref[:, a:b])

    om_ref[:, 0:M_WIDTH] = (proj(0, M_WIDTH) * (M_DIM ** -0.5)).astype(BF16)
    om_ref[:, M_WIDTH:3 * M_WIDTH] = proj(M_WIDTH, 3 * M_WIDTH).astype(BF16)
    og_ref[...] = _sigmoid(proj(3 * M_WIDTH, 4 * M_WIDTH))

    off = 4 * M_WIDTH
    pq = proj(off, off + A_WIDTH)
    for j in range(A_WIDTH // LANES):
        blk = _rope128(pq[:, j * LANES:(j + 1) * LANES], c, s1, s2) * (A_DIM ** -0.5)
        swapped = pltpu.roll(blk, A_DIM, 1)
        for e in range(2):
            hd = 2 * j + e
            grp = hd // A_REP
            src = blk if e == grp else swapped
            keep = (lane // A_DIM) == grp
            oq_ref[hd] = jnp.where(keep, src, 0.0).astype(BF16)

    off += A_WIDTH
    pk = proj(off, off + 6 * KV_WIDTH)
    kcr, vcr, ksl, vsl, kw, vw = [pk[:, i * LANES:(i + 1) * LANES] for i in range(6)]
    ksl = _rope128(ksl, c, s1, s2)
    kw = _rope128(kw, c, s1, s2)
    for i, a in enumerate((kcr, vcr, ksl, vsl)):
        orows_ref[:, i * LANES:(i + 1) * LANES] = a
    owin_ref[:, 0:LANES] = kw
    owin_ref[:, LANES:2 * LANES] = vw
    for i, a in enumerate((ksl, vsl, kw, vw)):
        okv_ref[:, i * LANES:(i + 1) * LANES] = a.astype(BF16)

    off += 6 * KV_WIDTH
    pm = proj(off, off + LANES) + bias_ref[...]
    log_sig = -(jnp.maximum(-pm, 0.0) + jnp.log(1.0 + jnp.exp(-jnp.abs(pm))))
    omisc_ref[...] = jnp.where(lane < M_HEADS, pm, jnp.where(lane < 2 * M_HEADS, log_sig, _sigmoid(pm)))


def _inproj(x, gain, w_perm, bias, tabs, tm):
    M, D = x.shape
    nt = tabs[0].shape[0] // tm
    row = lambda i: (i, 0)
    tab = pl.BlockSpec((tm, LANES), lambda i: (i % nt, 0))
    full = lambda a: pl.BlockSpec(a.shape, lambda i: (0,) * a.ndim)
    out_shape = (
        jax.ShapeDtypeStruct((M, 3 * M_WIDTH), BF16),
        jax.ShapeDtypeStruct((M, M_WIDTH), F32),
        jax.ShapeDtypeStruct((A_HEADS, M, LANES), BF16),
        jax.ShapeDtypeStruct((M, 4 * KV_WIDTH), F32),
        jax.ShapeDtypeStruct((M, 2 * KV_WIDTH), F32),
        jax.ShapeDtypeStruct((M, LANES), F32),
        jax.ShapeDtypeStruct((M, 4 * KV_WIDTH), BF16),
    )
    out_specs = (
        pl.BlockSpec((tm, 3 * M_WIDTH), row),
        pl.BlockSpec((tm, M_WIDTH), row),
        pl.BlockSpec((A_HEADS, tm, LANES), lambda i: (0, i, 0)),
        pl.BlockSpec((tm, 4 * KV_WIDTH), row),
        pl.BlockSpec((tm, 2 * KV_WIDTH), row),
        pl.BlockSpec((tm, LANES), row),
        pl.BlockSpec((tm, 4 * KV_WIDTH), row),
    )
    return pl.pallas_call(
        _inproj_kernel,
        grid=(M // tm,),
        in_specs=[pl.BlockSpec((tm, D), row), full(gain), full(w_perm), full(bias), tab, tab, tab],
        out_specs=out_specs,
        out_shape=out_shape,
        compiler_params=_params("parallel"),
        name="inproj",
    )(x, gain, w_perm, bias, *tabs)


def _mlstm_kernel(q_ref, k_ref, v_ref, og_ref, misc_ref, gt_ref, mg_ref, c0_ref, n0_ref, m0_ref,
                  hm_ref, c_ref, n_ref, m_ref):
    L = q_ref.shape[1]

    @pl.when(pl.program_id(1) == 0)
    def _():
        c_ref[...] = c0_ref[...]
        n_ref[...] = n0_ref[...]
        m_ref[...] = m0_ref[...]

    gc = misc_ref[0]
    gr = gt_ref[0]
    row = lax.broadcasted_iota(jnp.int32, (L, L), 0)
    col = lax.broadcasted_iota(jnp.int32, (L, L), 1)
    causal = row >= col
    b_col = _dot_exact(causal.astype(F32), gc)
    b_row = _dot_exact(gr, (row <= col).astype(F32))
    for h in range(M_HEADS):
        sl = slice(h * M_DIM, (h + 1) * M_DIM)
        q, k, v = q_ref[0, :, sl], k_ref[0, :, sl], v_ref[0, :, sl]
        c_prev, n_prev, m_prev = c_ref[0, h], n_ref[0, h:h + 1, :], m_ref[0, h:h + 1, 0:1]
        bc = b_col[:, M_HEADS + h:M_HEADS + h + 1]
        ic = gc[:, h:h + 1]
        br = b_row[M_HEADS + h:M_HEADS + h + 1, :]
        ir = gr[h:h + 1, :]
        d = jnp.where(causal, bc - br + ir, NEG)
        m_inter = bc + m_prev
        m_t = jnp.maximum(m_inter, jnp.max(d, axis=-1, keepdims=True))
        w_inter = jnp.exp(m_inter - m_t)
        p = _dot_nt(q, k) * jnp.exp(d - m_t)
        num = w_inter * _dot(q, c_prev.astype(BF16)) + _dot(p.astype(BF16), v)
        den = (w_inter * jnp.sum(q.astype(F32) * n_prev, axis=-1, keepdims=True)
               + jnp.sum(p, axis=-1, keepdims=True))
        hh = num / jnp.maximum(jnp.abs(den), jnp.exp(-m_t))
        m_new = m_t[L - 1:L, :]
        w_new = jnp.exp(bc[L - 1:L, :] - bc + ic - m_new)
        decay = jnp.exp(m_inter[L - 1:L, :] - m_new)
        kf, vf = k.astype(F32), v.astype(F32)
        c_ref[0, h] = decay * c_prev + lax.dot_general(k, (w_new * vf).astype(BF16), _TN,
                                                       preferred_element_type=F32)
        n_ref[0, h:h + 1, :] = decay * n_prev + jnp.sum(w_new * kf, axis=0, keepdims=True)
        m_ref[0, h:h + 1, :] = jnp.broadcast_to(m_new, (1, M_DIM))
        hn = hh * lax.rsqrt(jnp.mean(hh * hh, axis=-1, keepdims=True) + EPS)
        hm_ref[0, :, sl] = ((hn * mg_ref[:, sl]) * og_ref[0, :, sl]).astype(BF16)


def _mlstm(qkv, og, misc, gt, mg, c0, n0, m0, L):
    B, T, _ = qkv.shape
    seq = lambda j: pl.BlockSpec((1, L, M_WIDTH), lambda b, c: (b, c, j))
    st4 = pl.BlockSpec((1, M_HEADS, M_DIM, M_DIM), lambda b, c: (b, 0, 0, 0))
    st3 = pl.BlockSpec((1, M_HEADS, M_DIM), lambda b, c: (b, 0, 0))
    return pl.pallas_call(
        _mlstm_kernel,
        grid=(B, T // L),
        in_specs=[seq(0), seq(1), seq(2), seq(0),
                  pl.BlockSpec((1, L, LANES), lambda b, c: (b, c, 0)),
                  pl.BlockSpec((1, 2 * M_HEADS, L), lambda b, c: (b, 0, c)),
                  pl.BlockSpec((1, M_WIDTH), lambda b, c: (0, 0)),
                  st4, st3, st3],
        out_specs=(seq(0), st4, st3, st3),
        out_shape=(jax.ShapeDtypeStruct((B, T, M_WIDTH), BF16),
                   jax.ShapeDtypeStruct(c0.shape, F32),
                   jax.ShapeDtypeStruct(n0.shape, F32),
                   jax.ShapeDtypeStruct(m0.shape, F32)),
        compiler_params=_params("parallel", "arbitrary"),
        name="mlstm",
    )(qkv, qkv, qkv, og, misc, gt, mg, c0, n0, m0)


def _cmp_accumulate(load, w_ref, s):
    acc = None
    for r in range(D_CMP):
        term = _dot(load(r).astype(BF16), w_ref[s, r])
        acc = term if acc is None else acc + term
    return acc


def _cmp_prompt_kernel(k_ref, v_ref, w_ref, fs_ref):
    n = k_ref.shape[0] // D_CMP
    for s, ref in enumerate((k_ref, v_ref)):
        fs_ref[s] = _cmp_accumulate(lambda r: ref[pl.ds(r, n, stride=D_CMP), :], w_ref, s)


def _cmp_prompt(rows, w_cmp, tr):
    M = rows.shape[0]
    return pl.pallas_call(
        _cmp_prompt_kernel,
        grid=(M // tr,),
        in_specs=[pl.BlockSpec((tr, KV_WIDTH), lambda i: (i, 0)), pl.BlockSpec((tr, KV_WIDTH), lambda i: (i, 1)),
                  pl.BlockSpec(w_cmp.shape, lambda i: (0, 0, 0, 0))],
        out_specs=pl.BlockSpec((2, tr // D_CMP, 2 * LANES), lambda i: (0, i, 0)),
        out_shape=jax.ShapeDtypeStruct((2, M // D_CMP, 2 * LANES), F32),
        compiler_params=_params("parallel"),
        name="cmp_prompt",
    )(rows, rows, w_cmp)


def _cmp_sample_kernel(n_group, pt_ref, *refs):
    page_refs, w_ref, fs_ref = refs[:n_group], refs[n_group], refs[n_group + 1]
    per_page = page_refs[0].shape[1] // D_CMP
    for s in range(2):
        fs_ref[s, 0] = _cmp_accumulate(
            lambda r: jnp.concatenate([p[s, pl.ds(r, per_page, stride=D_CMP), :] for p in page_refs], axis=0),
            w_ref, s)


def _cmp_sample(cache5, page_table, layer, w_cmp, n_group):
    B, n_pages = page_table.shape
    page = cache5.shape[3]
    per_page = page // D_CMP

    def page_spec(i):
        return pl.BlockSpec((None, None, 2, page, LANES),
                            lambda b, p, pt: (pt[b, p * n_group + i], layer, 0, 0, 0))

    grid_spec = pltpu.PrefetchScalarGridSpec(
        num_scalar_prefetch=1,
        grid=(B, n_pages // n_group),
        in_specs=[page_spec(i) for i in range(n_group)]
        + [pl.BlockSpec(w_cmp.shape, lambda b, p, pt: (0, 0, 0, 0))],
        out_specs=pl.BlockSpec((2, 1, n_group * per_page, 2 * LANES), lambda b, p, pt: (0, b, p, 0)),
    )
    return pl.pallas_call(
        functools.partial(_cmp_sample_kernel, n_group),
        grid_spec=grid_spec,
        out_shape=jax.ShapeDtypeStruct((2, B, n_pages * per_page, 2 * LANES), F32),
        compiler_params=_params("parallel", "arbitrary"),
        name="cmp_sample",
    )(page_table, *([cache5] * n_group), w_cmp)


def _compressed_kv(fs_ref, bias_ref, tabs):
    n = fs_ref.shape[2]
    out = []
    for s in range(2):
        fs = fs_ref[s, 0]
        out.append(fs[:, 0:LANES] + pltpu.roll(fs[:, LANES:2 * LANES], n - 1, 0) + bias_ref[s:s + 1, :])
    kc = _rope128(out[0], *tabs)
    return kc.astype(BF16), out[1].astype(BF16)


def _masked_softmax(s, valid):
    m = jnp.max(jnp.where(valid, s, NEG), axis=-1, keepdims=True)
    e = jnp.where(valid, jnp.exp(s - m), 0.0)
    l = jnp.sum(e, axis=-1, keepdims=True)
    return e / jnp.where(l > 0.0, l, 1.0)


def _top_blocks(score, n_top):
    lane = lax.broadcasted_iota(jnp.int32, score.shape, 1).astype(F32)
    picks = []
    for _ in range(n_top):
        mx = jnp.max(score, axis=-1, keepdims=True)
        idx = jnp.min(jnp.where(score == mx, lane, float(score.shape[1])), axis=-1, keepdims=True)
        picks.append(idx)
        score = jnp.where(lane == idx, NEG, score)
    return picks, lane


def _nsa_prompt_kernel(tk, q_ref, misc_ref, fs_ref, cb_ref, kc_c, kc_s1, kc_s2, ov_ref, ex_ref,
                       ks_ref, vs_ref, kw_ref, vw_ref, oa_ref,
                       kc_s, vc_s, m_s, l_s, acc_s):
    T = ks_ref.shape[1]
    n_cmp = fs_ref.shape[2]
    n_sel = ov_ref.shape[1]
    nq = Q_BLOCK
    rows = A_REP * nq
    qi = pl.program_id(1)
    q0 = qi * nq

    @pl.when(qi == 0)
    def _():
        kc, vc = _compressed_kv(fs_ref, cb_ref, (kc_c[...], kc_s1[...], kc_s2[...]))
        kc_s[...] = kc
        vc_s[...] = vc

    t_all = q0 + (lax.broadcasted_iota(jnp.int32, (rows, 1), 0) & (nq - 1))
    t_pos = t_all[0:nq]
    gates = misc_ref[0]

    cmp_end = lax.broadcasted_iota(jnp.int32, (rows, n_cmp), 1) * D_CMP + (L_CMP - 1)
    c_valid = cmp_end <= t_all
    blk = lax.broadcasted_iota(jnp.int32, (nq, n_sel), 1)
    cur = t_pos // L_SEL
    forced = (blk == 0) | (blk == cur) | (blk == cur - 1)
    started = blk * L_SEL <= t_pos
    w_start = jnp.maximum(q0 - WINDOW, 0)
    w_len = WINDOW + nq
    w_pos = w_start + lax.broadcasted_iota(jnp.int32, (rows, w_len), 1)
    rel = t_all - w_pos
    w_valid = (rel >= 0) & (rel < WINDOW)
    n_tiles = (q0 + nq + tk - 1) // tk

    outs = []
    for g in range(A_KV):
        qg = q_ref[g * A_REP:(g + 1) * A_REP].reshape(rows, LANES)
        p_c = _masked_softmax(_dot_nt(qg, kc_s[...]), c_valid)
        o_c = _dot(p_c.astype(BF16), vc_s[...])
        p_sum = jnp.sum(p_c.reshape(A_REP, nq, n_cmp), axis=0)
        imp = _dot_exact(p_sum, ov_ref[...])
        score = jnp.where(forced, FORCE_SCORE, jnp.where(started, imp, -1.0))
        picks, lane = _top_blocks(score, min(N_TOP, n_sel))
        sel = jnp.zeros((nq, n_sel), F32)
        for idx in picks:
            sel = jnp.where(lane == idx, 1.0, sel)
        sel = sel.astype(BF16)

        m_s[...] = jnp.full((rows, 1), NEG, F32)
        l_s[...] = jnp.zeros((rows, 1), F32)
        acc_s[...] = jnp.zeros((rows, LANES), F32)

        def tile(kt, carry):
            k0 = pl.multiple_of(kt * tk, tk)
            s = _dot_nt(qg, ks_ref[0, pl.ds(k0, tk), :])
            chosen = _dot(sel, ex_ref[kt])
            chosen = jnp.concatenate([chosen] * A_REP, axis=0)
            s_pos = k0 + lax.broadcasted_iota(jnp.int32, (rows, tk), 1)
            valid = jnp.where(s_pos <= t_all, chosen, 0.0) > 0.5
            m_old = m_s[...]
            m_new = jnp.maximum(m_old, jnp.max(jnp.where(valid, s, NEG), axis=-1, keepdims=True))
            e = jnp.where(valid, jnp.exp(s - m_new), 0.0)
            a = jnp.exp(m_old - m_new)
            l_s[...] = a * l_s[...] + jnp.sum(e, axis=-1, keepdims=True)
            acc_s[...] = a * acc_s[...] + _dot(e.astype(BF16), vs_ref[0, pl.ds(k0, tk), :])
            m_s[...] = m_new
            return carry

        lax.fori_loop(0, n_tiles, tile, 0)
        l = l_s[...]
        o_s = acc_s[...] / jnp.where(l > 0.0, l, 1.0)

        ws = pl.multiple_of(w_start, nq)
        p_w = _masked_softmax(_dot_nt(qg, kw_ref[0, pl.ds(ws, w_len), :]), w_valid)
        o_w = _dot(p_w.astype(BF16), vw_ref[0, pl.ds(ws, w_len), :])

        for r in range(A_REP):
            base = 2 * M_HEADS + (g * A_REP + r) * N_BRANCH
            rs = slice(r * nq, (r + 1) * nq)
            outs.append(gates[:, base:base + 1] * o_c[rs] + gates[:, base + 1:base + 2] * o_s[rs]
                        + gates[:, base + 2:base + 3] * o_w[rs])

    lane_o = lax.broadcasted_iota(jnp.int32, (nq, LANES), 1)
    for j in range(A_HEADS // 2):
        grp = (2 * j) // A_REP
        a, b = outs[2 * j], outs[2 * j + 1]
        if grp == 0:
            pair = jnp.where(lane_o < A_DIM, a, pltpu.roll(b, A_DIM, 1))
        else:
            pair = jnp.where(lane_o < A_DIM, pltpu.roll(a, A_DIM, 1), b)
        oa_ref[0, :, j * LANES:(j + 1) * LANES] = pair.astype(BF16)


def _nsa_prompt(q8, misc, fs, cbias, kc_tabs, ov, expand, kv16, B, T, tk):
    nqb = T // Q_BLOCK
    n_cmp = fs.shape[2]
    n_sel = ov.shape[1]
    rows = A_REP * Q_BLOCK
    const = lambda a: pl.BlockSpec(a.shape, lambda b, i: (0,) * a.ndim)
    seq = lambda j: pl.BlockSpec((1, T, LANES), lambda b, i: (b, 0, j))
    return pl.pallas_call(
        functools.partial(_nsa_prompt_kernel, tk),
        grid=(B, nqb),
        in_specs=[pl.BlockSpec((A_HEADS, Q_BLOCK, LANES), lambda b, i: (0, b * nqb + i, 0)),
                  pl.BlockSpec((1, Q_BLOCK, LANES), lambda b, i: (b, i, 0)),
                  pl.BlockSpec((2, 1, n_cmp, 2 * LANES), lambda b, i: (0, b, 0, 0)),
                  const(cbias), const(kc_tabs[0]), const(kc_tabs[1]), const(kc_tabs[2]),
                  const(ov), const(expand), seq(0), seq(1), seq(2), seq(3)],
        out_specs=pl.BlockSpec((1, Q_BLOCK, A_WIDTH), lambda b, i: (b, i, 0)),
        out_shape=jax.ShapeDtypeStruct((B, T, A_WIDTH), BF16),
        scratch_shapes=[pltpu.VMEM((n_cmp, LANES), BF16), pltpu.VMEM((n_cmp, LANES), BF16),
                        pltpu.VMEM((rows, 1), F32), pltpu.VMEM((rows, 1), F32),
                        pltpu.VMEM((rows, LANES), F32)],
        compiler_params=_params("parallel", "arbitrary"),
        name="nsa_prompt",
    )(q8, misc, fs, cbias, *kc_tabs, ov, expand, kv16, kv16, kv16, kv16)


def _nsa_sample_select_kernel(q_pos, n_sel, q_ref, fs_ref, cb_ref, kc_c, kc_s1, kc_s2, ov_ref, oc_ref, idx_ref):
    n_cmp = fs_ref.shape[2]
    kc, vc = _compressed_kv(fs_ref, cb_ref, (kc_c[...], kc_s1[...], kc_s2[...]))
    q = q_ref[0]
    cmp_end = lax.broadcasted_iota(jnp.int32, (A_HEADS, n_cmp), 1) * D_CMP + (L_CMP - 1)
    p_c = _masked_softmax(_dot_nt(q, kc), cmp_end <= q_pos)
    oc_ref[0] = _dot(p_c.astype(BF16), vc)
    head = lax.broadcasted_iota(jnp.int32, (A_HEADS, n_cmp), 0)
    p_sum = jnp.zeros((A_HEADS, n_cmp), F32)
    for g in range(A_KV):
        pg = jnp.sum(jnp.where(head // A_REP == g, p_c, 0.0), axis=0, keepdims=True)
        p_sum = jnp.where(head == g, pg, p_sum)
    imp = _dot_exact(p_sum, ov_ref[...])
    n_pad = ov_ref.shape[1]
    blk = lax.broadcasted_iota(jnp.int32, (A_HEADS, n_pad), 1)
    cur = q_pos // L_SEL
    forced = (blk == 0) | (blk == cur) | (blk == cur - 1)
    score = jnp.where(forced, FORCE_SCORE, jnp.where(blk * L_SEL <= q_pos, imp, -1.0))
    score = jnp.where(blk < n_sel, score, 2 * NEG)
    picks, _ = _top_blocks(score, min(N_TOP, n_sel))
    lane = lax.broadcasted_iota(jnp.int32, (A_HEADS, LANES), 1)
    out = jnp.zeros((A_HEADS, LANES), jnp.int32)
    for j, idx in enumerate(picks):
        out = jnp.where(lane == j, idx.astype(jnp.int32), out)
    idx_ref[0] = out


def _nsa_sample_select(q8, fs, cbias, kc_tabs, ov, q_pos, n_sel):
    B = q8.shape[0]
    n_cmp = fs.shape[2]
    const = lambda a: pl.BlockSpec(a.shape, lambda b: (0,) * a.ndim)
    per_b = pl.BlockSpec((1, A_HEADS, LANES), lambda b: (b, 0, 0))
    return pl.pallas_call(
        functools.partial(_nsa_sample_select_kernel, q_pos, n_sel),
        grid=(B,),
        in_specs=[per_b, pl.BlockSpec((2, 1, n_cmp, 2 * LANES), lambda b: (0, b, 0, 0)),
                  const(cbias), const(kc_tabs[0]), const(kc_tabs[1]), const(kc_tabs[2]), const(ov)],
        out_specs=(per_b, per_b),
        out_shape=(jax.ShapeDtypeStruct((B, A_HEADS, LANES), F32),
                   jax.ShapeDtypeStruct((B, A_HEADS, LANES), jnp.int32)),
        compiler_params=_params("parallel"),
        name="nsa_sample_select",
    )(q8, fs, cbias, *kc_tabs, ov)


def _nsa_sample_attend_kernel(q_pos, pt_ref, ix_ref, q_ref, new_ref, oc_ref, misc_ref, win_ref,
                              k0_ref, k1_ref, v0_ref, v1_ref, oa_ref, nwin_ref, m_s, l_s, acc_s):
    b, j = pl.program_id(0), pl.program_id(1)
    q = q_ref[0]
    head = lax.broadcasted_iota(jnp.int32, (A_HEADS, 1), 0)
    new = new_ref[0]

    @pl.when(j == 0)
    def _():
        m_s[...] = jnp.full(m_s.shape, NEG, F32)
        l_s[...] = jnp.zeros(l_s.shape, F32)
        acc_s[...] = jnp.zeros(acc_s.shape, F32)

    row = lax.broadcasted_iota(jnp.int32, (L_SEL, LANES), 0)
    for g, (k_ref, v_ref) in enumerate(((k0_ref, v0_ref), (k1_ref, v1_ref))):
        blk = ix_ref[b, g, j]
        fresh = blk * L_SEL >= q_pos
        k_new = jnp.where(row == 0, jnp.broadcast_to(new[0:1, :], (L_SEL, LANES)), 0.0)
        v_new = jnp.where(row == 0, jnp.broadcast_to(new[1:2, :], (L_SEL, LANES)), 0.0)
        kb = jnp.where(fresh, k_new, k_ref[...]).astype(BF16)
        vb = jnp.where(fresh, v_new, v_ref[...]).astype(BF16)
        s = _dot_nt(q, kb)
        s_pos = blk * L_SEL + lax.broadcasted_iota(jnp.int32, (A_HEADS, L_SEL), 1)
        valid = (s_pos <= q_pos) & (head // A_REP == g)
        m_old = m_s[...]
        m_new = jnp.maximum(m_old, jnp.max(jnp.where(valid, s, NEG), axis=-1, keepdims=True))
        e = jnp.where(valid, jnp.exp(s - m_new), 0.0)
        a = jnp.exp(m_old - m_new)
        l_s[...] = a * l_s[...] + jnp.sum(e, axis=-1, keepdims=True)
        acc_s[...] = a * acc_s[...] + _dot(e.astype(BF16), vb)
        m_s[...] = m_new

    @pl.when(j == pl.num_programs(1) - 1)
    def _():
        l = l_s[...]
        o_s = acc_s[...] / jnp.where(l > 0.0, l, 1.0)
        wb = win_ref.shape[1]
        kw, vw = win_ref[0], win_ref[1]
        s_old = _dot_nt(q, kw.astype(BF16))
        s_new = _dot_nt(q, jnp.broadcast_to(new[2:3, :], (A_HEADS, LANES)).astype(BF16))[:, 0:1]
        rel = wb - lax.broadcasted_iota(jnp.int32, (A_HEADS, wb), 1)
        ok = (rel >= 0) & (rel < WINDOW) & (q_pos - rel >= 0)
        m = jnp.maximum(jnp.max(jnp.where(ok, s_old, NEG), axis=-1, keepdims=True), s_new)
        e_old = jnp.where(ok, jnp.exp(s_old - m), 0.0)
        e_new = jnp.exp(s_new - m)
        den = jnp.sum(e_old, axis=-1, keepdims=True) + e_new
        p_old, p_new = (e_old / den).astype(BF16), (e_new / den).astype(BF16).astype(F32)
        o_w = _dot(p_old, vw.astype(BF16)) + p_new * new[3:4, :].astype(BF16).astype(F32)
        gates = misc_ref[0]
        oa_ref[0] = gates[:, 0:1] * oc_ref[0] + gates[:, 1:2] * o_s + gates[:, 2:3] * o_w
        last = lax.broadcasted_iota(jnp.int32, (wb, LANES), 0) == wb - 1
        nwin_ref[0] = jnp.where(last, jnp.broadcast_to(new[2:3, :], (wb, LANES)), pltpu.roll(kw, wb - 1, 0))
        nwin_ref[1] = jnp.where(last, jnp.broadcast_to(new[3:4, :], (wb, LANES)), pltpu.roll(vw, wb - 1, 0))


def _nsa_sample_attend(page_table, idx, q8, new_rows, o_c, gate_rows, win5, cache5, layer, q_pos):
    B, n_pages = page_table.shape
    n_top = idx.shape[2]
    page = cache5.shape[3]
    per_page = page // L_SEL
    wb = win5.shape[3]

    def blk_spec(g, slot):
        def index(b, j, pt, ix):
            blk = ix[b, g, j]
            pg = jnp.minimum(blk // per_page, n_pages - 1)
            return (pt[b, pg], layer, slot, blk % per_page, 0)
        return pl.BlockSpec((None, None, None, L_SEL, LANES), index)

    per_b = pl.BlockSpec((1, A_HEADS, LANES), lambda b, j, pt, ix: (b, 0, 0))
    win_spec = pl.BlockSpec((None, None, 2, wb, LANES), lambda b, j, pt, ix: (b, layer, 0, 0, 0))
    grid_spec = pltpu.PrefetchScalarGridSpec(
        num_scalar_prefetch=2,
        grid=(B, n_top),
        in_specs=[per_b, per_b, per_b, per_b, win_spec,
                  blk_spec(0, 2), blk_spec(1, 2), blk_spec(0, 3), blk_spec(1, 3)],
        out_specs=(per_b, pl.BlockSpec((None, 2, wb, LANES), lambda b, j, pt, ix: (b, 0, 0, 0))),
        scratch_shapes=[pltpu.VMEM((A_HEADS, 1), F32), pltpu.VMEM((A_HEADS, 1), F32),
                        pltpu.VMEM((A_HEADS, LANES), F32)],
    )
    return pl.pallas_call(
        functools.partial(_nsa_sample_attend_kernel, q_pos),
        grid_spec=grid_spec,
        out_shape=(jax.ShapeDtypeStruct((B, A_HEADS, LANES), F32),
                   jax.ShapeDtypeStruct((B, 2, wb, LANES), F32)),
        compiler_params=_params("parallel", "arbitrary"),
        name="nsa_sample_attend",
    )(page_table, idx, q8, new_rows, o_c, gate_rows, win5, cache5, cache5, cache5, cache5)


def _ffn_kernel(final, x_ref, hm_ref, oa_ref, wo_ref, nf_ref, wg_ref, wu_ref, wd_ref, nfin_ref, out_ref,
                x1_s, xn_s, acc_s):
    j = pl.program_id(1)

    @pl.when(j == 0)
    def _():
        half = hm_ref.shape[1]
        x1 = x_ref[...] + _dot(hm_ref[...], wo_ref[0:half, :]) + _dot(oa_ref[...], wo_ref[half:, :])
        x1_s[...] = x1
        ms = jnp.mean(x1 * x1, axis=-1, keepdims=True)
        xn_s[...] = ((x1 * lax.rsqrt(ms + EPS)) * nf_ref[...]).astype(BF16)
        acc_s[...] = jnp.zeros(acc_s.shape, F32)

    xn = xn_s[...]
    gate = _dot(xn, wg_ref[...])
    act = (gate * _sigmoid(gate)) * _dot(xn, wu_ref[...])
    acc_s[...] += _dot(act.astype(BF16), wd_ref[...])

    @pl.when(j == pl.num_programs(1) - 1)
    def _():
        y = x1_s[...] + acc_s[...]
        if final:
            ms = jnp.mean(y * y, axis=-1, keepdims=True)
            y = (y * lax.rsqrt(ms + EPS)) * nfin_ref[...]
        out_ref[...] = y


def _ffn(x, hm, oa, wo, nf, wg, wu, wd, nfin, final, tm, tf):
    M, D = x.shape
    F = wg.shape[1]
    row = lambda w: pl.BlockSpec((tm, w), lambda i, j: (i, 0))
    const = lambda a: pl.BlockSpec(a.shape, lambda i, j: (0,) * a.ndim)
    return pl.pallas_call(
        functools.partial(_ffn_kernel, final),
        grid=(M // tm, F // tf),
        in_specs=[row(D), row(hm.shape[1]), row(oa.shape[1]), const(wo), const(nf),
                  pl.BlockSpec((D, tf), lambda i, j: (0, j)), pl.BlockSpec((D, tf), lambda i, j: (0, j)),
                  pl.BlockSpec((tf, D), lambda i, j: (j, 0)), const(nfin)],
        out_specs=row(D),
        out_shape=jax.ShapeDtypeStruct((M, D), F32),
        scratch_shapes=[pltpu.VMEM((tm, D), F32), pltpu.VMEM((tm, D), BF16), pltpu.VMEM((tm, D), F32)],
        compiler_params=_params("parallel", "arbitrary"),
        name="ffn",
    )(x, hm, oa, wo, nf, wg, wu, wd, nfin)


def _permute_w_in(w):
    a = 4 * M_WIDTH
    b = a + 2 * M_HEADS
    c = b + A_WIDTH + 6 * KV_WIDTH
    pad = jnp.zeros((w.shape[0], LANES - N_MISC), w.dtype)
    return jnp.concatenate([w[:, :a], w[:, b:c], w[:, a:b], w[:, c:], pad], axis=1).astype(BF16)


def _cmp_weights(wk, wv):
    def one(w):
        z = jnp.zeros((D_CMP, A_DIM, A_DIM), w.dtype)
        halves = []
        for part in (w[:D_CMP], w[D_CMP:]):
            top = jnp.concatenate([part, z], axis=2)
            bot = jnp.concatenate([z, part], axis=2)
            halves.append(jnp.concatenate([top, bot], axis=1))
        return jnp.concatenate(halves, axis=2)
    return jnp.stack([one(wk), one(wv)]).astype(BF16)


def _overlap(n_cmp_pad, n_sel_pad, n_cmp, n_sel):
    lo_c = np.arange(n_cmp_pad)[:, None] * D_CMP
    lo_s = np.arange(n_sel_pad)[None, :] * L_SEL
    ov = np.clip(np.minimum(lo_c + L_CMP, lo_s + L_SEL) - np.maximum(lo_c, lo_s), 0, None).astype(np.float32) / L_CMP
    ov[n_cmp:, :] = 0.0
    ov[:, n_sel:] = 0.0
    return jnp.asarray(ov)


def _pick(n, prefs):
    for p in prefs:
        if n % p == 0:
            return p
    return n


def kernel(x_prompt, x_sample, cache_nsa_kv, page_table, state_win_kv, state_mlstm_c, state_mlstm_n, state_mlstm_m, norm_mix, w_in, b_if, m_norm, w_cmp_k, b_cmp_k, w_cmp_v, b_cmp_v, w_out, norm_ffn, w_gate, w_up, w_down, norm_final):
    B, T, D = x_prompt.shape
    Bs, Ts, _ = x_sample.shape
    depth = w_in.shape[0]
    n_pool, _, n_slots, page = cache_nsa_kv.shape[:4]
    n_pages = page_table.shape[1]
    past = n_pages * page
    wb = state_win_kv.shape[3]
    assert Ts == 1 and T % Q_BLOCK == 0 and T >= WINDOW + Q_BLOCK and wb == WINDOW
    assert past % page == 0 and page % L_SEL == 0 and page % D_CMP == 0
    assert ((past + Ts) // D_CMP) * D_CMP <= past

    Mp = B * T
    tm_p = _pick(Mp, (512, 256, 128))
    tm_f = _pick(Mp, (1024, 512, 256, 128))
    ff = w_gate.shape[2]
    tf = _pick(ff, (256, 128))
    l_chunk = _pick(T, (256, 128, 64))
    ls_pad = 16
    tk = _pick(T, (512, 256, 128))
    n_group = _pick(n_pages, (16, 8, 4, 2, 1))
    tr = _pick(T, (2048, 1024, 512, 256))

    n_chunk_p = T // D_CMP
    n_cmp_p, n_sel_p = n_chunk_p - 1, -(-T // L_SEL)
    tabs_p = _rope_tables(jnp.arange(T))
    kc_tabs_p = _rope_tables(jnp.arange(n_chunk_p) * D_CMP)
    ov_p = _overlap(n_chunk_p, n_sel_p, n_cmp_p, n_sel_p)
    expand = (jnp.arange(T).reshape(T // tk, 1, tk) // L_SEL == jnp.arange(n_sel_p)[None, :, None]).astype(BF16)
    n_chunk_s = (past + Ts) // D_CMP
    n_cmp_s, n_sel_s = n_chunk_s - 1, -(-(past + Ts) // L_SEL)
    n_sel_pad = -(-n_sel_s // LANES) * LANES
    tabs_s = _rope_tables(jnp.full((Bs,), past))
    kc_tabs_s = _rope_tables(jnp.arange(n_chunk_s) * D_CMP)
    ov_s = _overlap(n_chunk_s, n_sel_pad, n_cmp_s, n_sel_s)
    cache5 = cache_nsa_kv.reshape(n_pool, depth, n_slots, page, KV_WIDTH)
    win5 = state_win_kv.reshape(Bs, depth, 2, wb, KV_WIDTH)

    xp = x_prompt.reshape(Mp, D)
    xs = x_sample.reshape(Bs, D)
    rows_p, rows_s, win_p, win_s, c_p, c_s, n_p, n_s, m_p, m_s = ([] for _ in range(10))
    y_p = y_s = None
    nfin = norm_final.reshape(1, D)
    for l in range(depth):
        w_perm = _permute_w_in(w_in[l])
        gain = norm_mix[l].reshape(1, D)
        bias = jnp.zeros((1, LANES), F32).at[0, :2 * M_HEADS].set(b_if[l].reshape(-1))
        mg = m_norm[l].reshape(1, M_WIDTH)
        w_cmp = _cmp_weights(w_cmp_k[l], w_cmp_v[l])
        cbias = jnp.stack([jnp.tile(b_cmp_k[l], A_KV), jnp.tile(b_cmp_v[l], A_KV)])
        wo, nf = w_out[l].astype(BF16), norm_ffn[l].reshape(1, D)
        wg, wu, wd = w_gate[l].astype(BF16), w_up[l].astype(BF16), w_down[l].astype(BF16)
        final = l == depth - 1

        qkv, og, q8, rows, win, misc, kv16 = _inproj(xp, gain, w_perm, bias, tabs_p, tm_p)
        misc3 = misc.reshape(B, T, LANES)
        gt = misc3[:, :, :2 * M_HEADS].transpose(0, 2, 1)
        hm, c_new, n_new, m_new = _mlstm(
            qkv.reshape(B, T, -1), og.reshape(B, T, -1), misc3, gt, mg,
            jnp.zeros((B, M_HEADS, M_DIM, M_DIM), F32), jnp.zeros((B, M_HEADS, M_DIM), F32),
            jnp.full((B, M_HEADS, M_DIM), NEG, F32), l_chunk)
        fs = _cmp_prompt(rows, w_cmp, tr).reshape(2, B, n_chunk_p, 2 * LANES)
        oa = _nsa_prompt(q8, misc3, fs, cbias, kc_tabs_p, ov_p, expand, kv16.reshape(B, T, -1), B, T, tk)
        xp_new = _ffn(xp, hm.reshape(Mp, -1), oa.reshape(Mp, -1), wo, nf, wg, wu, wd, nfin, final, tm_f, tf)
        if final:
            y_p = xp_new
        xp = xp_new
        rows_p.append(rows.reshape(B, T, n_slots, A_KV, A_DIM))
        win_p.append(win.reshape(B, T, 2, A_KV, A_DIM)[:, T - wb:].transpose(0, 2, 1, 3, 4))
        c_p.append(c_new); n_p.append(n_new); m_p.append(m_new[:, :, 0])

        qkv, og, q8, rows, win, misc, kv16 = _inproj(xs, gain, w_perm, bias, tabs_s, Bs)
        pad_t = lambda a: jnp.pad(a[:, None, :], ((0, 0), (0, ls_pad - 1), (0, 0)))
        inert = jnp.zeros((ls_pad, LANES), F32).at[1:, :M_HEADS].set(NEG)
        misc_pad = pad_t(misc) + inert[None]
        gt = misc_pad[:, :, :2 * M_HEADS].transpose(0, 2, 1)
        m0 = jnp.broadcast_to(state_mlstm_m[:, l, :, None].astype(F32), (Bs, M_HEADS, M_DIM))
        hm, c_new, n_new, m_new = _mlstm(
            pad_t(qkv), pad_t(og), misc_pad, gt, mg,
            state_mlstm_c[:, l].astype(F32), state_mlstm_n[:, l].astype(F32), m0, ls_pad)
        hm = hm[:, 0]
        fs = _cmp_sample(cache5, page_table, l, w_cmp, n_group)
        q8s = q8.transpose(1, 0, 2)
        o_c, idx = _nsa_sample_select(q8s, fs, cbias, kc_tabs_s, ov_s, past, n_sel_s)
        idx = idx[:, :A_KV, :min(N_TOP, n_sel_s)]
        new_rows = jnp.pad(jnp.stack([rows[:, 2 * LANES:3 * LANES], rows[:, 3 * LANES:], win[:, :LANES],
                                      win[:, LANES:]], axis=1), ((0, 0), (0, A_HEADS - 4), (0, 0)))
        gate_rows = jnp.pad(misc[:, 2 * M_HEADS:N_MISC].reshape(Bs, A_HEADS, N_BRANCH),
                            ((0, 0), (0, 0), (0, LANES - N_BRANCH)))
        oa8, nwin = _nsa_sample_attend(page_table, idx, q8s, new_rows, o_c, gate_rows, win5, cache5, l, past)
        grp = (jnp.arange(A_HEADS) // A_REP)[None, :, None, None]
        oa = jnp.take_along_axis(oa8.reshape(Bs, A_HEADS, A_KV, A_DIM), jnp.broadcast_to(grp, (Bs, A_HEADS, 1, A_DIM)),
                                 axis=2).reshape(Bs, A_WIDTH).astype(BF16)
        xs_new = _ffn(xs, hm, oa, wo, nf, wg, wu, wd, nfin, final, Bs, tf)
        if final:
            y_s = xs_new
        xs = xs_new
        rows_s.append(rows.reshape(Bs, Ts, n_slots, A_KV, A_DIM))
        win_s.append(nwin.reshape(Bs, 2, wb, A_KV, A_DIM))
        c_s.append(c_new); n_s.append(n_new); m_s.append(m_new[:, :, 0])

    return (y_p.reshape(B, T, D), y_s.reshape(Bs, Ts, D),
            jnp.stack(rows_p, axis=2), jnp.stack(rows_s, axis=2),
            jnp.stack(win_p, axis=1), jnp.stack(win_s, axis=1),
            jnp.stack(c_p, axis=1), jnp.stack(c_s, axis=1),
            jnp.stack(n_p, axis=1), jnp.stack(n_s, axis=1),
            jnp.stack(m_p, axis=1), jnp.stack(m_s, axis=1))
```

```python
import functools

import jax
import jax.numpy as jnp
import numpy as np
from jax import lax
from jax.experimental import pallas as pl
from jax.experimental.pallas import tpu as pltpu

F32 = jnp.float32
BF16 = jnp.bfloat16

M_HEADS = 4
M_DIM = 128
M_WIDTH = M_HEADS * M_DIM
A_HEADS = 8
A_DIM = 64
A_KV = 2
A_REP = A_HEADS // A_KV
A_WIDTH = A_HEADS * A_DIM
KV_WIDTH = A_KV * A_DIM
L_CMP = 32
D_CMP = 16
L_SEL = 64
N_TOP = 16
WINDOW = 512
Q_BLOCK = 128
N_BRANCH = 3
ROPE_THETA = 500000.0
ROPE_DIM = A_DIM // 4
EPS = 1e-6
NEG = -1e30
FORCE_SCORE = 1e4
LANES = 128
N_MISC = 2 * M_HEADS + N_BRANCH * A_HEADS
D_PERM = 4 * M_WIDTH + A_WIDTH + 6 * KV_WIDTH + LANES
VMEM_LIMIT = 56 * 1024 * 1024

_NT = (((1,), (1,)), ((), ()))
_TN = (((0,), (0,)), ((), ()))


def _params(*sem):
    return pltpu.CompilerParams(dimension_semantics=sem, vmem_limit_bytes=VMEM_LIMIT)


def _sigmoid(x):
    return 1.0 / (1.0 + jnp.exp(-x))


def _dot(a, b):
    return jnp.dot(a, b, preferred_element_type=F32)


def _dot_exact(a, b):
    return jnp.dot(a, b, preferred_element_type=F32, precision=lax.Precision.HIGHEST)


def _dot_nt(a, b):
    return lax.dot_general(a, b, _NT, preferred_element_type=F32)


def _rope128(v, c, s1, s2):
    half = ROPE_DIM // 2
    return v * c + pltpu.roll(v, LANES - half, 1) * s1 + pltpu.roll(v, half, 1) * s2


def _rope_tables(pos):
    half = ROPE_DIM // 2
    inv = ROPE_THETA ** (-jnp.arange(half, dtype=F32) / half)
    ang = pos.astype(F32)[:, None] * inv[None, :]
    cos, sin = jnp.cos(ang), jnp.sin(ang)
    n = pos.shape[0]
    one = jnp.ones((n, A_DIM - ROPE_DIM), F32)
    zero = jnp.zeros((n, A_DIM - ROPE_DIM), F32)
    zh = jnp.zeros((n, half), F32)
    c = jnp.concatenate([cos, cos, one], axis=1)
    s1 = jnp.concatenate([-sin, zh, zero], axis=1)
    s2 = jnp.concatenate([zh, sin, zero], axis=1)
    tile = lambda a: jnp.concatenate([a, a], axis=1)
    return tile(c), tile(s1), tile(s2)


def _inproj_kernel(x_ref, g_ref, w_ref, bias_ref, rc_ref, rs1_ref, rs2_ref,
                   om_ref, og_ref, oq_ref, orows_ref, owin_ref, omisc_ref, *attn_refs):
    x = x_ref[...]
    ms = jnp.mean(x * x, axis=-1, keepdims=True)
    xn = ((x * lax.rsqrt(ms + EPS)) * g_ref[...]).astype(BF16)
    c, s1, s2 = rc_ref[...], rs1_ref[...], rs2_ref[...]
    lane = lax.broadcasted_iota(jnp.int32, (x.shape[0], LANES), 1)

    def proj(a, b):
        return _dot(xn, w_ref[:, a:b])

    om_ref[:, 0:M_WIDTH] = (proj(0, M_WIDTH) * (M_DIM ** -0.5)).astype(BF16)
    om_ref[:, M_WIDTH:3 * M_WIDTH] = proj(M_WIDTH, 3 * M_WIDTH).astype(BF16)
    og_ref[...] = _sigmoid(proj(3 * M_WIDTH, 4 * M_WIDTH))

    off = 4 * M_WIDTH
    pq = proj(off, off + A_WIDTH)
    for j in range(A_WIDTH // LANES):
        blk = _rope128(pq[:, j * LANES:(j + 1) * LANES], c, s1, s2) * (A_DIM ** -0.5)
        swapped = pltpu.roll(blk, A_DIM, 1)
        for e in range(2):
            hd = 2 * j + e
            grp = hd // A_REP
            src = blk if e == grp else swapped
            keep = (lane // A_DIM) == grp
            oq_ref[hd] = jnp.where(keep, src, 0.0).astype(BF16)

    off += A_WIDTH
    pk = proj(off, off + 6 * KV_WIDTH)
    kcr, vcr, ksl, vsl, kw, vw = [pk[:, i * LANES:(i + 1) * LANES] for i in range(6)]
    ksl = _rope128(ksl, c, s1, s2)
    kw = _rope128(kw, c, s1, s2)
    for i, a in enumerate((kcr, vcr, ksl, vsl)):
        orows_ref[:, i * LANES:(i + 1) * LANES] = a
    owin_ref[:, 0:LANES] = kw
    owin_ref[:, LANES:2 * LANES] = vw
    if attn_refs:
        okk_ref, ovst_ref, ovwt_ref = attn_refs
        okk_ref[:, 0:LANES] = ksl.astype(BF16)
        okk_ref[:, LANES:2 * LANES] = kw.astype(BF16)
        ovst_ref[0] = vsl.T.astype(BF16)
        for i in range(ovwt_ref.shape[0]):
            ovwt_ref[i] = vw[i * LANES:(i + 1) * LANES].T.astype(BF16)

    off += 6 * KV_WIDTH
    pm = proj(off, off + LANES) + bias_ref[...]
    log_sig = -(jnp.maximum(-pm, 0.0) + jnp.log(1.0 + jnp.exp(-jnp.abs(pm))))
    omisc_ref[...] = jnp.where(lane < M_HEADS, pm, jnp.where(lane < 2 * M_HEADS, log_sig, _sigmoid(pm)))


def _inproj(x, gain, w_perm, bias, tabs, tm, attn):
    M, D = x.shape
    nt = tabs[0].shape[0] // tm
    row = lambda i: (i, 0)
    tab = pl.BlockSpec((tm, LANES), lambda i: (i % nt, 0))
    full = lambda a: pl.BlockSpec(a.shape, lambda i: (0,) * a.ndim)
    out_shape = (
        jax.ShapeDtypeStruct((M, 3 * M_WIDTH), BF16),
        jax.ShapeDtypeStruct((M, M_WIDTH), F32),
        jax.ShapeDtypeStruct((A_HEADS, M, LANES), BF16),
        jax.ShapeDtypeStruct((M, 4 * KV_WIDTH), F32),
        jax.ShapeDtypeStruct((M, 2 * KV_WIDTH), F32),
        jax.ShapeDtypeStruct((M, LANES), F32),
    )
    out_specs = (
        pl.BlockSpec((tm, 3 * M_WIDTH), row),
        pl.BlockSpec((tm, M_WIDTH), row),
        pl.BlockSpec((A_HEADS, tm, LANES), lambda i: (0, i, 0)),
        pl.BlockSpec((tm, 4 * KV_WIDTH), row),
        pl.BlockSpec((tm, 2 * KV_WIDTH), row),
        pl.BlockSpec((tm, LANES), row),
    )
    if attn:
        out_shape += (jax.ShapeDtypeStruct((M, 2 * KV_WIDTH), BF16),
                      jax.ShapeDtypeStruct((M // tm, KV_WIDTH, tm), BF16),
                      jax.ShapeDtypeStruct((M // LANES, KV_WIDTH, LANES), BF16))
        out_specs += (pl.BlockSpec((tm, 2 * KV_WIDTH), row),
                      pl.BlockSpec((1, KV_WIDTH, tm), lambda i: (i, 0, 0)),
                      pl.BlockSpec((tm // LANES, KV_WIDTH, LANES), lambda i: (i, 0, 0)))
    return pl.pallas_call(
        _inproj_kernel,
        grid=(M // tm,),
        in_specs=[pl.BlockSpec((tm, D), row), full(gain), full(w_perm), full(bias), tab, tab, tab],
        out_specs=out_specs,
        out_shape=out_shape,
        compiler_params=_params("parallel"),
        name="inproj",
    )(x, gain, w_perm, bias, *tabs)


def _mlstm_kernel(q_ref, k_ref, v_ref, og_ref, misc_ref, gt_ref, mg_ref, c0_ref, n0_ref, m0_ref,
                  hm_ref, c_ref, n_ref, m_ref):
    L = q_ref.shape[1]

    @pl.when(pl.program_id(1) == 0)
    def _():
        c_ref[...] = c0_ref[...]
        n_ref[...] = n0_ref[...]
        m_ref[...] = m0_ref[...]

    gc = misc_ref[0]
    gr = gt_ref[0]
    row = lax.broadcasted_iota(jnp.int32, (L, L), 0)
    col = lax.broadcasted_iota(jnp.int32, (L, L), 1)
    causal = row >= col
    b_col = _dot_exact(causal.astype(F32), gc)
    b_row = _dot_exact(gr, (row <= col).astype(F32))
    for h in range(M_HEADS):
        sl = slice(h * M_DIM, (h + 1) * M_DIM)
        q, k, v = q_ref[0, :, sl], k_ref[0, :, sl], v_ref[0, :, sl]
        c_prev, n_prev, m_prev = c_ref[0, h], n_ref[0, h:h + 1, :], m_ref[0, h:h + 1, 0:1]
        bc = b_col[:, M_HEADS + h:M_HEADS + h + 1]
        ic = gc[:, h:h + 1]
        br = b_row[M_HEADS + h:M_HEADS + h + 1, :]
        ir = gr[h:h + 1, :]
        d = jnp.where(causal, bc - br + ir, NEG)
        m_inter = bc + m_prev
        m_t = jnp.maximum(m_inter, jnp.max(d, axis=-1, keepdims=True))
        w_inter = jnp.exp(m_inter - m_t)
        p = _dot_nt(q, k) * jnp.exp(d - m_t)
        num = w_inter * _dot(q, c_prev.astype(BF16)) + _dot(p.astype(BF16), v)
        den = (w_inter * jnp.sum(q.astype(F32) * n_prev, axis=-1, keepdims=True)
               + jnp.sum(p, axis=-1, keepdims=True))
        hh = num / jnp.maximum(jnp.abs(den), jnp.exp(-m_t))
        m_new = m_t[L - 1:L, :]
        w_new = jnp.exp(bc[L - 1:L, :] - bc + ic - m_new)
        decay = jnp.exp(m_inter[L - 1:L, :] - m_new)
        kf, vf = k.astype(F32), v.astype(F32)
        c_ref[0, h] = decay * c_prev + lax.dot_general(k, (w_new * vf).astype(BF16), _TN,
                                                       preferred_element_type=F32)
        n_ref[0, h:h + 1, :] = decay * n_prev + jnp.sum(w_new * kf, axis=0, keepdims=True)
        m_ref[0, h:h + 1, :] = jnp.broadcast_to(m_new, (1, M_DIM))
        hn = hh * lax.rsqrt(jnp.mean(hh * hh, axis=-1, keepdims=True) + EPS)
        hm_ref[0, :, sl] = ((hn * mg_ref[:, sl]) * og_ref[0, :, sl]).astype(BF16)


def _mlstm(qkv, og, misc, gt, mg, c0, n0, m0, L):
    B, T, _ = qkv.shape
    seq = lambda j: pl.BlockSpec((1, L, M_WIDTH), lambda b, c: (b, c, j))
    st4 = pl.BlockSpec((1, M_HEADS, M_DIM, M_DIM), lambda b, c: (b, 0, 0, 0))
    st3 = pl.BlockSpec((1, M_HEADS, M_DIM), lambda b, c: (b, 0, 0))
    return pl.pallas_call(
        _mlstm_kernel,
        grid=(B, T // L),
        in_specs=[seq(0), seq(1), seq(2), seq(0),
                  pl.BlockSpec((1, L, LANES), lambda b, c: (b, c, 0)),
                  pl.BlockSpec((1, 2 * M_HEADS, L), lambda b, c: (b, 0, c)),
                  pl.BlockSpec((1, M_WIDTH), lambda b, c: (0, 0)),
                  st4, st3, st3],
        out_specs=(seq(0), st4, st3, st3),
        out_shape=(jax.ShapeDtypeStruct((B, T, M_WIDTH), BF16),
                   jax.ShapeDtypeStruct(c0.shape, F32),
                   jax.ShapeDtypeStruct(n0.shape, F32),
                   jax.ShapeDtypeStruct(m0.shape, F32)),
        compiler_params=_params("parallel", "arbitrary"),
        name="mlstm",
    )(qkv, qkv, qkv, og, misc, gt, mg, c0, n0, m0)


def _cmp_accumulate(load, w_ref, s):
    acc = None
    for r in range(D_CMP):
        term = _dot(load(r).astype(BF16), w_ref[s, r])
        acc = term if acc is None else acc + term
    return acc


def _cmp_prompt_kernel(k_ref, v_ref, w_ref, fs_ref):
    n = k_ref.shape[0] // D_CMP
    for s, ref in enumerate((k_ref, v_ref)):
        fs_ref[s] = _cmp_accumulate(lambda r: ref[pl.ds(r, n, stride=D_CMP), :], w_ref, s)


def _cmp_prompt(rows, w_cmp, tr):
    M = rows.shape[0]
    return pl.pallas_call(
        _cmp_prompt_kernel,
        grid=(M // tr,),
        in_specs=[pl.BlockSpec((tr, KV_WIDTH), lambda i: (i, 0)), pl.BlockSpec((tr, KV_WIDTH), lambda i: (i, 1)),
                  pl.BlockSpec(w_cmp.shape, lambda i: (0, 0, 0, 0))],
        out_specs=pl.BlockSpec((2, tr // D_CMP, 2 * LANES), lambda i: (0, i, 0)),
        out_shape=jax.ShapeDtypeStruct((2, M // D_CMP, 2 * LANES), F32),
        compiler_params=_params("parallel"),
        name="cmp_prompt",
    )(rows, rows, w_cmp)


def _cmp_sample_kernel(n_group, pt_ref, *refs):
    page_refs, w_ref, fs_ref, rows_s = refs[:n_group], refs[n_group], refs[n_group + 1], refs[n_group + 2]
    page = page_refs[0].shape[2]
    n = n_group * page // D_CMP
    for s in range(2):
        for i, p in enumerate(page_refs):
            rows_s[s, i * page:(i + 1) * page, :] = p[s].T
        fs_ref[s, 0] = _cmp_accumulate(lambda r: rows_s[s, pl.ds(r, n, stride=D_CMP), :], w_ref, s)


def _cmp_sample(cache_t, page_table, layer, w_cmp, n_group):
    B, n_pages = page_table.shape
    page = cache_t.shape[4]
    per_page = page // D_CMP

    def page_spec(i):
        return pl.BlockSpec((None, None, 2, KV_WIDTH, page),
                            lambda b, p, pt: (pt[b, p * n_group + i], layer, 0, 0, 0))

    grid_spec = pltpu.PrefetchScalarGridSpec(
        num_scalar_prefetch=1,
        grid=(B, n_pages // n_group),
        in_specs=[page_spec(i) for i in range(n_group)]
        + [pl.BlockSpec(w_cmp.shape, lambda b, p, pt: (0, 0, 0, 0))],
        out_specs=pl.BlockSpec((2, 1, n_group * per_page, 2 * LANES), lambda b, p, pt: (0, b, p, 0)),
        scratch_shapes=[pltpu.VMEM((2, n_group * page, KV_WIDTH), F32)],
    )
    return pl.pallas_call(
        functools.partial(_cmp_sample_kernel, n_group),
        grid_spec=grid_spec,
        out_shape=jax.ShapeDtypeStruct((2, B, n_pages * per_page, 2 * LANES), F32),
        compiler_params=_params("parallel", "arbitrary"),
        name="cmp_sample",
    )(page_table, *([cache_t] * n_group), w_cmp)


def _compressed_kv(fs_ref, bias_ref, tabs):
    n = fs_ref.shape[2]
    out = []
    for s in range(2):
        fs = fs_ref[s, 0]
        out.append(fs[:, 0:LANES] + pltpu.roll(fs[:, LANES:2 * LANES], n - 1, 0) + bias_ref[s:s + 1, :])
    return _rope128(out[0], *tabs), out[1]


def _masked_softmax(s, valid, axis):
    m = jnp.max(jnp.where(valid, s, NEG), axis=axis, keepdims=True)
    e = jnp.where(valid, jnp.exp(s - m), 0.0)
    l = jnp.sum(e, axis=axis, keepdims=True)
    return e * (1.0 / jnp.where(l > 0.0, l, 1.0))


def _bias_softmax(s, axis, keep=None):
    m = jnp.max(s, axis=axis, keepdims=True)
    e = jnp.exp(s - m)
    scale = 1.0 / jnp.sum(e, axis=axis, keepdims=True)
    return e * (scale if keep is None else scale * keep)


def _dot_split3(a, x):
    h1 = x.astype(BF16)
    r1 = x - h1.astype(F32)
    h2 = r1.astype(BF16)
    h3 = (r1 - h2.astype(F32)).astype(BF16)
    return _dot(a, h1) + _dot(a, h2) + _dot(a, h3)


def _top_blocks(score, n_top, axis):
    pos = lax.broadcasted_iota(jnp.int32, score.shape, axis).astype(F32)
    picks = []
    for _ in range(n_top):
        mx = jnp.max(score, axis=axis, keepdims=True)
        idx = jnp.min(jnp.where(score == mx, pos, float(score.shape[axis])), axis=axis, keepdims=True)
        picks.append(idx)
        score = jnp.where(pos == idx, NEG, score)
    return picks, score


def _nsa_prompt_kernel(tk, q_ref, gt_ref, fs_ref, cb_ref, kc_c, kc_s1, kc_s2, ovt_ref,
                       ks_ref, kw_ref, vst_ref, vwt_ref, oa_ref,
                       kc_s, vct_s, sel_s, m_s, l_s, acc_s):
    n_cmp = fs_ref.shape[2]
    n_sel = ovt_ref.shape[0]
    nq = Q_BLOCK
    cols = A_REP * nq
    per_tile = tk // L_SEL
    qi = pl.program_id(1)
    q0 = qi * nq

    @pl.when(qi == 0)
    def _():
        kc, vc = _compressed_kv(fs_ref, cb_ref, (kc_c[...], kc_s1[...], kc_s2[...]))
        kc_s[...] = kc.astype(BF16)
        vct_s[...] = vc.T.astype(BF16)

    t_all = q0 + (lax.broadcasted_iota(jnp.int32, (1, cols), 1) & (nq - 1))
    t_pos = t_all[:, 0:nq]
    gates = gt_ref[0]

    def heads(a):
        return jnp.concatenate([a] * A_REP, axis=1)

    cmp_end = lax.broadcasted_iota(jnp.int32, (n_cmp, 1), 0) * D_CMP + (L_CMP - 1)
    c_bias = heads(jnp.where(cmp_end <= t_pos, 0.0, NEG))
    c_keep = jnp.where(t_all >= L_CMP - 1, 1.0, 0.0)
    blk = lax.broadcasted_iota(jnp.int32, (n_sel, 1), 0)
    cur = t_pos // L_SEL
    forced = (blk == 0) | (blk == cur) | (blk == cur - 1)
    started = blk * L_SEL <= t_pos
    w_start = jnp.maximum(q0 - WINDOW, 0)
    w_len = WINDOW + nq
    rel = t_pos - (w_start + lax.broadcasted_iota(jnp.int32, (w_len, 1), 0))
    w_bias = heads(jnp.where((rel >= 0) & (rel < WINDOW), 0.0, NEG))
    n_tiles = (q0 + nq + tk - 1) // tk
    key_iota = lax.broadcasted_iota(jnp.int32, (tk, 1), 0)

    groups = range(A_KV)
    vs = [slice(g * A_DIM, (g + 1) * A_DIM) for g in groups]
    qg = [q_ref[g * A_REP:(g + 1) * A_REP].reshape(cols, LANES) for g in groups]

    o_c = []
    for g in groups:
        p_c = _bias_softmax(_dot_nt(kc_s[...], qg[g]) + c_bias, 0, c_keep)
        o_c.append(_dot(vct_s[vs[g], :], p_c.astype(BF16)))
        p_sum = p_c[:, 0:nq]
        for r in range(1, A_REP):
            p_sum = p_sum + p_c[:, r * nq:(r + 1) * nq]
        imp = _dot_split3(ovt_ref[...], p_sum)
        score = jnp.where(forced, FORCE_SCORE, jnp.where(started, imp, -1.0))
        _, left = _top_blocks(score, min(N_TOP, n_sel), 0)
        sel_s[g] = jnp.where(left < -2.0, 1.0, 0.0)

    m_s[...] = jnp.full(m_s.shape, NEG, F32)
    l_s[...] = jnp.zeros(l_s.shape, F32)
    acc_s[...] = jnp.zeros(acc_s.shape, F32)

    def tile(kt, carry):
        k0 = pl.multiple_of(kt * tk, tk)
        keys = ks_ref[0, pl.ds(k0, tk), :]
        causal = k0 + key_iota <= t_pos
        for g in groups:
            chosen = jnp.concatenate(
                [jnp.broadcast_to(sel_s[g, pl.ds(kt * per_tile + i, 1), :], (L_SEL, nq)) for i in range(per_tile)],
                axis=0)
            bias = jnp.where(jnp.where(causal, chosen, 0.0) > 0.5, 0.0, NEG)
            s = _dot_nt(keys, qg[g]) + jnp.concatenate([bias] * A_REP, axis=1)
            m_old = m_s[g]
            m_new = jnp.maximum(m_old, jnp.max(s, axis=0, keepdims=True))
            e = jnp.exp(s - m_new)
            a = jnp.exp(m_old - m_new)
            l_s[g] = a * l_s[g] + jnp.sum(e, axis=0, keepdims=True)
            acc_s[g] = a * acc_s[g] + _dot(vst_ref[0, kt, vs[g], :], e.astype(BF16))
            m_s[g] = m_new
        return carry

    lax.fori_loop(0, n_tiles, tile, 0)

    ws = pl.multiple_of(w_start, nq)
    wkeys = kw_ref[0, pl.ds(ws, w_len), :]
    wc = w_start // LANES
    heads_out = []
    for g in groups:
        l = l_s[g]
        o_s = acc_s[g] * (1.0 / jnp.where(l > 0.0, l, 1.0))
        p_w = _bias_softmax(_dot_nt(wkeys, qg[g]) + w_bias, 0).astype(BF16)
        o_w = None
        for c in range(w_len // LANES):
            term = _dot(vwt_ref[0, wc + c, vs[g], :], p_w[c * LANES:(c + 1) * LANES])
            o_w = term if o_w is None else o_w + term
        for r in range(A_REP):
            base = 2 * M_HEADS + (g * A_REP + r) * N_BRANCH
            cs = slice(r * nq, (r + 1) * nq)
            heads_out.append(gates[base:base + 1] * o_c[g][:, cs] + gates[base + 1:base + 2] * o_s[:, cs]
                             + gates[base + 2:base + 3] * o_w[:, cs])
    oa_ref[0] = jnp.concatenate(heads_out, axis=0).T.astype(BF16)


def _nsa_prompt(q8, gt, fs, cbias, kc_tabs, ovt, kk, vst, vwt, B, T, tk):
    nqb = T // Q_BLOCK
    n_cmp = fs.shape[2]
    n_sel = ovt.shape[0]
    cols = A_REP * Q_BLOCK
    const = lambda a: pl.BlockSpec(a.shape, lambda b, i: (0,) * a.ndim)
    keys = lambda j: pl.BlockSpec((1, T, LANES), lambda b, i: (b, 0, j))
    per_b = lambda a: pl.BlockSpec((1,) + a.shape[1:], lambda b, i: (b, 0, 0, 0))
    return pl.pallas_call(
        functools.partial(_nsa_prompt_kernel, tk),
        grid=(B, nqb),
        in_specs=[pl.BlockSpec((A_HEADS, Q_BLOCK, LANES), lambda b, i: (0, b * nqb + i, 0)),
                  pl.BlockSpec((1, gt.shape[1], Q_BLOCK), lambda b, i: (b, 0, i)),
                  pl.BlockSpec((2, 1, n_cmp, 2 * LANES), lambda b, i: (0, b, 0, 0)),
                  const(cbias), const(kc_tabs[0]), const(kc_tabs[1]), const(kc_tabs[2]),
                  const(ovt), keys(0), keys(1), per_b(vst), per_b(vwt)],
        out_specs=pl.BlockSpec((1, Q_BLOCK, A_WIDTH), lambda b, i: (b, i, 0)),
        out_shape=jax.ShapeDtypeStruct((B, T, A_WIDTH), BF16),
        scratch_shapes=[pltpu.VMEM((n_cmp, LANES), BF16), pltpu.VMEM((LANES, n_cmp), BF16),
                        pltpu.VMEM((A_KV, n_sel, Q_BLOCK), F32),
                        pltpu.VMEM((A_KV, 1, cols), F32), pltpu.VMEM((A_KV, 1, cols), F32),
                        pltpu.VMEM((A_KV, A_DIM, cols), F32)],
        compiler_params=_params("parallel", "arbitrary"),
        name="nsa_prompt",
    )(q8, gt, fs, cbias, *kc_tabs, ovt, kk, kk, vst, vwt)


def _nsa_sample_select_kernel(q_pos, n_sel, q_ref, fs_ref, cb_ref, kc_c, kc_s1, kc_s2, ov_ref, oc_ref, idx_ref):
    n_cmp = fs_ref.shape[2]
    kc, vc = _compressed_kv(fs_ref, cb_ref, (kc_c[...], kc_s1[...], kc_s2[...]))
    q = q_ref[0]
    cmp_end = lax.broadcasted_iota(jnp.int32, (A_HEADS, n_cmp), 1) * D_CMP + (L_CMP - 1)
    p_c = _masked_softmax(_dot_nt(q, kc.astype(BF16)), cmp_end <= q_pos, 1)
    oc_ref[0] = _dot(p_c.astype(BF16), vc.astype(BF16))
    head = lax.broadcasted_iota(jnp.int32, (A_HEADS, n_cmp), 0)
    p_sum = jnp.zeros((A_HEADS, n_cmp), F32)
    for g in range(A_KV):
        pg = jnp.sum(jnp.where(head // A_REP == g, p_c, 0.0), axis=0, keepdims=True)
        p_sum = jnp.where(head == g, pg, p_sum)
    imp = _dot_exact(p_sum, ov_ref[...])
    n_pad = ov_ref.shape[1]
    blk = lax.broadcasted_iota(jnp.int32, (A_HEADS, n_pad), 1)
    cur = q_pos // L_SEL
    forced = (blk == 0) | (blk == cur) | (blk == cur - 1)
    score = jnp.where(forced, FORCE_SCORE, jnp.where(blk * L_SEL <= q_pos, imp, -1.0))
    score = jnp.where(blk < n_sel, score, 2 * NEG)
    picks, _ = _top_blocks(score, min(N_TOP, n_sel), 1)
    lane = lax.broadcasted_iota(jnp.int32, (A_HEADS, LANES), 1)
    out = jnp.zeros((A_HEADS, LANES), jnp.int32)
    for j, idx in enumerate(picks):
        out = jnp.where(lane == j, idx.astype(jnp.int32), out)
    idx_ref[0] = out


def _nsa_sample_select(q8, fs, cbias, kc_tabs, ov, q_pos, n_sel):
    B = q8.shape[0]
    n_cmp = fs.shape[2]
    const = lambda a: pl.BlockSpec(a.shape, lambda b: (0,) * a.ndim)
    per_b = pl.BlockSpec((1, A_HEADS, LANES), lambda b: (b, 0, 0))
    return pl.pallas_call(
        functools.partial(_nsa_sample_select_kernel, q_pos, n_sel),
        grid=(B,),
        in_specs=[per_b, pl.BlockSpec((2, 1, n_cmp, 2 * LANES), lambda b: (0, b, 0, 0)),
                  const(cbias), const(kc_tabs[0]), const(kc_tabs[1]), const(kc_tabs[2]), const(ov)],
        out_specs=(per_b, per_b),
        out_shape=(jax.ShapeDtypeStruct((B, A_HEADS, LANES), F32),
                   jax.ShapeDtypeStruct((B, A_HEADS, LANES), jnp.int32)),
        compiler_params=_params("parallel"),
        name="nsa_sample_select",
    )(q8, fs, cbias, *kc_tabs, ov)


def _r16(a):
    return a.astype(BF16).astype(F32)


def _nsa_sample_attend_kernel(q_pos, pt_ref, ix_ref, q_ref, new_ref, newc_ref, oc_ref, misc_ref, win_ref,
                              k0_ref, k1_ref, v0_ref, v1_ref, oa_ref, nwin_ref, m_s, l_s, acc_s):
    b, j = pl.program_id(0), pl.program_id(1)
    q = q_ref[0]
    qf = q.astype(F32)
    head = lax.broadcasted_iota(jnp.int32, (A_HEADS, 1), 0)
    new = new_ref[0]
    page = k0_ref.shape[1]
    per_page = page // L_SEL

    @pl.when(j == 0)
    def _():
        m_s[...] = jnp.full(m_s.shape, NEG, F32)
        l_s[...] = jnp.zeros(l_s.shape, F32)
        acc_s[...] = jnp.zeros(acc_s.shape, F32)

    lane = lax.broadcasted_iota(jnp.int32, (A_HEADS, page), 1)
    for g, (k_ref, v_ref) in enumerate(((k0_ref, v0_ref), (k1_ref, v1_ref))):
        blk = ix_ref[b, g, j]
        fresh = blk * L_SEL >= q_pos
        limit = jnp.where(fresh, -1, q_pos)
        mine = head // A_REP == g
        s = _dot(q, k_ref[...].astype(BF16))
        s_pos = (blk // per_page) * page + lane
        valid = jnp.where(mine, jnp.where(lane // L_SEL == blk % per_page, s_pos, limit + 1), limit + 1) <= limit
        s_new = jnp.sum(qf * _r16(new[0:1, :]), axis=-1, keepdims=True)
        valid_new = jnp.where(mine, jnp.where(fresh, 1, 0), 0) > 0
        m_old = m_s[...]
        m_new = jnp.maximum(jnp.maximum(m_old, jnp.max(jnp.where(valid, s, NEG), axis=-1, keepdims=True)),
                            jnp.where(valid_new, s_new, NEG))
        e = jnp.where(valid, jnp.exp(s - m_new), 0.0)
        e_new = jnp.where(valid_new, jnp.exp(s_new - m_new), 0.0)
        a = jnp.exp(m_old - m_new)
        l_s[...] = a * l_s[...] + jnp.sum(e, axis=-1, keepdims=True) + e_new
        acc_s[...] = (a * acc_s[...] + _dot_nt(e.astype(BF16), v_ref[...].astype(BF16))
                      + _r16(e_new) * _r16(new[1:2, :]))
        m_s[...] = m_new

    @pl.when(j == pl.num_programs(1) - 1)
    def _():
        l = l_s[...]
        o_s = acc_s[...] / jnp.where(l > 0.0, l, 1.0)
        wb = win_ref.shape[2]
        kwt, vwt = win_ref[0], win_ref[1]
        s_old = _dot(q, kwt.astype(BF16))
        s_new = jnp.sum(qf * _r16(new[2:3, :]), axis=-1, keepdims=True)
        rel = wb - lax.broadcasted_iota(jnp.int32, (A_HEADS, wb), 1)
        ok = (rel >= 0) & (rel < WINDOW) & (q_pos - rel >= 0)
        m = jnp.maximum(jnp.max(jnp.where(ok, s_old, NEG), axis=-1, keepdims=True), s_new)
        e_old = jnp.where(ok, jnp.exp(s_old - m), 0.0)
        e_new = jnp.exp(s_new - m)
        den = jnp.sum(e_old, axis=-1, keepdims=True) + e_new
        o_w = _dot_nt((e_old / den).astype(BF16), vwt.astype(BF16)) + _r16(e_new / den) * _r16(new[3:4, :])
        gates = misc_ref[0]
        oa_ref[0] = gates[:, 0:1] * oc_ref[0] + gates[:, 1:2] * o_s + gates[:, 2:3] * o_w
        newc = newc_ref[0]
        last = lax.broadcasted_iota(jnp.int32, (LANES, wb), 1) == wb - 1
        nwin_ref[0] = jnp.where(last, newc[:, 2:3], pltpu.roll(kwt, wb - 1, 1))
        nwin_ref[1] = jnp.where(last, newc[:, 3:4], pltpu.roll(vwt, wb - 1, 1))


def _nsa_sample_attend(page_table, idx, q8, new_rows, new_cols, o_c, gate_rows, win_t, cache_t, layer, q_pos):
    B, n_pages = page_table.shape
    n_top = idx.shape[2]
    page = cache_t.shape[4]
    per_page = page // L_SEL
    wb = win_t.shape[4]

    def blk_spec(g, slot):
        def index(b, j, pt, ix):
            pg = jnp.minimum(ix[b, g, j] // per_page, n_pages - 1)
            return (pt[b, pg], layer, slot, 0, 0)
        return pl.BlockSpec((None, None, None, KV_WIDTH, page), index)

    per_b = pl.BlockSpec((1, A_HEADS, LANES), lambda b, j, pt, ix: (b, 0, 0))
    grid_spec = pltpu.PrefetchScalarGridSpec(
        num_scalar_prefetch=2,
        grid=(B, n_top),
        in_specs=[per_b, per_b, pl.BlockSpec((1, KV_WIDTH, A_HEADS), lambda b, j, pt, ix: (b, 0, 0)), per_b, per_b,
                  pl.BlockSpec((None, None, 2, KV_WIDTH, wb), lambda b, j, pt, ix: (b, layer, 0, 0, 0)),
                  blk_spec(0, 2), blk_spec(1, 2), blk_spec(0, 3), blk_spec(1, 3)],
        out_specs=(per_b, pl.BlockSpec((None, 2, KV_WIDTH, wb), lambda b, j, pt, ix: (b, 0, 0, 0))),
        scratch_shapes=[pltpu.VMEM((A_HEADS, 1), F32), pltpu.VMEM((A_HEADS, 1), F32),
                        pltpu.VMEM((A_HEADS, LANES), F32)],
    )
    return pl.pallas_call(
        functools.partial(_nsa_sample_attend_kernel, q_pos),
        grid_spec=grid_spec,
        out_shape=(jax.ShapeDtypeStruct((B, A_HEADS, LANES), F32),
                   jax.ShapeDtypeStruct((B, 2, KV_WIDTH, wb), F32)),
        compiler_params=_params("parallel", "arbitrary"),
        name="nsa_sample_attend",
    )(page_table, idx, q8, new_rows, new_cols, o_c, gate_rows, win_t, cache_t, cache_t, cache_t, cache_t)


def _ffn_kernel(final, x_ref, hm_ref, oa_ref, wo_ref, nf_ref, wg_ref, wu_ref, wd_ref, nfin_ref, out_ref,
                x1_s, xn_s, acc_s):
    j = pl.program_id(1)

    @pl.when(j == 0)
    def _():
        half = hm_ref.shape[1]
        x1 = x_ref[...] + _dot(hm_ref[...], wo_ref[0:half, :]) + _dot(oa_ref[...], wo_ref[half:, :])
        x1_s[...] = x1
        ms = jnp.mean(x1 * x1, axis=-1, keepdims=True)
        xn_s[...] = ((x1 * lax.rsqrt(ms + EPS)) * nf_ref[...]).astype(BF16)
        acc_s[...] = jnp.zeros(acc_s.shape, F32)

    xn = xn_s[...]
    gate = _dot(xn, wg_ref[...])
    act = (gate * _sigmoid(gate)) * _dot(xn, wu_ref[...])
    acc_s[...] += _dot(act.astype(BF16), wd_ref[...])

    @pl.when(j == pl.num_programs(1) - 1)
    def _():
        y = x1_s[...] + acc_s[...]
        if final:
            ms = jnp.mean(y * y, axis=-1, keepdims=True)
            y = (y * lax.rsqrt(ms + EPS)) * nfin_ref[...]
        out_ref[...] = y


def _ffn(x, hm, oa, wo, nf, wg, wu, wd, nfin, final, tm, tf):
    M, D = x.shape
    F = wg.shape[1]
    row = lambda w: pl.BlockSpec((tm, w), lambda i, j: (i, 0))
    const = lambda a: pl.BlockSpec(a.shape, lambda i, j: (0,) * a.ndim)
    return pl.pallas_call(
        functools.partial(_ffn_kernel, final),
        grid=(M // tm, F // tf),
        in_specs=[row(D), row(hm.shape[1]), row(oa.shape[1]), const(wo), const(nf),
                  pl.BlockSpec((D, tf), lambda i, j: (0, j)), pl.BlockSpec((D, tf), lambda i, j: (0, j)),
                  pl.BlockSpec((tf, D), lambda i, j: (j, 0)), const(nfin)],
        out_specs=row(D),
        out_shape=jax.ShapeDtypeStruct((M, D), F32),
        scratch_shapes=[pltpu.VMEM((tm, D), F32), pltpu.VMEM((tm, D), BF16), pltpu.VMEM((tm, D), F32)],
        compiler_params=_params("parallel", "arbitrary"),
        name="ffn",
    )(x, hm, oa, wo, nf, wg, wu, wd, nfin)


def _permute_w_in(w):
    a = 4 * M_WIDTH
    b = a + 2 * M_HEADS
    c = b + A_WIDTH + 6 * KV_WIDTH
    pad = jnp.zeros((w.shape[0], LANES - N_MISC), w.dtype)
    return jnp.concatenate([w[:, :a], w[:, b:c], w[:, a:b], w[:, c:], pad], axis=1).astype(BF16)


def _cmp_weights(wk, wv):
    def one(w):
        z = jnp.zeros((D_CMP, A_DIM, A_DIM), w.dtype)
        halves = []
        for part in (w[:D_CMP], w[D_CMP:]):
            top = jnp.concatenate([part, z], axis=2)
            bot = jnp.concatenate([z, part], axis=2)
            halves.append(jnp.concatenate([top, bot], axis=1))
        return jnp.concatenate(halves, axis=2)
    return jnp.stack([one(wk), one(wv)]).astype(BF16)


def _overlap(n_cmp_pad, n_sel_pad, n_cmp, n_sel):
    lo_c = np.arange(n_cmp_pad)[:, None] * D_CMP
    lo_s = np.arange(n_sel_pad)[None, :] * L_SEL
    ov = np.clip(np.minimum(lo_c + L_CMP, lo_s + L_SEL) - np.maximum(lo_c, lo_s), 0, None).astype(np.float32) / L_CMP
    ov[n_cmp:, :] = 0.0
    ov[:, n_sel:] = 0.0
    return jnp.asarray(ov)


def _pick(n, prefs):
    for p in prefs:
        if n % p == 0:
            return p
    return n


def kernel(x_prompt, x_sample, cache_nsa_kv, page_table, state_win_kv, state_mlstm_c, state_mlstm_n, state_mlstm_m, norm_mix, w_in, b_if, m_norm, w_cmp_k, b_cmp_k, w_cmp_v, b_cmp_v, w_out, norm_ffn, w_gate, w_up, w_down, norm_final):
    B, T, D = x_prompt.shape
    Bs, Ts, _ = x_sample.shape
    depth = w_in.shape[0]
    n_pool, _, n_slots, page = cache_nsa_kv.shape[:4]
    n_pages = page_table.shape[1]
    past = n_pages * page
    wb = state_win_kv.shape[3]
    assert Ts == 1 and T % Q_BLOCK == 0 and T >= WINDOW + Q_BLOCK and wb == WINDOW
    assert past % page == 0 and page % L_SEL == 0 and page % D_CMP == 0
    assert ((past + Ts) // D_CMP) * D_CMP <= past

    Mp = B * T
    tm_p = _pick(Mp, (512, 256, 128))
    tm_f = _pick(Mp, (1024, 512, 256, 128))
    ff = w_gate.shape[2]
    tf = _pick(ff, (256, 128))
    l_chunk = _pick(T, (256, 128, 64))
    ls_pad = 16
    tk = tm_p
    assert T % tk == 0 and tk % LANES == 0
    n_group = _pick(n_pages, (16, 8, 4, 2, 1))
    tr = _pick(T, (2048, 1024, 512, 256))

    n_chunk_p = T // D_CMP
    n_cmp_p, n_sel_p = n_chunk_p - 1, -(-T // L_SEL)
    tabs_p = _rope_tables(jnp.arange(T))
    kc_tabs_p = _rope_tables(jnp.arange(n_chunk_p) * D_CMP)
    assert L_CMP <= 256
    ovt_p = _overlap(n_chunk_p, n_sel_p, n_cmp_p, n_sel_p).T.astype(BF16)
    n_chunk_s = (past + Ts) // D_CMP
    n_cmp_s, n_sel_s = n_chunk_s - 1, -(-(past + Ts) // L_SEL)
    n_sel_pad = -(-n_sel_s // LANES) * LANES
    tabs_s = _rope_tables(jnp.full((Bs,), past))
    kc_tabs_s = _rope_tables(jnp.arange(n_chunk_s) * D_CMP)
    ov_s = _overlap(n_chunk_s, n_sel_pad, n_cmp_s, n_sel_s)
    cache_t = cache_nsa_kv.transpose(0, 1, 2, 4, 5, 3).reshape(n_pool, depth, n_slots, KV_WIDTH, page)
    win_t = state_win_kv.transpose(0, 1, 2, 4, 5, 3).reshape(Bs, depth, 2, KV_WIDTH, wb)

    xp = x_prompt.reshape(Mp, D)
    xs = x_sample.reshape(Bs, D)
    rows_p, rows_s, win_p, win_s, c_p, c_s, n_p, n_s, m_p, m_s = ([] for _ in range(10))
    y_p = y_s = None
    nfin = norm_final.reshape(1, D)
    for l in range(depth):
        w_perm = _permute_w_in(w_in[l])
        gain = norm_mix[l].reshape(1, D)
        bias = jnp.zeros((1, LANES), F32).at[0, :2 * M_HEADS].set(b_if[l].reshape(-1))
        mg = m_norm[l].reshape(1, M_WIDTH)
        w_cmp = _cmp_weights(w_cmp_k[l], w_cmp_v[l])
        cbias = jnp.stack([jnp.tile(b_cmp_k[l], A_KV), jnp.tile(b_cmp_v[l], A_KV)])
        wo, nf = w_out[l].astype(BF16), norm_ffn[l].reshape(1, D)
        wg, wu, wd = w_gate[l].astype(BF16), w_up[l].astype(BF16), w_down[l].astype(BF16)
        final = l == depth - 1

        qkv, og, q8, rows, win, misc, kk, vst, vwt = _inproj(xp, gain, w_perm, bias, tabs_p, tm_p, True)
        misc3 = misc.reshape(B, T, LANES)
        gt = misc3[:, :, :N_MISC].transpose(0, 2, 1)
        hm, c_new, n_new, m_new = _mlstm(
            qkv.reshape(B, T, -1), og.reshape(B, T, -1), misc3, gt, mg,
            jnp.zeros((B, M_HEADS, M_DIM, M_DIM), F32), jnp.zeros((B, M_HEADS, M_DIM), F32),
            jnp.full((B, M_HEADS, M_DIM), NEG, F32), l_chunk)
        fs = _cmp_prompt(rows, w_cmp, tr).reshape(2, B, n_chunk_p, 2 * LANES)
        oa = _nsa_prompt(q8, gt, fs, cbias, kc_tabs_p, ovt_p, kk.reshape(B, T, -1),
                         vst.reshape(B, T // tk, KV_WIDTH, tk), vwt.reshape(B, T // LANES, KV_WIDTH, LANES), B, T, tk)
        xp_new = _ffn(xp, hm.reshape(Mp, -1), oa.reshape(Mp, -1), wo, nf, wg, wu, wd, nfin, final, tm_f, tf)
        if final:
            y_p = xp_new
        xp = xp_new
        rows_p.append(rows.reshape(B, T, n_slots, A_KV, A_DIM))
        win_p.append(win.reshape(B, T, 2, A_KV, A_DIM)[:, T - wb:].transpose(0, 2, 1, 3, 4))
        c_p.append(c_new); n_p.append(n_new); m_p.append(m_new[:, :, 0])

        qkv, og, q8, rows, win, misc = _inproj(xs, gain, w_perm, bias, tabs_s, Bs, False)
        pad_t = lambda a: jnp.pad(a[:, None, :], ((0, 0), (0, ls_pad - 1), (0, 0)))
        inert = jnp.zeros((ls_pad, LANES), F32).at[1:, :M_HEADS].set(NEG)
        misc_pad = pad_t(misc) + inert[None]
        gt = misc_pad[:, :, :2 * M_HEADS].transpose(0, 2, 1)
        m0 = jnp.broadcast_to(state_mlstm_m[:, l, :, None].astype(F32), (Bs, M_HEADS, M_DIM))
        hm, c_new, n_new, m_new = _mlstm(
            pad_t(qkv), pad_t(og), misc_pad, gt, mg,
            state_mlstm_c[:, l].astype(F32), state_mlstm_n[:, l].astype(F32), m0, ls_pad)
        hm = hm[:, 0]
        fs = _cmp_sample(cache_t, page_table, l, w_cmp, n_group)
        q8s = q8.transpose(1, 0, 2)
        o_c, idx = _nsa_sample_select(q8s, fs, cbias, kc_tabs_s, ov_s, past, n_sel_s)
        idx = idx[:, :A_KV, :min(N_TOP, n_sel_s)]
        new_rows = jnp.pad(jnp.stack([rows[:, 2 * LANES:3 * LANES], rows[:, 3 * LANES:], win[:, :LANES],
                                      win[:, LANES:]], axis=1), ((0, 0), (0, A_HEADS - 4), (0, 0)))
        gate_rows = jnp.pad(misc[:, 2 * M_HEADS:N_MISC].reshape(Bs, A_HEADS, N_BRANCH),
                            ((0, 0), (0, 0), (0, LANES - N_BRANCH)))
        oa8, nwin = _nsa_sample_attend(page_table, idx, q8s, new_rows, new_rows.transpose(0, 2, 1), o_c, gate_rows,
                                       win_t, cache_t, l, past)
        grp = (jnp.arange(A_HEADS) // A_REP)[None, :, None, None]
        oa = jnp.take_along_axis(oa8.reshape(Bs, A_HEADS, A_KV, A_DIM), jnp.broadcast_to(grp, (Bs, A_HEADS, 1, A_DIM)),
                                 axis=2).reshape(Bs, A_WIDTH).astype(BF16)
        xs_new = _ffn(xs, hm, oa, wo, nf, wg, wu, wd, nfin, final, Bs, tf)
        if final:
            y_s = xs_new
        xs = xs_new
        rows_s.append(rows.reshape(Bs, Ts, n_slots, A_KV, A_DIM))
        win_s.append(nwin.reshape(Bs, 2, A_KV, A_DIM, wb).transpose(0, 1, 4, 2, 3))
        c_s.append(c_new); n_s.append(n_new); m_s.append(m_new[:, :, 0])

    return (y_p.reshape(B, T, D), y_s.reshape(Bs, Ts, D),
            jnp.stack(rows_p, axis=2), jnp.stack(rows_s, axis=2),
            jnp.stack(win_p, axis=1), jnp.stack(win_s, axis=1),
            jnp.stack(c_p, axis=1), jnp.stack(c_s, axis=1),
            jnp.stack(n_p, axis=1), jnp.stack(n_s, axis=1),
            jnp.stack(m_p, axis=1), jnp.stack(m_s, axis=1))
```

```python
import functools

import jax
import jax.numpy as jnp
import numpy as np
from jax import lax
from jax.experimental import pallas as pl
from jax.experimental.pallas import tpu as pltpu

F32 = jnp.float32
BF16 = jnp.bfloat16

M_HEADS = 4
M_DIM = 128
M_WIDTH = M_HEADS * M_DIM
A_HEADS = 8
A_DIM = 64
A_KV = 2
A_REP = A_HEADS // A_KV
A_WIDTH = A_HEADS * A_DIM
KV_WIDTH = A_KV * A_DIM
L_CMP = 32
D_CMP = 16
L_SEL = 64
N_TOP = 16
WINDOW = 512
Q_BLOCK = 128
N_BRANCH = 3
ROPE_THETA = 500000.0
ROPE_DIM = A_DIM // 4
EPS = 1e-6
NEG = -1e30
LOG2_E = 1.4426950408889634
FORCE_SCORE = 1e4
LANES = 128
N_MISC = 2 * M_HEADS + N_BRANCH * A_HEADS
D_PERM = 4 * M_WIDTH + A_WIDTH + 6 * KV_WIDTH + LANES
VMEM_LIMIT = 56 * 1024 * 1024

_NT = (((1,), (1,)), ((), ()))
_TN = (((0,), (0,)), ((), ()))


def _params(*sem):
    return pltpu.CompilerParams(dimension_semantics=sem, vmem_limit_bytes=VMEM_LIMIT)


def _sigmoid(x):
    return 1.0 / (1.0 + jnp.exp(-x))


def _dot(a, b):
    return jnp.dot(a, b, preferred_element_type=F32)


def _dot_exact(a, b):
    return jnp.dot(a, b, preferred_element_type=F32, precision=lax.Precision.HIGHEST)


def _dot_nt(a, b):
    return lax.dot_general(a, b, _NT, preferred_element_type=F32)


def _rope128(v, c, s1, s2):
    half = ROPE_DIM // 2
    return v * c + pltpu.roll(v, LANES - half, 1) * s1 + pltpu.roll(v, half, 1) * s2


def _rope_tables(pos):
    half = ROPE_DIM // 2
    inv = ROPE_THETA ** (-jnp.arange(half, dtype=F32) / half)
    ang = pos.astype(F32)[:, None] * inv[None, :]
    cos, sin = jnp.cos(ang), jnp.sin(ang)
    n = pos.shape[0]
    one = jnp.ones((n, A_DIM - ROPE_DIM), F32)
    zero = jnp.zeros((n, A_DIM - ROPE_DIM), F32)
    zh = jnp.zeros((n, half), F32)
    c = jnp.concatenate([cos, cos, one], axis=1)
    s1 = jnp.concatenate([-sin, zh, zero], axis=1)
    s2 = jnp.concatenate([zh, sin, zero], axis=1)
    tile = lambda a: jnp.concatenate([a, a], axis=1)
    return tile(c), tile(s1), tile(s2)


def _inproj_kernel(q_scale, x_ref, g_ref, w_ref, bias_ref, rc_ref, rs1_ref, rs2_ref,
                   om_ref, og_ref, oq_ref, orows_ref, owin_ref, omisc_ref, *attn_refs):
    x = x_ref[...]
    ms = jnp.mean(x * x, axis=-1, keepdims=True)
    xn = ((x * lax.rsqrt(ms + EPS)) * g_ref[...]).astype(BF16)
    c, s1, s2 = rc_ref[...], rs1_ref[...], rs2_ref[...]
    lane = lax.broadcasted_iota(jnp.int32, (x.shape[0], LANES), 1)

    def proj(a, b):
        return _dot(xn, w_ref[:, a:b])

    om_ref[:, 0:M_WIDTH] = (proj(0, M_WIDTH) * (M_DIM ** -0.5)).astype(BF16)
    om_ref[:, M_WIDTH:3 * M_WIDTH] = proj(M_WIDTH, 3 * M_WIDTH).astype(BF16)
    og_ref[...] = _sigmoid(proj(3 * M_WIDTH, 4 * M_WIDTH))

    off = 4 * M_WIDTH
    pq = proj(off, off + A_WIDTH)
    for j in range(A_WIDTH // LANES):
        blk = _rope128(pq[:, j * LANES:(j + 1) * LANES], c, s1, s2) * q_scale
        swapped = pltpu.roll(blk, A_DIM, 1)
        for e in range(2):
            hd = 2 * j + e
            grp = hd // A_REP
            src = blk if e == grp else swapped
            keep = (lane // A_DIM) == grp
            oq_ref[hd] = jnp.where(keep, src, 0.0).astype(BF16)

    off += A_WIDTH
    pk = proj(off, off + 6 * KV_WIDTH)
    kcr, vcr, ksl, vsl, kw, vw = [pk[:, i * LANES:(i + 1) * LANES] for i in range(6)]
    ksl = _rope128(ksl, c, s1, s2)
    kw = _rope128(kw, c, s1, s2)
    for i, a in enumerate((kcr, vcr, ksl, vsl)):
        orows_ref[:, i * LANES:(i + 1) * LANES] = a
    owin_ref[:, 0:LANES] = kw
    owin_ref[:, LANES:2 * LANES] = vw
    if attn_refs:
        okk_ref, ovst_ref, ovwt_ref = attn_refs
        okk_ref[:, 0:LANES] = ksl.astype(BF16)
        okk_ref[:, LANES:2 * LANES] = kw.astype(BF16)
        ovst_ref[0] = vsl.T.astype(BF16)
        for i in range(ovwt_ref.shape[0]):
            ovwt_ref[i] = vw[i * LANES:(i + 1) * LANES].T.astype(BF16)

    off += 6 * KV_WIDTH
    pm = proj(off, off + LANES) + bias_ref[...]
    log_sig = -(jnp.maximum(-pm, 0.0) + jnp.log(1.0 + jnp.exp(-jnp.abs(pm))))
    omisc_ref[...] = jnp.where(lane < M_HEADS, pm, jnp.where(lane < 2 * M_HEADS, log_sig, _sigmoid(pm)))


def _inproj(x, gain, w_perm, bias, tabs, tm, attn):
    M, D = x.shape
    nt = tabs[0].shape[0] // tm
    row = lambda i: (i, 0)
    tab = pl.BlockSpec((tm, LANES), lambda i: (i % nt, 0))
    full = lambda a: pl.BlockSpec(a.shape, lambda i: (0,) * a.ndim)
    out_shape = (
        jax.ShapeDtypeStruct((M, 3 * M_WIDTH), BF16),
        jax.ShapeDtypeStruct((M, M_WIDTH), F32),
        jax.ShapeDtypeStruct((A_HEADS, M, LANES), BF16),
        jax.ShapeDtypeStruct((M, 4 * KV_WIDTH), F32),
        jax.ShapeDtypeStruct((M, 2 * KV_WIDTH), F32),
        jax.ShapeDtypeStruct((M, LANES), F32),
    )
    out_specs = (
        pl.BlockSpec((tm, 3 * M_WIDTH), row),
        pl.BlockSpec((tm, M_WIDTH), row),
        pl.BlockSpec((A_HEADS, tm, LANES), lambda i: (0, i, 0)),
        pl.BlockSpec((tm, 4 * KV_WIDTH), row),
        pl.BlockSpec((tm, 2 * KV_WIDTH), row),
        pl.BlockSpec((tm, LANES), row),
    )
    if attn:
        tk = attn
        per = tk // tm
        out_shape += (jax.ShapeDtypeStruct((M, 2 * KV_WIDTH), BF16),
                      jax.ShapeDtypeStruct((M // tk, KV_WIDTH, tk), BF16),
                      jax.ShapeDtypeStruct((M // LANES, KV_WIDTH, LANES), BF16))
        out_specs += (pl.BlockSpec((tm, 2 * KV_WIDTH), row),
                      pl.BlockSpec((1, KV_WIDTH, tm), lambda i: (i // per, 0, i % per)),
                      pl.BlockSpec((tm // LANES, KV_WIDTH, LANES), lambda i: (i, 0, 0)))
    return pl.pallas_call(
        functools.partial(_inproj_kernel, A_DIM ** -0.5 * (LOG2_E if attn else 1.0)),
        grid=(M // tm,),
        in_specs=[pl.BlockSpec((tm, D), row), full(gain), full(w_perm), full(bias), tab, tab, tab],
        out_specs=out_specs,
        out_shape=out_shape,
        compiler_params=_params("parallel"),
        name="inproj",
    )(x, gain, w_perm, bias, *tabs)


def _mlstm_kernel(q_ref, k_ref, v_ref, og_ref, misc_ref, gt_ref, mg_ref, c0_ref, n0_ref, m0_ref,
                  hm_ref, c_ref, n_ref, m_ref):
    L = q_ref.shape[1]

    @pl.when(pl.program_id(1) == 0)
    def _():
        c_ref[...] = c0_ref[...]
        n_ref[...] = n0_ref[...]
        m_ref[...] = m0_ref[...]

    gc = misc_ref[0]
    gr = gt_ref[0]
    row = lax.broadcasted_iota(jnp.int32, (L, L), 0)
    col = lax.broadcasted_iota(jnp.int32, (L, L), 1)
    causal = row >= col
    b_col = _dot_exact(causal.astype(F32), gc)
    b_row = _dot_exact(gr, (row <= col).astype(F32))
    for h in range(M_HEADS):
        sl = slice(h * M_DIM, (h + 1) * M_DIM)
        q, k, v = q_ref[0, :, sl], k_ref[0, :, sl], v_ref[0, :, sl]
        c_prev, n_prev, m_prev = c_ref[0, h], n_ref[0, h:h + 1, :], m_ref[0, h:h + 1, 0:1]
        bc = b_col[:, M_HEADS + h:M_HEADS + h + 1]
        ic = gc[:, h:h + 1]
        br = b_row[M_HEADS + h:M_HEADS + h + 1, :]
        ir = gr[h:h + 1, :]
        d = jnp.where(causal, bc - br + ir, NEG)
        m_inter = bc + m_prev
        m_t = jnp.maximum(m_inter, jnp.max(d, axis=-1, keepdims=True))
        w_inter = jnp.exp(m_inter - m_t)
        p = _dot_nt(q, k) * jnp.exp(d - m_t)
        num = w_inter * _dot(q, c_prev.astype(BF16)) + _dot(p.astype(BF16), v)
        den = (w_inter * jnp.sum(q.astype(F32) * n_prev, axis=-1, keepdims=True)
               + jnp.sum(p, axis=-1, keepdims=True))
        hh = num / jnp.maximum(jnp.abs(den), jnp.exp(-m_t))
        m_new = m_t[L - 1:L, :]
        w_new = jnp.exp(bc[L - 1:L, :] - bc + ic - m_new)
        decay = jnp.exp(m_inter[L - 1:L, :] - m_new)
        kf, vf = k.astype(F32), v.astype(F32)
        c_ref[0, h] = decay * c_prev + lax.dot_general(k, (w_new * vf).astype(BF16), _TN,
                                                       preferred_element_type=F32)
        n_ref[0, h:h + 1, :] = decay * n_prev + jnp.sum(w_new * kf, axis=0, keepdims=True)
        m_ref[0, h:h + 1, :] = jnp.broadcast_to(m_new, (1, M_DIM))
        hn = hh * lax.rsqrt(jnp.mean(hh * hh, axis=-1, keepdims=True) + EPS)
        hm_ref[0, :, sl] = ((hn * mg_ref[:, sl]) * og_ref[0, :, sl]).astype(BF16)


def _mlstm(qkv, og, misc, gt, mg, c0, n0, m0, L):
    B, T, _ = qkv.shape
    seq = lambda j: pl.BlockSpec((1, L, M_WIDTH), lambda b, c: (b, c, j))
    st4 = pl.BlockSpec((1, M_HEADS, M_DIM, M_DIM), lambda b, c: (b, 0, 0, 0))
    st3 = pl.BlockSpec((1, M_HEADS, M_DIM), lambda b, c: (b, 0, 0))
    return pl.pallas_call(
        _mlstm_kernel,
        grid=(B, T // L),
        in_specs=[seq(0), seq(1), seq(2), seq(0),
                  pl.BlockSpec((1, L, LANES), lambda b, c: (b, c, 0)),
                  pl.BlockSpec((1, 2 * M_HEADS, L), lambda b, c: (b, 0, c)),
                  pl.BlockSpec((1, M_WIDTH), lambda b, c: (0, 0)),
                  st4, st3, st3],
        out_specs=(seq(0), st4, st3, st3),
        out_shape=(jax.ShapeDtypeStruct((B, T, M_WIDTH), BF16),
                   jax.ShapeDtypeStruct(c0.shape, F32),
                   jax.ShapeDtypeStruct(n0.shape, F32),
                   jax.ShapeDtypeStruct(m0.shape, F32)),
        compiler_params=_params("parallel", "arbitrary"),
        name="mlstm",
    )(qkv, qkv, qkv, og, misc, gt, mg, c0, n0, m0)


def _cmp_accumulate(load, w_ref, s):
    acc = None
    for r in range(0, D_CMP, 2):
        lhs = jnp.concatenate([load(r), load(r + 1)], axis=1).astype(BF16)
        term = _dot(lhs, w_ref[s, r // 2])
        acc = term if acc is None else acc + term
    return acc


def _cmp_prompt_kernel(k_ref, v_ref, w_ref, fs_ref):
    n = k_ref.shape[0] // D_CMP
    for s, ref in enumerate((k_ref, v_ref)):
        fs_ref[s] = _cmp_accumulate(lambda r: ref[pl.ds(r, n, stride=D_CMP), :], w_ref, s)


def _cmp_prompt(rows, w_cmp, tr):
    M = rows.shape[0]
    return pl.pallas_call(
        _cmp_prompt_kernel,
        grid=(M // tr,),
        in_specs=[pl.BlockSpec((tr, KV_WIDTH), lambda i: (i, 0)), pl.BlockSpec((tr, KV_WIDTH), lambda i: (i, 1)),
                  pl.BlockSpec(w_cmp.shape, lambda i: (0, 0, 0, 0))],
        out_specs=pl.BlockSpec((2, tr // D_CMP, 2 * LANES), lambda i: (0, i, 0)),
        out_shape=jax.ShapeDtypeStruct((2, M // D_CMP, 2 * LANES), F32),
        compiler_params=_params("parallel"),
        name="cmp_prompt",
    )(rows, rows, w_cmp)


def _cmp_sample_kernel(n_group, pt_ref, *refs):
    page_refs, w_ref, fs_ref, rows_s = refs[:n_group], refs[n_group], refs[n_group + 1], refs[n_group + 2:]
    page = page_refs[0].shape[2]
    n = n_group * page // D_CMP
    for s in range(2):
        for i, p in enumerate(page_refs):
            rows_s[s][i * page:(i + 1) * page, :] = p[s].T
    for s in range(2):
        fs_ref[s, 0] = _cmp_accumulate(lambda r: rows_s[s][pl.ds(r, n, stride=D_CMP), :], w_ref, s)


def _cmp_sample(cache_t, page_table, layer, w_cmp, n_group):
    B, n_pages = page_table.shape
    page = cache_t.shape[4]
    per_page = page // D_CMP

    def page_spec(i):
        return pl.BlockSpec((None, None, 2, KV_WIDTH, page),
                            lambda b, p, pt: (pt[b, p * n_group + i], layer, 0, 0, 0))

    grid_spec = pltpu.PrefetchScalarGridSpec(
        num_scalar_prefetch=1,
        grid=(B, n_pages // n_group),
        in_specs=[page_spec(i) for i in range(n_group)]
        + [pl.BlockSpec(w_cmp.shape, lambda b, p, pt: (0, 0, 0, 0))],
        out_specs=pl.BlockSpec((2, 1, n_group * per_page, 2 * LANES), lambda b, p, pt: (0, b, p, 0)),
        scratch_shapes=[pltpu.VMEM((n_group * page, KV_WIDTH), F32), pltpu.VMEM((n_group * page, KV_WIDTH), F32)],
    )
    return pl.pallas_call(
        functools.partial(_cmp_sample_kernel, n_group),
        grid_spec=grid_spec,
        out_shape=jax.ShapeDtypeStruct((2, B, n_pages * per_page, 2 * LANES), F32),
        compiler_params=_params("parallel", "arbitrary"),
        name="cmp_sample",
    )(page_table, *([cache_t] * n_group), w_cmp)


def _compressed_kv(fs_ref, bias_ref, tabs):
    n = fs_ref.shape[2]
    out = []
    for s in range(2):
        fs = fs_ref[s, 0]
        out.append(fs[:, 0:LANES] + pltpu.roll(fs[:, LANES:2 * LANES], n - 1, 0) + bias_ref[s:s + 1, :])
    return _rope128(out[0], *tabs), out[1]


def _masked_softmax(s, valid, axis):
    m = jnp.max(jnp.where(valid, s, NEG), axis=axis, keepdims=True)
    e = jnp.where(valid, jnp.exp(s - m), 0.0)
    l = jnp.sum(e, axis=axis, keepdims=True)
    return e * (1.0 / jnp.where(l > 0.0, l, 1.0))


def _bias_softmax(s, axis, keep=None):
    m = jnp.max(s, axis=axis, keepdims=True)
    e = jnp.exp2(s - m)
    scale = 1.0 / jnp.sum(e, axis=axis, keepdims=True)
    return e * (scale if keep is None else scale * keep)


def _dot_split3(a, x):
    h1 = x.astype(BF16)
    r1 = x - h1.astype(F32)
    h2 = r1.astype(BF16)
    h3 = (r1 - h2.astype(F32)).astype(BF16)
    return _dot(a, h1) + _dot(a, h2) + _dot(a, h3)


def _top_blocks(score, n_top, axis):
    pos = lax.broadcasted_iota(jnp.int32, score.shape, axis).astype(F32)
    picks = []
    for _ in range(n_top):
        mx = jnp.max(score, axis=axis, keepdims=True)
        idx = jnp.min(jnp.where(score == mx, pos, float(score.shape[axis])), axis=axis, keepdims=True)
        picks.append(idx)
        score = jnp.where(pos == idx, NEG, score)
    return picks, score


def _nsa_prompt_kernel(tk, q_ref, gt_ref, fs_ref, cb_ref, kc_c, kc_s1, kc_s2, ovt_ref,
                       ks_ref, kw_ref, vst_ref, vwt_ref, oa_ref,
                       kc_s, vct_s, sel_s, m_s, l_s, acc_s, sc_s, part_s):
    n_cmp = fs_ref.shape[2]
    n_sel = ovt_ref.shape[0]
    nq = Q_BLOCK
    cols = A_REP * nq
    per_tile = tk // L_SEL
    qi = pl.program_id(1)
    q0 = qi * nq

    @pl.when(qi == 0)
    def _():
        kc, vc = _compressed_kv(fs_ref, cb_ref, (kc_c[...], kc_s1[...], kc_s2[...]))
        kc_s[...] = kc.astype(BF16)
        vct_s[...] = vc.T.astype(BF16)

    t_pos = q0 + lax.broadcasted_iota(jnp.int32, (1, nq), 1)
    gates = gt_ref[0]

    def heads(a):
        return jnp.concatenate([a] * A_REP, axis=1)

    def all_heads(a):
        return jnp.concatenate([a] * A_HEADS, axis=1)

    cmp_end = lax.broadcasted_iota(jnp.int32, (n_cmp, 1), 0) * D_CMP + (L_CMP - 1)
    c_bias = jnp.where(cmp_end <= t_pos, 0.0, NEG)
    c_keep = jnp.where(t_pos >= L_CMP - 1, 1.0, 0.0)
    blk = lax.broadcasted_iota(jnp.int32, (n_sel, 1), 0)
    cur = t_pos // L_SEL
    forced = (blk == 0) | (blk == cur) | (blk == cur - 1)
    started = blk * L_SEL <= t_pos
    w_start = jnp.maximum(q0 - WINDOW, 0)
    w_len = WINDOW + nq
    rel = t_pos - (w_start + lax.broadcasted_iota(jnp.int32, (w_len, 1), 0))
    w_bias = jnp.where((rel >= 0) & (rel < WINDOW), 0.0, NEG)
    n_tiles = (q0 + nq + tk - 1) // tk
    key_iota = lax.broadcasted_iota(jnp.int32, (tk, 1), 0)

    groups = range(A_KV)
    vs = [slice(g * A_DIM, (g + 1) * A_DIM) for g in groups]
    gs = [slice(g * cols, (g + 1) * cols) for g in groups]
    q_all = q_ref[...].reshape(A_HEADS * nq, LANES)

    p_c = _bias_softmax(_dot_nt(kc_s[...], q_all) + all_heads(c_bias), 0, all_heads(c_keep))
    p_c16 = p_c.astype(BF16)
    o_c = [_dot(vct_s[vs[g], :], p_c16[:, gs[g]]) for g in groups]
    for g in groups:
        p_sum = p_c[:, g * cols:g * cols + nq]
        for r in range(1, A_REP):
            p_sum = p_sum + p_c[:, g * cols + r * nq:g * cols + (r + 1) * nq]
        imp = _dot_split3(ovt_ref[...], p_sum)
        score = jnp.where(forced, FORCE_SCORE, jnp.where(started, imp, -1.0))
        _, left = _top_blocks(score, min(N_TOP, n_sel), 0)
        sel_s[g] = jnp.where(left < -2.0, 1.0, 0.0)

    ws = pl.multiple_of(w_start, nq)
    p_w = _bias_softmax(_dot_nt(kw_ref[0, pl.ds(ws, w_len), :], q_all) + all_heads(w_bias), 0).astype(BF16)
    wc = w_start // LANES
    for g in groups:
        o_w = None
        for c in range(w_len // LANES):
            term = _dot(vwt_ref[0, wc + c, vs[g], :], p_w[c * LANES:(c + 1) * LANES, gs[g]])
            o_w = term if o_w is None else o_w + term
        for r in range(A_REP):
            hd = g * A_REP + r
            base = 2 * M_HEADS + hd * N_BRANCH
            cs = slice(r * nq, (r + 1) * nq)
            part_s[hd * A_DIM:(hd + 1) * A_DIM, :] = (gates[base:base + 1] * o_c[g][:, cs]
                                                      + gates[base + 2:base + 3] * o_w[:, cs])

    m_s[...] = jnp.full(m_s.shape, NEG, F32)
    l_s[...] = jnp.zeros(l_s.shape, F32)
    acc_s[...] = jnp.zeros(acc_s.shape, F32)

    last_tile = ks_ref.shape[1] // tk - 1

    def scores(kt):
        k0 = pl.multiple_of(jnp.minimum(kt, last_tile) * tk, tk)
        return _dot_nt(ks_ref[0, pl.ds(k0, tk), :], q_all)

    def attend(kt, slot):
        kc = jnp.minimum(kt, last_tile)
        causal = kt * tk + key_iota <= t_pos
        bias = []
        for g in groups:
            chosen = jnp.concatenate(
                [jnp.broadcast_to(sel_s[g, pl.ds(kc * per_tile + i, 1), :], (L_SEL, nq)) for i in range(per_tile)],
                axis=0)
            bias.append(heads(jnp.where(jnp.where(causal, chosen, 0.0) > 0.5, 0.0, NEG)))
        s = sc_s[slot] + jnp.concatenate(bias, axis=1)
        m_old = m_s[...]
        m_new = jnp.maximum(m_old, jnp.max(s, axis=0, keepdims=True))
        e = jnp.exp2(s - m_new)
        a = jnp.exp2(m_old - m_new)
        l_s[...] = a * l_s[...] + jnp.sum(e, axis=0, keepdims=True)
        e16 = e.astype(BF16)
        for g in groups:
            acc_s[g] = a[:, gs[g]] * acc_s[g] + _dot(vst_ref[0, kc, vs[g], :], e16[:, gs[g]])
        m_s[...] = m_new

    sc_s[0] = scores(0)

    def pair(j, carry):
        sc_s[1] = scores(2 * j + 1)
        attend(2 * j, 0)
        sc_s[0] = scores(2 * j + 2)
        attend(2 * j + 1, 1)
        return carry

    lax.fori_loop(0, (n_tiles + 1) // 2, pair, 0)

    l = l_s[...]
    inv_l = 1.0 / jnp.where(l > 0.0, l, 1.0)
    heads_out = []
    for g in groups:
        o_s = acc_s[g] * inv_l[:, gs[g]]
        for r in range(A_REP):
            hd = g * A_REP + r
            gate = gates[2 * M_HEADS + hd * N_BRANCH + 1:2 * M_HEADS + hd * N_BRANCH + 2]
            heads_out.append(part_s[hd * A_DIM:(hd + 1) * A_DIM, :] + gate * o_s[:, r * nq:(r + 1) * nq])
    oa_ref[0] = jnp.concatenate(heads_out, axis=0).T.astype(BF16)


def _nsa_prompt(q8, gt, fs, cbias, kc_tabs, ovt, kk, vst, vwt, B, T, tk):
    nqb = T // Q_BLOCK
    n_cmp = fs.shape[2]
    n_sel = ovt.shape[0]
    cols = A_REP * Q_BLOCK
    const = lambda a: pl.BlockSpec(a.shape, lambda b, i: (0,) * a.ndim)
    keys = lambda j: pl.BlockSpec((1, T, LANES), lambda b, i: (b, 0, j))
    per_b = lambda a: pl.BlockSpec((1,) + a.shape[1:], lambda b, i: (b, 0, 0, 0))
    return pl.pallas_call(
        functools.partial(_nsa_prompt_kernel, tk),
        grid=(B, nqb),
        in_specs=[pl.BlockSpec((A_HEADS, Q_BLOCK, LANES), lambda b, i: (0, b * nqb + i, 0)),
                  pl.BlockSpec((1, gt.shape[1], Q_BLOCK), lambda b, i: (b, 0, i)),
                  pl.BlockSpec((2, 1, n_cmp, 2 * LANES), lambda b, i: (0, b, 0, 0)),
                  const(cbias), const(kc_tabs[0]), const(kc_tabs[1]), const(kc_tabs[2]),
                  const(ovt), keys(0), keys(1), per_b(vst), per_b(vwt)],
        out_specs=pl.BlockSpec((1, Q_BLOCK, A_WIDTH), lambda b, i: (b, i, 0)),
        out_shape=jax.ShapeDtypeStruct((B, T, A_WIDTH), BF16),
        scratch_shapes=[pltpu.VMEM((n_cmp, LANES), BF16), pltpu.VMEM((LANES, n_cmp), BF16),
                        pltpu.VMEM((A_KV, n_sel, Q_BLOCK), F32),
                        pltpu.VMEM((1, A_KV * cols), F32), pltpu.VMEM((1, A_KV * cols), F32),
                        pltpu.VMEM((A_KV, A_DIM, cols), F32),
                        pltpu.VMEM((2, tk, A_KV * cols), F32),
                        pltpu.VMEM((A_WIDTH, Q_BLOCK), F32)],
        compiler_params=_params("parallel", "arbitrary"),
        name="nsa_prompt",
    )(q8, gt, fs, cbias, *kc_tabs, ovt, kk, kk, vst, vwt)


def _nsa_sample_select_kernel(q_pos, n_sel, q_ref, fs_ref, cb_ref, kc_c, kc_s1, kc_s2, ov_ref, oc_ref, idx_ref):
    n_cmp = fs_ref.shape[2]
    kc, vc = _compressed_kv(fs_ref, cb_ref, (kc_c[...], kc_s1[...], kc_s2[...]))
    q = q_ref[0]
    cmp_end = lax.broadcasted_iota(jnp.int32, (A_HEADS, n_cmp), 1) * D_CMP + (L_CMP - 1)
    p_c = _masked_softmax(_dot_nt(q, kc.astype(BF16)), cmp_end <= q_pos, 1)
    oc_ref[0] = _dot(p_c.astype(BF16), vc.astype(BF16))
    head = lax.broadcasted_iota(jnp.int32, (A_HEADS, n_cmp), 0)
    p_sum = jnp.zeros((A_HEADS, n_cmp), F32)
    for g in range(A_KV):
        pg = jnp.sum(jnp.where(head // A_REP == g, p_c, 0.0), axis=0, keepdims=True)
        p_sum = jnp.where(head == g, pg, p_sum)
    imp = _dot_exact(p_sum, ov_ref[...])
    n_pad = ov_ref.shape[1]
    blk = lax.broadcasted_iota(jnp.int32, (A_HEADS, n_pad), 1)
    cur = q_pos // L_SEL
    forced = (blk == 0) | (blk == cur) | (blk == cur - 1)
    score = jnp.where(forced, FORCE_SCORE, jnp.where(blk * L_SEL <= q_pos, imp, -1.0))
    score = jnp.where(blk < n_sel, score, 2 * NEG)
    picks, _ = _top_blocks(score, min(N_TOP, n_sel), 1)
    lane = lax.broadcasted_iota(jnp.int32, (A_HEADS, LANES), 1)
    out = jnp.zeros((A_HEADS, LANES), jnp.int32)
    for j, idx in enumerate(picks):
        out = jnp.where(lane == j, idx.astype(jnp.int32), out)
    idx_ref[0] = out


def _nsa_sample_select(q8, fs, cbias, kc_tabs, ov, q_pos, n_sel):
    B = q8.shape[0]
    n_cmp = fs.shape[2]
    const = lambda a: pl.BlockSpec(a.shape, lambda b: (0,) * a.ndim)
    per_b = pl.BlockSpec((1, A_HEADS, LANES), lambda b: (b, 0, 0))
    return pl.pallas_call(
        functools.partial(_nsa_sample_select_kernel, q_pos, n_sel),
        grid=(B,),
        in_specs=[per_b, pl.BlockSpec((2, 1, n_cmp, 2 * LANES), lambda b: (0, b, 0, 0)),
                  const(cbias), const(kc_tabs[0]), const(kc_tabs[1]), const(kc_tabs[2]), const(ov)],
        out_specs=(per_b, per_b),
        out_shape=(jax.ShapeDtypeStruct((B, A_HEADS, LANES), F32),
                   jax.ShapeDtypeStruct((B, A_HEADS, LANES), jnp.int32)),
        compiler_params=_params("parallel"),
        name="nsa_sample_select",
    )(q8, fs, cbias, *kc_tabs, ov)


def _r16(a):
    return a.astype(BF16).astype(F32)


def _nsa_sample_attend_kernel(q_pos, n_top, pt_ref, ix_ref, q_ref, new_ref, newc_ref, oc_ref, misc_ref, win_ref,
                              *refs):
    page_refs, (oa_ref, nwin_ref) = refs[:A_KV * n_top], refs[A_KV * n_top:]
    b = pl.program_id(0)
    q = q_ref[0]
    qf = q.astype(F32)
    head = lax.broadcasted_iota(jnp.int32, (A_HEADS, 1), 0)
    new = new_ref[0]
    page = page_refs[0].shape[2]
    per_page = page // L_SEL
    lane = lax.broadcasted_iota(jnp.int32, (A_HEADS, page), 1)
    s_tok = jnp.sum(qf * _r16(new[0:1, :]), axis=-1, keepdims=True)

    o_s = jnp.zeros((A_HEADS, LANES), F32)
    for g in range(A_KV):
        parts, n_fresh = [], 0
        for j in range(n_top):
            blk = ix_ref[b, g, j]
            fresh = blk * L_SEL >= q_pos
            limit = jnp.where(fresh, -1, q_pos)
            s = _dot(q, page_refs[g * n_top + j][0].astype(BF16))
            s_pos = (blk // per_page) * page + lane
            ok = jnp.where(lane // L_SEL == blk % per_page, s_pos, limit + 1) <= limit
            parts.append(jnp.where(ok, s, NEG))
            n_fresh = n_fresh + jnp.where(fresh, 1, 0)
        s_all = jnp.concatenate(parts, axis=1)
        has_tok = jnp.where(head >= 0, n_fresh, 0) > 0
        m = jnp.maximum(jnp.max(s_all, axis=-1, keepdims=True), jnp.where(has_tok, s_tok, NEG))
        e = jnp.exp(s_all - m)
        e_tok = jnp.where(has_tok, jnp.exp(s_tok - m), 0.0)
        den = jnp.sum(e, axis=-1, keepdims=True) + e_tok
        e16 = e.astype(BF16)
        acc = _r16(e_tok) * _r16(new[1:2, :])
        for j in range(n_top):
            acc = acc + _dot_nt(e16[:, j * page:(j + 1) * page], page_refs[g * n_top + j][1].astype(BF16))
        o_s = jnp.where(head // A_REP == g, acc / den, o_s)

    wb = win_ref.shape[2]
    kwt, vwt = win_ref[0], win_ref[1]
    s_old = _dot(q, kwt.astype(BF16))
    s_new = jnp.sum(qf * _r16(new[2:3, :]), axis=-1, keepdims=True)
    rel = wb - lax.broadcasted_iota(jnp.int32, (A_HEADS, wb), 1)
    ok = (rel >= 0) & (rel < WINDOW) & (q_pos - rel >= 0)
    m = jnp.maximum(jnp.max(jnp.where(ok, s_old, NEG), axis=-1, keepdims=True), s_new)
    e_old = jnp.where(ok, jnp.exp(s_old - m), 0.0)
    e_new = jnp.exp(s_new - m)
    den = jnp.sum(e_old, axis=-1, keepdims=True) + e_new
    o_w = _dot_nt((e_old / den).astype(BF16), vwt.astype(BF16)) + _r16(e_new / den) * _r16(new[3:4, :])
    gates = misc_ref[0]
    oa_ref[0] = gates[:, 0:1] * oc_ref[0] + gates[:, 1:2] * o_s + gates[:, 2:3] * o_w
    newc = newc_ref[0]
    last = lax.broadcasted_iota(jnp.int32, (LANES, wb), 1) == wb - 1
    nwin_ref[0] = jnp.where(last, newc[:, 2:3], pltpu.roll(kwt, wb - 1, 1))
    nwin_ref[1] = jnp.where(last, newc[:, 3:4], pltpu.roll(vwt, wb - 1, 1))


def _nsa_sample_attend(page_table, idx, q8, new_rows, new_cols, o_c, gate_rows, win_t, cache_t, layer, q_pos):
    B, n_pages = page_table.shape
    n_top = idx.shape[2]
    page = cache_t.shape[4]
    per_page = page // L_SEL
    wb = win_t.shape[4]

    def page_spec(g, j):
        def index(b, pt, ix):
            pg = jnp.clip(ix[b, g, j] // per_page, 0, n_pages - 1)
            return (pt[b, pg], layer, 1, 0, 0)
        return pl.BlockSpec((None, None, 2, KV_WIDTH, page), index)

    per_b = pl.BlockSpec((1, A_HEADS, LANES), lambda b, pt, ix: (b, 0, 0))
    pages = [page_spec(g, j) for g in range(A_KV) for j in range(n_top)]
    grid_spec = pltpu.PrefetchScalarGridSpec(
        num_scalar_prefetch=2,
        grid=(B,),
        in_specs=[per_b, per_b, pl.BlockSpec((1, KV_WIDTH, A_HEADS), lambda b, pt, ix: (b, 0, 0)), per_b, per_b,
                  pl.BlockSpec((None, None, 2, KV_WIDTH, wb), lambda b, pt, ix: (b, layer, 0, 0, 0))] + pages,
        out_specs=(per_b, pl.BlockSpec((None, 2, KV_WIDTH, wb), lambda b, pt, ix: (b, 0, 0, 0))),
    )
    return pl.pallas_call(
        functools.partial(_nsa_sample_attend_kernel, q_pos, n_top),
        grid_spec=grid_spec,
        out_shape=(jax.ShapeDtypeStruct((B, A_HEADS, LANES), F32),
                   jax.ShapeDtypeStruct((B, 2, KV_WIDTH, wb), F32)),
        compiler_params=_params("parallel"),
        name="nsa_sample_attend",
    )(page_table, idx, q8, new_rows, new_cols, o_c, gate_rows, win_t, *([cache_t] * len(pages)))


def _ffn_kernel(final, x_ref, hm_ref, oa_ref, wo_ref, nf_ref, wg_ref, wu_ref, wd_ref, nfin_ref, out_ref,
                x1_s, xn_s, acc_s):
    j = pl.program_id(1)

    @pl.when(j == 0)
    def _():
        half = hm_ref.shape[1]
        x1 = x_ref[...] + _dot(hm_ref[...], wo_ref[0:half, :]) + _dot(oa_ref[...], wo_ref[half:, :])
        x1_s[...] = x1
        ms = jnp.mean(x1 * x1, axis=-1, keepdims=True)
        xn_s[...] = ((x1 * lax.rsqrt(ms + EPS)) * nf_ref[...]).astype(BF16)
        acc_s[...] = jnp.zeros(acc_s.shape, F32)

    xn = xn_s[...]
    gate = _dot(xn, wg_ref[...])
    act = (gate * _sigmoid(gate)) * _dot(xn, wu_ref[...])
    acc_s[...] += _dot(act.astype(BF16), wd_ref[...])

    @pl.when(j == pl.num_programs(1) - 1)
    def _():
        y = x1_s[...] + acc_s[...]
        if final:
            ms = jnp.mean(y * y, axis=-1, keepdims=True)
            y = (y * lax.rsqrt(ms + EPS)) * nfin_ref[...]
        out_ref[...] = y


def _ffn(x, hm, oa, wo, nf, wg, wu, wd, nfin, final, tm, tf):
    M, D = x.shape
    F = wg.shape[1]
    row = lambda w: pl.BlockSpec((tm, w), lambda i, j: (i, 0))
    const = lambda a: pl.BlockSpec(a.shape, lambda i, j: (0,) * a.ndim)
    return pl.pallas_call(
        functools.partial(_ffn_kernel, final),
        grid=(M // tm, F // tf),
        in_specs=[row(D), row(hm.shape[1]), row(oa.shape[1]), const(wo), const(nf),
                  pl.BlockSpec((D, tf), lambda i, j: (0, j)), pl.BlockSpec((D, tf), lambda i, j: (0, j)),
                  pl.BlockSpec((tf, D), lambda i, j: (j, 0)), const(nfin)],
        out_specs=row(D),
        out_shape=jax.ShapeDtypeStruct((M, D), F32),
        scratch_shapes=[pltpu.VMEM((tm, D), F32), pltpu.VMEM((tm, D), BF16), pltpu.VMEM((tm, D), F32)],
        compiler_params=_params("parallel", "arbitrary"),
        name="ffn",
    )(x, hm, oa, wo, nf, wg, wu, wd, nfin)


def _permute_w_in(w):
    a = 4 * M_WIDTH
    b = a + 2 * M_HEADS
    c = b + A_WIDTH + 6 * KV_WIDTH
    pad = jnp.zeros((w.shape[0], LANES - N_MISC), w.dtype)
    return jnp.concatenate([w[:, :a], w[:, b:c], w[:, a:b], w[:, c:], pad], axis=1).astype(BF16)


def _cmp_weights(wk, wv):
    def one(w):
        z = jnp.zeros((D_CMP, A_DIM, A_DIM), w.dtype)
        halves = []
        for part in (w[:D_CMP], w[D_CMP:]):
            top = jnp.concatenate([part, z], axis=2)
            bot = jnp.concatenate([z, part], axis=2)
            halves.append(jnp.concatenate([top, bot], axis=1))
        return jnp.concatenate(halves, axis=2)
    return jnp.stack([one(wk), one(wv)]).astype(BF16).reshape(2, D_CMP // 2, 2 * KV_WIDTH, 2 * LANES)


def _overlap(n_cmp_pad, n_sel_pad, n_cmp, n_sel):
    lo_c = np.arange(n_cmp_pad)[:, None] * D_CMP
    lo_s = np.arange(n_sel_pad)[None, :] * L_SEL
    ov = np.clip(np.minimum(lo_c + L_CMP, lo_s + L_SEL) - np.maximum(lo_c, lo_s), 0, None).astype(np.float32) / L_CMP
    ov[n_cmp:, :] = 0.0
    ov[:, n_sel:] = 0.0
    return jnp.asarray(ov)


def _pick(n, prefs):
    for p in prefs:
        if n % p == 0:
            return p
    return n


def kernel(x_prompt, x_sample, cache_nsa_kv, page_table, state_win_kv, state_mlstm_c, state_mlstm_n, state_mlstm_m, norm_mix, w_in, b_if, m_norm, w_cmp_k, b_cmp_k, w_cmp_v, b_cmp_v, w_out, norm_ffn, w_gate, w_up, w_down, norm_final):
    B, T, D = x_prompt.shape
    Bs, Ts, _ = x_sample.shape
    depth = w_in.shape[0]
    n_pool, _, n_slots, page = cache_nsa_kv.shape[:4]
    n_pages = page_table.shape[1]
    past = n_pages * page
    wb = state_win_kv.shape[3]
    assert Ts == 1 and T % Q_BLOCK == 0 and T >= WINDOW + Q_BLOCK and wb == WINDOW
    assert past % page == 0 and page % L_SEL == 0 and page % D_CMP == 0
    assert ((past + Ts) // D_CMP) * D_CMP <= past

    Mp = B * T
    tm_p = _pick(Mp, (512, 256, 128))
    tm_f = _pick(Mp, (1024, 512, 256, 128))
    ff = w_gate.shape[2]
    tf = _pick(ff, (256, 128))
    l_chunk = _pick(T, (256, 128, 64))
    ls_pad = 16
    tk = _pick(T, (512, 256, 128))
    assert tk % tm_p == 0 and tm_p % LANES == 0
    n_group = _pick(n_pages, (16, 8, 4, 2, 1))
    tr = _pick(T, (2048, 1024, 512, 256))

    n_chunk_p = T // D_CMP
    n_cmp_p, n_sel_p = n_chunk_p - 1, -(-T // L_SEL)
    tabs_p = _rope_tables(jnp.arange(T))
    kc_tabs_p = _rope_tables(jnp.arange(n_chunk_p) * D_CMP)
    assert L_CMP <= 256
    ovt_p = _overlap(n_chunk_p, n_sel_p, n_cmp_p, n_sel_p).T.astype(BF16)
    n_chunk_s = (past + Ts) // D_CMP
    n_cmp_s, n_sel_s = n_chunk_s - 1, -(-(past + Ts) // L_SEL)
    n_sel_pad = -(-n_sel_s // LANES) * LANES
    tabs_s = _rope_tables(jnp.full((Bs,), past))
    kc_tabs_s = _rope_tables(jnp.arange(n_chunk_s) * D_CMP)
    ov_s = _overlap(n_chunk_s, n_sel_pad, n_cmp_s, n_sel_s)
    cache_t = cache_nsa_kv.transpose(0, 1, 2, 4, 5, 3).reshape(n_pool, depth, n_slots, KV_WIDTH, page)
    win_t = state_win_kv.transpose(0, 1, 2, 4, 5, 3).reshape(Bs, depth, 2, KV_WIDTH, wb)

    xp = x_prompt.reshape(Mp, D)
    xs = x_sample.reshape(Bs, D)
    rows_p, rows_s, win_p, win_s, c_p, c_s, n_p, n_s, m_p, m_s = ([] for _ in range(10))
    y_p = y_s = None
    nfin = norm_final.reshape(1, D)
    for l in range(depth):
        w_perm = _permute_w_in(w_in[l])
        gain = norm_mix[l].reshape(1, D)
        bias = jnp.zeros((1, LANES), F32).at[0, :2 * M_HEADS].set(b_if[l].reshape(-1))
        mg = m_norm[l].reshape(1, M_WIDTH)
        w_cmp = _cmp_weights(w_cmp_k[l], w_cmp_v[l])
        cbias = jnp.stack([jnp.tile(b_cmp_k[l], A_KV), jnp.tile(b_cmp_v[l], A_KV)])
        wo, nf = w_out[l].astype(BF16), norm_ffn[l].reshape(1, D)
        wg, wu, wd = w_gate[l].astype(BF16), w_up[l].astype(BF16), w_down[l].astype(BF16)
        final = l == depth - 1

        qkv, og, q8, rows, win, misc, kk, vst, vwt = _inproj(xp, gain, w_perm, bias, tabs_p, tm_p, tk)
        misc3 = misc.reshape(B, T, LANES)
        gt = misc3[:, :, :N_MISC].transpose(0, 2, 1)
        hm, c_new, n_new, m_new = _mlstm(
            qkv.reshape(B, T, -1), og.reshape(B, T, -1), misc3, gt, mg,
            jnp.zeros((B, M_HEADS, M_DIM, M_DIM), F32), jnp.zeros((B, M_HEADS, M_DIM), F32),
            jnp.full((B, M_HEADS, M_DIM), NEG, F32), l_chunk)
        fs = _cmp_prompt(rows, w_cmp, tr).reshape(2, B, n_chunk_p, 2 * LANES)
        oa = _nsa_prompt(q8, gt, fs, cbias, kc_tabs_p, ovt_p, kk.reshape(B, T, -1),
                         vst.reshape(B, T // tk, KV_WIDTH, tk), vwt.reshape(B, T // LANES, KV_WIDTH, LANES), B, T, tk)
        xp_new = _ffn(xp, hm.reshape(Mp, -1), oa.reshape(Mp, -1), wo, nf, wg, wu, wd, nfin, final, tm_f, tf)
        if final:
            y_p = xp_new
        xp = xp_new
        rows_p.append(rows.reshape(B, T, n_slots, A_KV, A_DIM))
        win_p.append(win.reshape(B, T, 2, A_KV, A_DIM)[:, T - wb:].transpose(0, 2, 1, 3, 4))
        c_p.append(c_new); n_p.append(n_new); m_p.append(m_new[:, :, 0])

        qkv, og, q8, rows, win, misc = _inproj(xs, gain, w_perm, bias, tabs_s, Bs, 0)
        pad_t = lambda a: jnp.pad(a[:, None, :], ((0, 0), (0, ls_pad - 1), (0, 0)))
        inert = jnp.zeros((ls_pad, LANES), F32).at[1:, :M_HEADS].set(NEG)
        misc_pad = pad_t(misc) + inert[None]
        gt = misc_pad[:, :, :2 * M_HEADS].transpose(0, 2, 1)
        m0 = jnp.broadcast_to(state_mlstm_m[:, l, :, None].astype(F32), (Bs, M_HEADS, M_DIM))
        hm, c_new, n_new, m_new = _mlstm(
            pad_t(qkv), pad_t(og), misc_pad, gt, mg,
            state_mlstm_c[:, l].astype(F32), state_mlstm_n[:, l].astype(F32), m0, ls_pad)
        hm = hm[:, 0]
        fs = _cmp_sample(cache_t, page_table, l, w_cmp, n_group)
        q8s = q8.transpose(1, 0, 2)
        o_c, idx = _nsa_sample_select(q8s, fs, cbias, kc_tabs_s, ov_s, past, n_sel_s)
        idx = idx[:, :A_KV, :min(N_TOP, n_sel_s)]
        new_rows = jnp.pad(jnp.stack([rows[:, 2 * LANES:3 * LANES], rows[:, 3 * LANES:], win[:, :LANES],
                                      win[:, LANES:]], axis=1), ((0, 0), (0, A_HEADS - 4), (0, 0)))
        gate_rows = jnp.pad(misc[:, 2 * M_HEADS:N_MISC].reshape(Bs, A_HEADS, N_BRANCH),
                            ((0, 0), (0, 0), (0, LANES - N_BRANCH)))
        oa8, nwin = _nsa_sample_attend(page_table, idx, q8s, new_rows, new_rows.transpose(0, 2, 1), o_c, gate_rows,
                                       win_t, cache_t, l, past)
        grp = (jnp.arange(A_HEADS) // A_REP)[None, :, None, None]
        oa = jnp.take_along_axis(oa8.reshape(Bs, A_HEADS, A_KV, A_DIM), jnp.broadcast_to(grp, (Bs, A_HEADS, 1, A_DIM)),
                                 axis=2).reshape(Bs, A_WIDTH).astype(BF16)
        xs_new = _ffn(xs, hm, oa, wo, nf, wg, wu, wd, nfin, final, Bs, tf)
        if final:
            y_s = xs_new
        xs = xs_new
        rows_s.append(rows.reshape(Bs, Ts, n_slots, A_KV, A_DIM))
        win_s.append(nwin.reshape(Bs, 2, A_KV, A_DIM, wb).transpose(0, 1, 4, 2, 3))
        c_s.append(c_new); n_s.append(n_new); m_s.append(m_new[:, :, 0])

    return (y_p.reshape(B, T, D), y_s.reshape(Bs, Ts, D),
            jnp.stack(rows_p, axis=2), jnp.stack(rows_s, axis=2),
            jnp.stack(win_p, axis=1), jnp.stack(win_s, axis=1),
            jnp.stack(c_p, axis=1), jnp.stack(c_s, axis=1),
            jnp.stack(n_p, axis=1), jnp.stack(n_s, axis=1),
            jnp.stack(m_p, axis=1), jnp.stack(m_s, axis=1))
```

```python
import functools

import jax
import jax.numpy as jnp
import numpy as np
from jax import lax
from jax.experimental import pallas as pl
from jax.experimental.pallas import tpu as pltpu

F32 = jnp.float32
BF16 = jnp.bfloat16

M_HEADS = 4
M_DIM = 128
M_WIDTH = M_HEADS * M_DIM
A_HEADS = 8
A_DIM = 64
A_KV = 2
A_REP = A_HEADS // A_KV
A_WIDTH = A_HEADS * A_DIM
KV_WIDTH = A_KV * A_DIM
L_CMP = 32
D_CMP = 16
L_SEL = 64
N_TOP = 16
WINDOW = 512
Q_BLOCK = 128
N_BRANCH = 3
ROPE_THETA = 500000.0
ROPE_DIM = A_DIM // 4
EPS = 1e-6
NEG = -1e30
LOG2_E = 1.4426950408889634
FORCE_SCORE = 1e4
LANES = 128
N_MISC = 2 * M_HEADS + N_BRANCH * A_HEADS
D_PERM = 4 * M_WIDTH + A_WIDTH + 6 * KV_WIDTH + LANES
VMEM_LIMIT = 56 * 1024 * 1024

_NT = (((1,), (1,)), ((), ()))
_TN = (((0,), (0,)), ((), ()))


def _params(*sem):
    return pltpu.CompilerParams(dimension_semantics=sem, vmem_limit_bytes=VMEM_LIMIT)


def _sigmoid(x):
    return 1.0 / (1.0 + jnp.exp(-x))


def _dot(a, b):
    return jnp.dot(a, b, preferred_element_type=F32)


def _dot_exact(a, b):
    return jnp.dot(a, b, preferred_element_type=F32, precision=lax.Precision.HIGHEST)


def _dot_nt(a, b):
    return lax.dot_general(a, b, _NT, preferred_element_type=F32)


def _rope128(v, c, s1, s2):
    half = ROPE_DIM // 2
    return v * c + pltpu.roll(v, LANES - half, 1) * s1 + pltpu.roll(v, half, 1) * s2


def _rope_tables(pos):
    half = ROPE_DIM // 2
    inv = ROPE_THETA ** (-jnp.arange(half, dtype=F32) / half)
    ang = pos.astype(F32)[:, None] * inv[None, :]
    cos, sin = jnp.cos(ang), jnp.sin(ang)
    n = pos.shape[0]
    one = jnp.ones((n, A_DIM - ROPE_DIM), F32)
    zero = jnp.zeros((n, A_DIM - ROPE_DIM), F32)
    zh = jnp.zeros((n, half), F32)
    c = jnp.concatenate([cos, cos, one], axis=1)
    s1 = jnp.concatenate([-sin, zh, zero], axis=1)
    s2 = jnp.concatenate([zh, sin, zero], axis=1)
    tile = lambda a: jnp.concatenate([a, a], axis=1)
    return tile(c), tile(s1), tile(s2)


def _inproj_kernel(q_scale, x_ref, g_ref, w_ref, bias_ref, rc_ref, rs1_ref, rs2_ref,
                   om_ref, og_ref, oq_ref, orows_ref, owin_ref, omisc_ref, *attn_refs):
    x = x_ref[...]
    ms = jnp.mean(x * x, axis=-1, keepdims=True)
    xn = ((x * lax.rsqrt(ms + EPS)) * g_ref[...]).astype(BF16)
    c, s1, s2 = rc_ref[...], rs1_ref[...], rs2_ref[...]
    lane = lax.broadcasted_iota(jnp.int32, (x.shape[0], LANES), 1)

    def proj(a, b):
        return _dot(xn, w_ref[:, a:b])

    om_ref[:, 0:M_WIDTH] = (proj(0, M_WIDTH) * (M_DIM ** -0.5)).astype(BF16)
    om_ref[:, M_WIDTH:3 * M_WIDTH] = proj(M_WIDTH, 3 * M_WIDTH).astype(BF16)
    og_ref[...] = _sigmoid(proj(3 * M_WIDTH, 4 * M_WIDTH))

    off = 4 * M_WIDTH
    pq = proj(off, off + A_WIDTH)
    for j in range(A_WIDTH // LANES):
        blk = _rope128(pq[:, j * LANES:(j + 1) * LANES], c, s1, s2) * q_scale
        swapped = pltpu.roll(blk, A_DIM, 1)
        for e in range(2):
            hd = 2 * j + e
            grp = hd // A_REP
            src = blk if e == grp else swapped
            keep = (lane // A_DIM) == grp
            oq_ref[hd] = jnp.where(keep, src, 0.0).astype(BF16)

    off += A_WIDTH
    pk = proj(off, off + 6 * KV_WIDTH)
    kcr, vcr, ksl, vsl, kw, vw = [pk[:, i * LANES:(i + 1) * LANES] for i in range(6)]
    ksl = _rope128(ksl, c, s1, s2)
    kw = _rope128(kw, c, s1, s2)
    for i, a in enumerate((kcr, vcr, ksl, vsl)):
        orows_ref[:, i * LANES:(i + 1) * LANES] = a
    owin_ref[:, 0:LANES] = kw
    owin_ref[:, LANES:2 * LANES] = vw
    if attn_refs:
        okk_ref, ovst_ref, ovwt_ref = attn_refs
        okk_ref[:, 0:LANES] = ksl.astype(BF16)
        okk_ref[:, LANES:2 * LANES] = kw.astype(BF16)
        ovst_ref[0] = vsl.T.astype(BF16)
        for i in range(ovwt_ref.shape[0]):
            ovwt_ref[i] = vw[i * LANES:(i + 1) * LANES].T.astype(BF16)

    off += 6 * KV_WIDTH
    pm = proj(off, off + LANES) + bias_ref[...]
    log_sig = -(jnp.maximum(-pm, 0.0) + jnp.log(1.0 + jnp.exp(-jnp.abs(pm))))
    omisc_ref[...] = jnp.where(lane < M_HEADS, pm, jnp.where(lane < 2 * M_HEADS, log_sig, _sigmoid(pm)))


def _inproj(x, gain, w_perm, bias, tabs, tm, attn):
    M, D = x.shape
    nt = tabs[0].shape[0] // tm
    row = lambda i: (i, 0)
    tab = pl.BlockSpec((tm, LANES), lambda i: (i % nt, 0))
    full = lambda a: pl.BlockSpec(a.shape, lambda i: (0,) * a.ndim)
    out_shape = (
        jax.ShapeDtypeStruct((M, 3 * M_WIDTH), BF16),
        jax.ShapeDtypeStruct((M, M_WIDTH), F32),
        jax.ShapeDtypeStruct((A_HEADS, M, LANES), BF16),
        jax.ShapeDtypeStruct((M, 4 * KV_WIDTH), F32),
        jax.ShapeDtypeStruct((M, 2 * KV_WIDTH), F32),
        jax.ShapeDtypeStruct((M, LANES), F32),
    )
    out_specs = (
        pl.BlockSpec((tm, 3 * M_WIDTH), row),
        pl.BlockSpec((tm, M_WIDTH), row),
        pl.BlockSpec((A_HEADS, tm, LANES), lambda i: (0, i, 0)),
        pl.BlockSpec((tm, 4 * KV_WIDTH), row),
        pl.BlockSpec((tm, 2 * KV_WIDTH), row),
        pl.BlockSpec((tm, LANES), row),
    )
    if attn:
        tk = attn
        per = tk // tm
        out_shape += (jax.ShapeDtypeStruct((M, 2 * KV_WIDTH), BF16),
                      jax.ShapeDtypeStruct((M // tk, KV_WIDTH, tk), BF16),
                      jax.ShapeDtypeStruct((M // LANES, KV_WIDTH, LANES), BF16))
        out_specs += (pl.BlockSpec((tm, 2 * KV_WIDTH), row),
                      pl.BlockSpec((1, KV_WIDTH, tm), lambda i: (i // per, 0, i % per)),
                      pl.BlockSpec((tm // LANES, KV_WIDTH, LANES), lambda i: (i, 0, 0)))
    return pl.pallas_call(
        functools.partial(_inproj_kernel, A_DIM ** -0.5 * (LOG2_E if attn else 1.0)),
        grid=(M // tm,),
        in_specs=[pl.BlockSpec((tm, D), row), full(gain), full(w_perm), full(bias), tab, tab, tab],
        out_specs=out_specs,
        out_shape=out_shape,
        compiler_params=_params("parallel"),
        name="inproj",
    )(x, gain, w_perm, bias, *tabs)


def _mlstm_kernel(q_ref, k_ref, v_ref, og_ref, misc_ref, gt_ref, mg_ref, c0_ref, n0_ref, m0_ref,
                  hm_ref, c_ref, n_ref, m_ref):
    L = q_ref.shape[1]

    @pl.when(pl.program_id(1) == 0)
    def _():
        c_ref[...] = c0_ref[...]
        n_ref[...] = n0_ref[...]
        m_ref[...] = m0_ref[...]

    gc = misc_ref[0]
    gr = gt_ref[0]
    row = lax.broadcasted_iota(jnp.int32, (L, L), 0)
    col = lax.broadcasted_iota(jnp.int32, (L, L), 1)
    causal = row >= col
    b_col = _dot_exact(causal.astype(F32), gc)
    b_row = _dot_exact(gr, (row <= col).astype(F32))
    for h in range(M_HEADS):
        sl = slice(h * M_DIM, (h + 1) * M_DIM)
        q, k, v = q_ref[0, :, sl], k_ref[0, :, sl], v_ref[0, :, sl]
        c_prev, n_prev, m_prev = c_ref[0, h], n_ref[0, h:h + 1, :], m_ref[0, h:h + 1, 0:1]
        bc = b_col[:, M_HEADS + h:M_HEADS + h + 1]
        ic = gc[:, h:h + 1]
        br = b_row[M_HEADS + h:M_HEADS + h + 1, :]
        ir = gr[h:h + 1, :]
        d = jnp.where(causal, bc - br + ir, NEG)
        m_inter = bc + m_prev
        m_t = jnp.maximum(m_inter, jnp.max(d, axis=-1, keepdims=True))
        w_inter = jnp.exp(m_inter - m_t)
        p = _dot_nt(q, k) * jnp.exp(d - m_t)
        num = w_inter * _dot(q, c_prev.astype(BF16)) + _dot(p.astype(BF16), v)
        den = (w_inter * jnp.sum(q.astype(F32) * n_prev, axis=-1, keepdims=True)
               + jnp.sum(p, axis=-1, keepdims=True))
        hh = num / jnp.maximum(jnp.abs(den), jnp.exp(-m_t))
        m_new = m_t[L - 1:L, :]
        w_new = jnp.exp(bc[L - 1:L, :] - bc + ic - m_new)
        decay = jnp.exp(m_inter[L - 1:L, :] - m_new)
        kf, vf = k.astype(F32), v.astype(F32)
        c_ref[0, h] = decay * c_prev + lax.dot_general(k, (w_new * vf).astype(BF16), _TN,
                                                       preferred_element_type=F32)
        n_ref[0, h:h + 1, :] = decay * n_prev + jnp.sum(w_new * kf, axis=0, keepdims=True)
        m_ref[0, h:h + 1, :] = jnp.broadcast_to(m_new, (1, M_DIM))
        hn = hh * lax.rsqrt(jnp.mean(hh * hh, axis=-1, keepdims=True) + EPS)
        hm_ref[0, :, sl] = ((hn * mg_ref[:, sl]) * og_ref[0, :, sl]).astype(BF16)


def _mlstm(qkv, og, misc, gt, mg, c0, n0, m0, L):
    B, T, _ = qkv.shape
    seq = lambda j: pl.BlockSpec((1, L, M_WIDTH), lambda b, c: (b, c, j))
    st4 = pl.BlockSpec((1, M_HEADS, M_DIM, M_DIM), lambda b, c: (b, 0, 0, 0))
    st3 = pl.BlockSpec((1, M_HEADS, M_DIM), lambda b, c: (b, 0, 0))
    return pl.pallas_call(
        _mlstm_kernel,
        grid=(B, T // L),
        in_specs=[seq(0), seq(1), seq(2), seq(0),
                  pl.BlockSpec((1, L, LANES), lambda b, c: (b, c, 0)),
                  pl.BlockSpec((1, 2 * M_HEADS, L), lambda b, c: (b, 0, c)),
                  pl.BlockSpec((1, M_WIDTH), lambda b, c: (0, 0)),
                  st4, st3, st3],
        out_specs=(seq(0), st4, st3, st3),
        out_shape=(jax.ShapeDtypeStruct((B, T, M_WIDTH), BF16),
                   jax.ShapeDtypeStruct(c0.shape, F32),
                   jax.ShapeDtypeStruct(n0.shape, F32),
                   jax.ShapeDtypeStruct(m0.shape, F32)),
        compiler_params=_params("parallel", "arbitrary"),
        name="mlstm",
    )(qkv, qkv, qkv, og, misc, gt, mg, c0, n0, m0)


def _cmp_accumulate(load, w_ref, s):
    acc = None
    for r in range(0, D_CMP, 2):
        lhs = jnp.concatenate([load(r), load(r + 1)], axis=1).astype(BF16)
        term = _dot(lhs, w_ref[s, r // 2])
        acc = term if acc is None else acc + term
    return acc


def _cmp_prompt_kernel(k_ref, v_ref, w_ref, fs_ref):
    n = k_ref.shape[0] // D_CMP
    for s, ref in enumerate((k_ref, v_ref)):
        fs_ref[s] = _cmp_accumulate(lambda r: ref[pl.ds(r, n, stride=D_CMP), :], w_ref, s)


def _cmp_prompt(rows, w_cmp, tr):
    M = rows.shape[0]
    return pl.pallas_call(
        _cmp_prompt_kernel,
        grid=(M // tr,),
        in_specs=[pl.BlockSpec((tr, KV_WIDTH), lambda i: (i, 0)), pl.BlockSpec((tr, KV_WIDTH), lambda i: (i, 1)),
                  pl.BlockSpec(w_cmp.shape, lambda i: (0, 0, 0, 0))],
        out_specs=pl.BlockSpec((2, tr // D_CMP, 2 * LANES), lambda i: (0, i, 0)),
        out_shape=jax.ShapeDtypeStruct((2, M // D_CMP, 2 * LANES), F32),
        compiler_params=_params("parallel"),
        name="cmp_prompt",
    )(rows, rows, w_cmp)


def _cmp_sample_kernel(n_group, pt_ref, *refs):
    page_refs, w_ref, fs_ref, rows_s = refs[:n_group], refs[n_group], refs[n_group + 1], refs[n_group + 2:]
    page = page_refs[0].shape[2]
    n = n_group * page // D_CMP
    for s in range(2):
        for i, p in enumerate(page_refs):
            rows_s[s][i * page:(i + 1) * page, :] = p[s].T
    for s in range(2):
        fs_ref[s, 0] = _cmp_accumulate(lambda r: rows_s[s][pl.ds(r, n, stride=D_CMP), :], w_ref, s)


def _cmp_sample(cache_t, page_table, layer, w_cmp, n_group):
    B, n_pages = page_table.shape
    page = cache_t.shape[4]
    per_page = page // D_CMP

    def page_spec(i):
        return pl.BlockSpec((None, None, 2, KV_WIDTH, page),
                            lambda b, p, pt: (pt[b, p * n_group + i], layer, 0, 0, 0))

    grid_spec = pltpu.PrefetchScalarGridSpec(
        num_scalar_prefetch=1,
        grid=(B, n_pages // n_group),
        in_specs=[page_spec(i) for i in range(n_group)]
        + [pl.BlockSpec(w_cmp.shape, lambda b, p, pt: (0, 0, 0, 0))],
        out_specs=pl.BlockSpec((2, 1, n_group * per_page, 2 * LANES), lambda b, p, pt: (0, b, p, 0)),
        scratch_shapes=[pltpu.VMEM((n_group * page, KV_WIDTH), F32), pltpu.VMEM((n_group * page, KV_WIDTH), F32)],
    )
    return pl.pallas_call(
        functools.partial(_cmp_sample_kernel, n_group),
        grid_spec=grid_spec,
        out_shape=jax.ShapeDtypeStruct((2, B, n_pages * per_page, 2 * LANES), F32),
        compiler_params=_params("parallel", "arbitrary"),
        name="cmp_sample",
    )(page_table, *([cache_t] * n_group), w_cmp)


def _compressed_kv(fs_ref, bias_ref, tabs):
    n = fs_ref.shape[2]
    out = []
    for s in range(2):
        fs = fs_ref[s, 0]
        out.append(fs[:, 0:LANES] + pltpu.roll(fs[:, LANES:2 * LANES], n - 1, 0) + bias_ref[s:s + 1, :])
    return _rope128(out[0], *tabs), out[1]


def _masked_softmax(s, valid, axis):
    m = jnp.max(jnp.where(valid, s, NEG), axis=axis, keepdims=True)
    e = jnp.where(valid, jnp.exp(s - m), 0.0)
    l = jnp.sum(e, axis=axis, keepdims=True)
    return e * (1.0 / jnp.where(l > 0.0, l, 1.0))


def _bias_softmax(s, axis, keep=None):
    m = jnp.max(s, axis=axis, keepdims=True)
    e = jnp.exp2(s - m)
    scale = 1.0 / jnp.sum(e, axis=axis, keepdims=True)
    return e * (scale if keep is None else scale * keep)


def _dot_split3(a, x):
    h1 = x.astype(BF16)
    r1 = x - h1.astype(F32)
    h2 = r1.astype(BF16)
    h3 = (r1 - h2.astype(F32)).astype(BF16)
    return _dot(a, h1) + _dot(a, h2) + _dot(a, h3)


def _top_blocks(score, n_top, axis):
    pos = lax.broadcasted_iota(jnp.int32, score.shape, axis).astype(F32)
    picks = []
    for _ in range(n_top):
        mx = jnp.max(score, axis=axis, keepdims=True)
        idx = jnp.min(jnp.where(score == mx, pos, float(score.shape[axis])), axis=axis, keepdims=True)
        picks.append(idx)
        score = jnp.where(pos == idx, NEG, score)
    return picks, score


def _nsa_prompt_kernel(tk, q_ref, gt_ref, fs_ref, cb_ref, kc_c, kc_s1, kc_s2, ovt_ref,
                       ks_ref, kw_ref, vst_ref, vwt_ref, oa_ref,
                       kc_s, vct_s, sel_s, m_s, l_s, acc_s, sc_s, part_s):
    n_cmp = fs_ref.shape[2]
    n_sel = ovt_ref.shape[0]
    nq = Q_BLOCK
    cols = A_REP * nq
    per_tile = tk // L_SEL
    qi = pl.program_id(1)
    q0 = qi * nq

    @pl.when(qi == 0)
    def _():
        kc, vc = _compressed_kv(fs_ref, cb_ref, (kc_c[...], kc_s1[...], kc_s2[...]))
        kc_s[...] = kc.astype(BF16)
        vct_s[...] = vc.T.astype(BF16)

    t_pos = q0 + lax.broadcasted_iota(jnp.int32, (1, nq), 1)
    gates = gt_ref[0]

    def heads(a):
        return jnp.concatenate([a] * A_REP, axis=1)

    def all_heads(a):
        return jnp.concatenate([a] * A_HEADS, axis=1)

    cmp_end = lax.broadcasted_iota(jnp.int32, (n_cmp, 1), 0) * D_CMP + (L_CMP - 1)
    c_bias = jnp.where(cmp_end <= t_pos, 0.0, NEG)
    c_keep = jnp.where(t_pos >= L_CMP - 1, 1.0, 0.0)
    blk = lax.broadcasted_iota(jnp.int32, (n_sel, 1), 0)
    cur = t_pos // L_SEL
    forced = (blk == 0) | (blk == cur) | (blk == cur - 1)
    started = blk * L_SEL <= t_pos
    w_start = jnp.maximum(q0 - WINDOW, 0)
    w_len = WINDOW + nq
    rel = t_pos - (w_start + lax.broadcasted_iota(jnp.int32, (w_len, 1), 0))
    w_bias = jnp.where((rel >= 0) & (rel < WINDOW), 0.0, NEG)
    n_tiles = (q0 + nq + tk - 1) // tk
    key_iota = lax.broadcasted_iota(jnp.int32, (tk, 1), 0)

    groups = range(A_KV)
    vs = [slice(g * A_DIM, (g + 1) * A_DIM) for g in groups]
    gs = [slice(g * cols, (g + 1) * cols) for g in groups]
    q_all = q_ref[...].reshape(A_HEADS * nq, LANES)

    p_c = _bias_softmax(_dot_nt(kc_s[...], q_all) + all_heads(c_bias), 0, all_heads(c_keep))
    p_c16 = p_c.astype(BF16)
    o_c = [_dot(vct_s[vs[g], :], p_c16[:, gs[g]]) for g in groups]

    for g in groups:
        p_sum = p_c[:, g * cols:g * cols + nq]
        for r in range(1, A_REP):
            p_sum = p_sum + p_c[:, g * cols + r * nq:g * cols + (r + 1) * nq]
        imp = _dot_split3(ovt_ref[...], p_sum)
        score = jnp.where(forced, FORCE_SCORE, jnp.where(started, imp, -1.0))
        _, left = _top_blocks(score, min(N_TOP, n_sel), 0)
        sel_s[g] = jnp.where(left < -2.0, 0.0, NEG)

    ws = pl.multiple_of(w_start, nq)
    p_w = _bias_softmax(_dot_nt(kw_ref[0, pl.ds(ws, w_len), :], q_all) + all_heads(w_bias), 0).astype(BF16)
    wc = w_start // LANES
    vw_t = jnp.concatenate([vwt_ref[0, wc + c] for c in range(w_len // LANES)], axis=1)
    for g in groups:
        o_w = _dot(vw_t[vs[g], :], p_w[:, gs[g]])
        for r in range(A_REP):
            hd = g * A_REP + r
            base = 2 * M_HEADS + hd * N_BRANCH
            cs = slice(r * nq, (r + 1) * nq)
            part_s[hd * A_DIM:(hd + 1) * A_DIM, :] = (gates[base:base + 1] * o_c[g][:, cs]
                                                      + gates[base + 2:base + 3] * o_w[:, cs])

    m_s[...] = jnp.full(m_s.shape, NEG, F32)
    l_s[...] = jnp.zeros(l_s.shape, F32)
    acc_s[...] = jnp.zeros(acc_s.shape, F32)

    last_tile = ks_ref.shape[1] // tk - 1

    def scores(kt):
        k0 = pl.multiple_of(jnp.minimum(kt, last_tile) * tk, tk)
        return _dot_nt(ks_ref[0, pl.ds(k0, tk), :], q_all)

    def attend(kt, slot):
        kc = jnp.minimum(kt, last_tile)
        causal = kt * tk + key_iota <= t_pos
        bias = []
        for g in groups:
            chosen = jnp.concatenate(
                [jnp.broadcast_to(sel_s[g, pl.ds(kc * per_tile + i, 1), :], (L_SEL, nq)) for i in range(per_tile)],
                axis=0)
            bias.append(heads(jnp.where(causal, chosen, NEG)))
        s = sc_s[slot] + jnp.concatenate(bias, axis=1)
        m_old = m_s[...]
        m_new = jnp.maximum(m_old, jnp.max(s, axis=0, keepdims=True))
        e = jnp.exp2(s - m_new)
        a = jnp.exp2(m_old - m_new)
        l_s[...] = a * l_s[...] + jnp.sum(e, axis=0, keepdims=True)
        e16 = e.astype(BF16)
        for g in groups:
            acc_s[g] = a[:, gs[g]] * acc_s[g] + _dot(vst_ref[0, kc, vs[g], :], e16[:, gs[g]])
        m_s[...] = m_new

    sc_s[0] = scores(0)

    def pair(j, carry):
        sc_s[1] = scores(2 * j + 1)
        attend(2 * j, 0)
        sc_s[0] = scores(2 * j + 2)
        attend(2 * j + 1, 1)
        return carry

    lax.fori_loop(0, (n_tiles + 1) // 2, pair, 0)

    l = l_s[...]
    inv_l = 1.0 / jnp.where(l > 0.0, l, 1.0)
    heads_out = []
    for g in groups:
        o_s = acc_s[g] * inv_l[:, gs[g]]
        for r in range(A_REP):
            hd = g * A_REP + r
            gate = gates[2 * M_HEADS + hd * N_BRANCH + 1:2 * M_HEADS + hd * N_BRANCH + 2]
            heads_out.append(part_s[hd * A_DIM:(hd + 1) * A_DIM, :] + gate * o_s[:, r * nq:(r + 1) * nq])
    oa_ref[0] = jnp.concatenate(heads_out, axis=0).T.astype(BF16)


def _nsa_prompt(q8, gt, fs, cbias, kc_tabs, ovt, kk, vst, vwt, B, T, tk):
    nqb = T // Q_BLOCK
    n_cmp = fs.shape[2]
    n_sel = ovt.shape[0]
    cols = A_REP * Q_BLOCK
    const = lambda a: pl.BlockSpec(a.shape, lambda b, i: (0,) * a.ndim)
    keys = lambda j: pl.BlockSpec((1, T, LANES), lambda b, i: (b, 0, j))
    per_b = lambda a: pl.BlockSpec((1,) + a.shape[1:], lambda b, i: (b, 0, 0, 0))
    return pl.pallas_call(
        functools.partial(_nsa_prompt_kernel, tk),
        grid=(B, nqb),
        in_specs=[pl.BlockSpec((A_HEADS, Q_BLOCK, LANES), lambda b, i: (0, b * nqb + i, 0)),
                  pl.BlockSpec((1, gt.shape[1], Q_BLOCK), lambda b, i: (b, 0, i)),
                  pl.BlockSpec((2, 1, n_cmp, 2 * LANES), lambda b, i: (0, b, 0, 0)),
                  const(cbias), const(kc_tabs[0]), const(kc_tabs[1]), const(kc_tabs[2]),
                  const(ovt), keys(0), keys(1), per_b(vst), per_b(vwt)],
        out_specs=pl.BlockSpec((1, Q_BLOCK, A_WIDTH), lambda b, i: (b, i, 0)),
        out_shape=jax.ShapeDtypeStruct((B, T, A_WIDTH), BF16),
        scratch_shapes=[pltpu.VMEM((n_cmp, LANES), BF16), pltpu.VMEM((LANES, n_cmp), BF16),
                        pltpu.VMEM((A_KV, n_sel, Q_BLOCK), F32),
                        pltpu.VMEM((1, A_KV * cols), F32), pltpu.VMEM((1, A_KV * cols), F32),
                        pltpu.VMEM((A_KV, A_DIM, cols), F32),
                        pltpu.VMEM((2, tk, A_KV * cols), F32),
                        pltpu.VMEM((A_WIDTH, Q_BLOCK), F32)],
        compiler_params=_params("parallel", "arbitrary"),
        name="nsa_prompt",
    )(q8, gt, fs, cbias, *kc_tabs, ovt, kk, kk, vst, vwt)


def _nsa_sample_select_kernel(q_pos, n_sel, q_ref, fs_ref, cb_ref, kc_c, kc_s1, kc_s2, ov_ref, oc_ref, idx_ref):
    n_cmp = fs_ref.shape[2]
    kc, vc = _compressed_kv(fs_ref, cb_ref, (kc_c[...], kc_s1[...], kc_s2[...]))
    q = q_ref[0]
    cmp_end = lax.broadcasted_iota(jnp.int32, (A_HEADS, n_cmp), 1) * D_CMP + (L_CMP - 1)
    p_c = _masked_softmax(_dot_nt(q, kc.astype(BF16)), cmp_end <= q_pos, 1)
    oc_ref[0] = _dot(p_c.astype(BF16), vc.astype(BF16))
    head = lax.broadcasted_iota(jnp.int32, (A_HEADS, n_cmp), 0)
    p_sum = jnp.zeros((A_HEADS, n_cmp), F32)
    for g in range(A_KV):
        pg = jnp.sum(jnp.where(head // A_REP == g, p_c, 0.0), axis=0, keepdims=True)
        p_sum = jnp.where(head == g, pg, p_sum)
    imp = _dot_exact(p_sum, ov_ref[...])
    n_pad = ov_ref.shape[1]
    blk = lax.broadcasted_iota(jnp.int32, (A_HEADS, n_pad), 1)
    cur = q_pos // L_SEL
    forced = (blk == 0) | (blk == cur) | (blk == cur - 1)
    score = jnp.where(forced, FORCE_SCORE, jnp.where(blk * L_SEL <= q_pos, imp, -1.0))
    score = jnp.where(blk < n_sel, score, 2 * NEG)
    picks, _ = _top_blocks(score, min(N_TOP, n_sel), 1)
    lane = lax.broadcasted_iota(jnp.int32, (A_HEADS, LANES), 1)
    out = jnp.zeros((A_HEADS, LANES), jnp.int32)
    for j, idx in enumerate(picks):
        out = jnp.where(lane == j, idx.astype(jnp.int32), out)
    idx_ref[0] = out


def _nsa_sample_select(q8, fs, cbias, kc_tabs, ov, q_pos, n_sel):
    B = q8.shape[0]
    n_cmp = fs.shape[2]
    const = lambda a: pl.BlockSpec(a.shape, lambda b: (0,) * a.ndim)
    per_b = pl.BlockSpec((1, A_HEADS, LANES), lambda b: (b, 0, 0))
    return pl.pallas_call(
        functools.partial(_nsa_sample_select_kernel, q_pos, n_sel),
        grid=(B,),
        in_specs=[per_b, pl.BlockSpec((2, 1, n_cmp, 2 * LANES), lambda b: (0, b, 0, 0)),
                  const(cbias), const(kc_tabs[0]), const(kc_tabs[1]), const(kc_tabs[2]), const(ov)],
        out_specs=(per_b, per_b),
        out_shape=(jax.ShapeDtypeStruct((B, A_HEADS, LANES), F32),
                   jax.ShapeDtypeStruct((B, A_HEADS, LANES), jnp.int32)),
        compiler_params=_params("parallel"),
        name="nsa_sample_select",
    )(q8, fs, cbias, *kc_tabs, ov)


def _r16(a):
    return a.astype(BF16).astype(F32)


def _nsa_sample_attend_kernel(q_pos, n_top, pt_ref, ix_ref, q_ref, new_ref, newc_ref, oc_ref, misc_ref, win_ref,
                              *refs):
    page_refs, (oa_ref, nwin_ref) = refs[:A_KV * n_top], refs[A_KV * n_top:]
    b = pl.program_id(0)
    q = q_ref[0]
    qf = q.astype(F32)
    head = lax.broadcasted_iota(jnp.int32, (A_HEADS, 1), 0)
    new = new_ref[0]
    page = page_refs[0].shape[2]
    per_page = page // L_SEL
    lane = lax.broadcasted_iota(jnp.int32, (A_HEADS, page), 1)
    s_tok = jnp.sum(qf * _r16(new[0:1, :]), axis=-1, keepdims=True)

    o_s = jnp.zeros((A_HEADS, LANES), F32)
    for g in range(A_KV):
        parts, n_fresh = [], 0
        for j in range(n_top):
            blk = ix_ref[b, g, j]
            fresh = blk * L_SEL >= q_pos
            limit = jnp.where(fresh, -1, q_pos)
            s = _dot(q, page_refs[g * n_top + j][0].astype(BF16))
            s_pos = (blk // per_page) * page + lane
            ok = jnp.where(lane // L_SEL == blk % per_page, s_pos, limit + 1) <= limit
            parts.append(jnp.where(ok, s, NEG))
            n_fresh = n_fresh + jnp.where(fresh, 1, 0)
        s_all = jnp.concatenate(parts, axis=1)
        has_tok = jnp.where(head >= 0, n_fresh, 0) > 0
        m = jnp.maximum(jnp.max(s_all, axis=-1, keepdims=True), jnp.where(has_tok, s_tok, NEG))
        e = jnp.exp(s_all - m)
        e_tok = jnp.where(has_tok, jnp.exp(s_tok - m), 0.0)
        den = jnp.sum(e, axis=-1, keepdims=True) + e_tok
        e16 = e.astype(BF16)
        acc = _r16(e_tok) * _r16(new[1:2, :])
        for j in range(n_top):
            acc = acc + _dot_nt(e16[:, j * page:(j + 1) * page], page_refs[g * n_top + j][1].astype(BF16))
        o_s = jnp.where(head // A_REP == g, acc / den, o_s)

    wb = win_ref.shape[2]
    kwt, vwt = win_ref[0], win_ref[1]
    s_old = _dot(q, kwt.astype(BF16))
    s_new = jnp.sum(qf * _r16(new[2:3, :]), axis=-1, keepdims=True)
    rel = wb - lax.broadcasted_iota(jnp.int32, (A_HEADS, wb), 1)
    ok = (rel >= 0) & (rel < WINDOW) & (q_pos - rel >= 0)
    m = jnp.maximum(jnp.max(jnp.where(ok, s_old, NEG), axis=-1, keepdims=True), s_new)
    e_old = jnp.where(ok, jnp.exp(s_old - m), 0.0)
    e_new = jnp.exp(s_new - m)
    den = jnp.sum(e_old, axis=-1, keepdims=True) + e_new
    o_w = _dot_nt((e_old / den).astype(BF16), vwt.astype(BF16)) + _r16(e_new / den) * _r16(new[3:4, :])
    gates = misc_ref[0]
    oa_ref[0] = gates[:, 0:1] * oc_ref[0] + gates[:, 1:2] * o_s + gates[:, 2:3] * o_w
    newc = newc_ref[0]
    last = lax.broadcasted_iota(jnp.int32, (LANES, wb), 1) == wb - 1
    nwin_ref[0] = jnp.where(last, newc[:, 2:3], pltpu.roll(kwt, wb - 1, 1))
    nwin_ref[1] = jnp.where(last, newc[:, 3:4], pltpu.roll(vwt, wb - 1, 1))


def _nsa_sample_attend(page_table, idx, q8, new_rows, new_cols, o_c, gate_rows, win_t, cache_t, layer, q_pos):
    B, n_pages = page_table.shape
    n_top = idx.shape[2]
    page = cache_t.shape[4]
    per_page = page // L_SEL
    wb = win_t.shape[4]

    def page_spec(g, j):
        def index(b, pt, ix):
            pg = jnp.clip(ix[b, g, j] // per_page, 0, n_pages - 1)
            return (pt[b, pg], layer, 1, 0, 0)
        return pl.BlockSpec((None, None, 2, KV_WIDTH, page), index)

    per_b = pl.BlockSpec((1, A_HEADS, LANES), lambda b, pt, ix: (b, 0, 0))
    pages = [page_spec(g, j) for g in range(A_KV) for j in range(n_top)]
    grid_spec = pltpu.PrefetchScalarGridSpec(
        num_scalar_prefetch=2,
        grid=(B,),
        in_specs=[per_b, per_b, pl.BlockSpec((1, KV_WIDTH, A_HEADS), lambda b, pt, ix: (b, 0, 0)), per_b, per_b,
                  pl.BlockSpec((None, None, 2, KV_WIDTH, wb), lambda b, pt, ix: (b, layer, 0, 0, 0))] + pages,
        out_specs=(per_b, pl.BlockSpec((None, 2, KV_WIDTH, wb), lambda b, pt, ix: (b, 0, 0, 0))),
    )
    return pl.pallas_call(
        functools.partial(_nsa_sample_attend_kernel, q_pos, n_top),
        grid_spec=grid_spec,
        out_shape=(jax.ShapeDtypeStruct((B, A_HEADS, LANES), F32),
                   jax.ShapeDtypeStruct((B, 2, KV_WIDTH, wb), F32)),
        compiler_params=_params("parallel"),
        name="nsa_sample_attend",
    )(page_table, idx, q8, new_rows, new_cols, o_c, gate_rows, win_t, *([cache_t] * len(pages)))


def _ffn_kernel(final, tf, x_ref, hm_ref, oa_ref, wo_ref, nf_ref, wg_ref, wu_ref, wd_ref, nfin_ref, out_ref, act_s):
    half = hm_ref.shape[1]
    x1 = x_ref[...] + _dot(hm_ref[...], wo_ref[0:half, :]) + _dot(oa_ref[...], wo_ref[half:, :])
    ms = jnp.mean(x1 * x1, axis=-1, keepdims=True)
    xn = ((x1 * lax.rsqrt(ms + EPS)) * nf_ref[...]).astype(BF16)
    for c in range(wg_ref.shape[1] // tf):
        sl = slice(c * tf, (c + 1) * tf)
        gate = _dot(xn, wg_ref[:, sl])
        act_s[:, sl] = ((gate * _sigmoid(gate)) * _dot(xn, wu_ref[:, sl])).astype(BF16)
    y = x1 + _dot(act_s[...], wd_ref[...])
    if final:
        ms = jnp.mean(y * y, axis=-1, keepdims=True)
        y = (y * lax.rsqrt(ms + EPS)) * nfin_ref[...]
    out_ref[...] = y


def _ffn(x, hm, oa, wo, nf, wg, wu, wd, nfin, final, tm, tf):
    M, D = x.shape
    F = wg.shape[1]
    row = lambda w: pl.BlockSpec((tm, w), lambda i: (i, 0))
    const = lambda a: pl.BlockSpec(a.shape, lambda i: (0,) * a.ndim, pipeline_mode=pl.Buffered(1))
    return pl.pallas_call(
        functools.partial(_ffn_kernel, final, tf),
        grid=(M // tm,),
        in_specs=[row(D), row(hm.shape[1]), row(oa.shape[1]), const(wo), const(nf),
                  const(wg), const(wu), const(wd), const(nfin)],
        out_specs=row(D),
        out_shape=jax.ShapeDtypeStruct((M, D), F32),
        scratch_shapes=[pltpu.VMEM((tm, F), BF16)],
        compiler_params=_params("parallel"),
        name="ffn",
    )(x, hm, oa, wo, nf, wg, wu, wd, nfin)


def _permute_w_in(w):
    a = 4 * M_WIDTH
    b = a + 2 * M_HEADS
    c = b + A_WIDTH + 6 * KV_WIDTH
    pad = jnp.zeros((w.shape[0], LANES - N_MISC), w.dtype)
    return jnp.concatenate([w[:, :a], w[:, b:c], w[:, a:b], w[:, c:], pad], axis=1).astype(BF16)


def _cmp_weights(wk, wv):
    def one(w):
        z = jnp.zeros((D_CMP, A_DIM, A_DIM), w.dtype)
        halves = []
        for part in (w[:D_CMP], w[D_CMP:]):
            top = jnp.concatenate([part, z], axis=2)
            bot = jnp.concatenate([z, part], axis=2)
            halves.append(jnp.concatenate([top, bot], axis=1))
        return jnp.concatenate(halves, axis=2)
    return jnp.stack([one(wk), one(wv)]).astype(BF16).reshape(2, D_CMP // 2, 2 * KV_WIDTH, 2 * LANES)


def _overlap(n_cmp_pad, n_sel_pad, n_cmp, n_sel):
    lo_c = np.arange(n_cmp_pad)[:, None] * D_CMP
    lo_s = np.arange(n_sel_pad)[None, :] * L_SEL
    ov = np.clip(np.minimum(lo_c + L_CMP, lo_s + L_SEL) - np.maximum(lo_c, lo_s), 0, None).astype(np.float32) / L_CMP
    ov[n_cmp:, :] = 0.0
    ov[:, n_sel:] = 0.0
    return jnp.asarray(ov)


def _pick(n, prefs):
    for p in prefs:
        if n % p == 0:
            return p
    return n


def kernel(x_prompt, x_sample, cache_nsa_kv, page_table, state_win_kv, state_mlstm_c, state_mlstm_n, state_mlstm_m, norm_mix, w_in, b_if, m_norm, w_cmp_k, b_cmp_k, w_cmp_v, b_cmp_v, w_out, norm_ffn, w_gate, w_up, w_down, norm_final):
    B, T, D = x_prompt.shape
    Bs, Ts, _ = x_sample.shape
    depth = w_in.shape[0]
    n_pool, _, n_slots, page = cache_nsa_kv.shape[:4]
    n_pages = page_table.shape[1]
    past = n_pages * page
    wb = state_win_kv.shape[3]
    assert Ts == 1 and T % Q_BLOCK == 0 and T >= WINDOW + Q_BLOCK and wb == WINDOW
    assert past % page == 0 and page % L_SEL == 0 and page % D_CMP == 0
    assert ((past + Ts) // D_CMP) * D_CMP <= past

    Mp = B * T
    tm_p = _pick(Mp, (512, 256, 128))
    tm_f = _pick(Mp, (512, 256, 128))
    ff = w_gate.shape[2]
    tf = _pick(ff, (256, 128))
    l_chunk = _pick(T, (256, 128, 64))
    ls_pad = 16
    tk = _pick(T, (512, 256, 128))
    assert tk % tm_p == 0 and tm_p % LANES == 0
    n_group = _pick(n_pages, (16, 8, 4, 2, 1))
    tr = _pick(T, (2048, 1024, 512, 256))

    n_chunk_p = T // D_CMP
    n_cmp_p, n_sel_p = n_chunk_p - 1, -(-T // L_SEL)
    tabs_p = _rope_tables(jnp.arange(T))
    kc_tabs_p = _rope_tables(jnp.arange(n_chunk_p) * D_CMP)
    assert L_CMP <= 256
    ovt_p = _overlap(n_chunk_p, n_sel_p, n_cmp_p, n_sel_p).T.astype(BF16)
    n_chunk_s = (past + Ts) // D_CMP
    n_cmp_s, n_sel_s = n_chunk_s - 1, -(-(past + Ts) // L_SEL)
    n_sel_pad = -(-n_sel_s // LANES) * LANES
    tabs_s = _rope_tables(jnp.full((Bs,), past))
    kc_tabs_s = _rope_tables(jnp.arange(n_chunk_s) * D_CMP)
    ov_s = _overlap(n_chunk_s, n_sel_pad, n_cmp_s, n_sel_s)
    cache_t = cache_nsa_kv.transpose(0, 1, 2, 4, 5, 3).reshape(n_pool, depth, n_slots, KV_WIDTH, page)
    win_t = state_win_kv.transpose(0, 1, 2, 4, 5, 3).reshape(Bs, depth, 2, KV_WIDTH, wb)

    xp = x_prompt.reshape(Mp, D)
    xs = x_sample.reshape(Bs, D)
    rows_p, rows_s, win_p, win_s, c_p, c_s, n_p, n_s, m_p, m_s = ([] for _ in range(10))
    y_p = y_s = None
    nfin = norm_final.reshape(1, D)
    for l in range(depth):
        w_perm = _permute_w_in(w_in[l])
        gain = norm_mix[l].reshape(1, D)
        bias = jnp.zeros((1, LANES), F32).at[0, :2 * M_HEADS].set(b_if[l].reshape(-1))
        mg = m_norm[l].reshape(1, M_WIDTH)
        w_cmp = _cmp_weights(w_cmp_k[l], w_cmp_v[l])
        cbias = jnp.stack([jnp.tile(b_cmp_k[l], A_KV), jnp.tile(b_cmp_v[l], A_KV)])
        wo, nf = w_out[l].astype(BF16), norm_ffn[l].reshape(1, D)
        wg, wu, wd = w_gate[l].astype(BF16), w_up[l].astype(BF16), w_down[l].astype(BF16)
        final = l == depth - 1

        qkv, og, q8, rows, win, misc, kk, vst, vwt = _inproj(xp, gain, w_perm, bias, tabs_p, tm_p, tk)
        misc3 = misc.reshape(B, T, LANES)
        gt = misc3[:, :, :N_MISC].transpose(0, 2, 1)
        hm, c_new, n_new, m_new = _mlstm(
            qkv.reshape(B, T, -1), og.reshape(B, T, -1), misc3, gt, mg,
            jnp.zeros((B, M_HEADS, M_DIM, M_DIM), F32), jnp.zeros((B, M_HEADS, M_DIM), F32),
            jnp.full((B, M_HEADS, M_DIM), NEG, F32), l_chunk)
        fs = _cmp_prompt(rows, w_cmp, tr).reshape(2, B, n_chunk_p, 2 * LANES)
        oa = _nsa_prompt(q8, gt, fs, cbias, kc_tabs_p, ovt_p, kk.reshape(B, T, -1),
                         vst.reshape(B, T // tk, KV_WIDTH, tk), vwt.reshape(B, T // LANES, KV_WIDTH, LANES), B, T, tk)
        xp_new = _ffn(xp, hm.reshape(Mp, -1), oa.reshape(Mp, -1), wo, nf, wg, wu, wd, nfin, final, tm_f, tf)
        if final:
            y_p = xp_new
        xp = xp_new
        rows_p.append(rows.reshape(B, T, n_slots, A_KV, A_DIM))
        win_p.append(win.reshape(B, T, 2, A_KV, A_DIM)[:, T - wb:].transpose(0, 2, 1, 3, 4))
        c_p.append(c_new); n_p.append(n_new); m_p.append(m_new[:, :, 0])

        qkv, og, q8, rows, win, misc = _inproj(xs, gain, w_perm, bias, tabs_s, Bs, 0)
        pad_t = lambda a: jnp.pad(a[:, None, :], ((0, 0), (0, ls_pad - 1), (0, 0)))
        inert = jnp.zeros((ls_pad, LANES), F32).at[1:, :M_HEADS].set(NEG)
        misc_pad = pad_t(misc) + inert[None]
        gt = misc_pad[:, :, :2 * M_HEADS].transpose(0, 2, 1)
        m0 = jnp.broadcast_to(state_mlstm_m[:, l, :, None].astype(F32), (Bs, M_HEADS, M_DIM))
        hm, c_new, n_new, m_new = _mlstm(
            pad_t(qkv), pad_t(og), misc_pad, gt, mg,
            state_mlstm_c[:, l].astype(F32), state_mlstm_n[:, l].astype(F32), m0, ls_pad)
        hm = hm[:, 0]
        fs = _cmp_sample(cache_t, page_table, l, w_cmp, n_group)
        q8s = q8.transpose(1, 0, 2)
        o_c, idx = _nsa_sample_select(q8s, fs, cbias, kc_tabs_s, ov_s, past, n_sel_s)
        idx = idx[:, :A_KV, :min(N_TOP, n_sel_s)]
        new_rows = jnp.pad(jnp.stack([rows[:, 2 * LANES:3 * LANES], rows[:, 3 * LANES:], win[:, :LANES],
                                      win[:, LANES:]], axis=1), ((0, 0), (0, A_HEADS - 4), (0, 0)))
        gate_rows = jnp.pad(misc[:, 2 * M_HEADS:N_MISC].reshape(Bs, A_HEADS, N_BRANCH),
                            ((0, 0), (0, 0), (0, LANES - N_BRANCH)))
        oa8, nwin = _nsa_sample_attend(page_table, idx, q8s, new_rows, new_rows.transpose(0, 2, 1), o_c, gate_rows,
                                       win_t, cache_t, l, past)
        grp = (jnp.arange(A_HEADS) // A_REP)[None, :, None, None]
        oa = jnp.take_along_axis(oa8.reshape(Bs, A_HEADS, A_KV, A_DIM), jnp.broadcast_to(grp, (Bs, A_HEADS, 1, A_DIM)),
                                 axis=2).reshape(Bs, A_WIDTH).astype(BF16)
        xs_new = _ffn(xs, hm, oa, wo, nf, wg, wu, wd, nfin, final, Bs, tf)
        if final:
            y_s = xs_new
        xs = xs_new
        rows_s.append(rows.reshape(Bs, Ts, n_slots, A_KV, A_DIM))
        win_s.append(nwin.reshape(Bs, 2, A_KV, A_DIM, wb).transpose(0, 1, 4, 2, 3))
        c_s.append(c_new); n_s.append(n_new); m_s.append(m_new[:, :, 0])

    return (y_p.reshape(B, T, D), y_s.reshape(Bs, Ts, D),
            jnp.stack(rows_p, axis=2), jnp.stack(rows_s, axis=2),
            jnp.stack(win_p, axis=1), jnp.stack(win_s, axis=1),
            jnp.stack(c_p, axis=1), jnp.stack(c_s, axis=1),
            jnp.stack(n_p, axis=1), jnp.stack(n_s, axis=1),
            jnp.stack(m_p, axis=1), jnp.stack(m_s, axis=1))
```

```python
import functools

import jax
import jax.numpy as jnp
import numpy as np
from jax import lax
from jax.experimental import pallas as pl
from jax.experimental.pallas import tpu as pltpu

F32 = jnp.float32
BF16 = jnp.bfloat16

M_HEADS = 4
M_DIM = 128
M_WIDTH = M_HEADS * M_DIM
A_HEADS = 8
A_DIM = 64
A_KV = 2
A_REP = A_HEADS // A_KV
A_WIDTH = A_HEADS * A_DIM
KV_WIDTH = A_KV * A_DIM
L_CMP = 32
D_CMP = 16
L_SEL = 64
N_TOP = 16
WINDOW = 512
Q_BLOCK = 128
N_BRANCH = 3
ROPE_THETA = 500000.0
ROPE_DIM = A_DIM // 4
EPS = 1e-6
NEG = -1e30
LOG2_E = 1.4426950408889634
FORCE_SCORE = 1e4
LANES = 128
N_MISC = 2 * M_HEADS + N_BRANCH * A_HEADS
D_PERM = 4 * M_WIDTH + A_WIDTH + 6 * KV_WIDTH + LANES
VMEM_LIMIT = 56 * 1024 * 1024

_NT = (((1,), (1,)), ((), ()))
_TN = (((0,), (0,)), ((), ()))


def _params(*sem):
    return pltpu.CompilerParams(dimension_semantics=sem, vmem_limit_bytes=VMEM_LIMIT)


def _sigmoid(x):
    return 1.0 / (1.0 + jnp.exp(-x))


def _dot(a, b):
    return jnp.dot(a, b, preferred_element_type=F32)


def _dot_exact(a, b):
    return jnp.dot(a, b, preferred_element_type=F32, precision=lax.Precision.HIGHEST)


def _dot_nt(a, b):
    return lax.dot_general(a, b, _NT, preferred_element_type=F32)


def _rope128(v, c, s1, s2):
    half = ROPE_DIM // 2
    return v * c + pltpu.roll(v, LANES - half, 1) * s1 + pltpu.roll(v, half, 1) * s2


def _rope_tables(pos):
    half = ROPE_DIM // 2
    inv = ROPE_THETA ** (-jnp.arange(half, dtype=F32) / half)
    ang = pos.astype(F32)[:, None] * inv[None, :]
    cos, sin = jnp.cos(ang), jnp.sin(ang)
    n = pos.shape[0]
    one = jnp.ones((n, A_DIM - ROPE_DIM), F32)
    zero = jnp.zeros((n, A_DIM - ROPE_DIM), F32)
    zh = jnp.zeros((n, half), F32)
    c = jnp.concatenate([cos, cos, one], axis=1)
    s1 = jnp.concatenate([-sin, zh, zero], axis=1)
    s2 = jnp.concatenate([zh, sin, zero], axis=1)
    tile = lambda a: jnp.concatenate([a, a], axis=1)
    return tile(c), tile(s1), tile(s2)


def _inproj_kernel(q_scale, n_in, *refs):
    x_ref, g_ref, w_ref, bias_ref, rc_ref, rs1_ref, rs2_ref = refs[:7]
    om_ref, og_ref, oq_ref, orows_ref, owin_ref, omisc_ref = refs[n_in:n_in + 6]
    attn_refs = refs[n_in + 6:]
    x = x_ref[...]
    ms = jnp.mean(x * x, axis=-1, keepdims=True)
    xn = ((x * lax.rsqrt(ms + EPS)) * g_ref[...]).astype(BF16)
    c, s1, s2 = rc_ref[...], rs1_ref[...], rs2_ref[...]
    lane = lax.broadcasted_iota(jnp.int32, (x.shape[0], LANES), 1)

    def proj(a, b):
        return _dot(xn, w_ref[:, a:b])

    om_ref[:, 0:M_WIDTH] = (proj(0, M_WIDTH) * (M_DIM ** -0.5)).astype(BF16)
    om_ref[:, M_WIDTH:3 * M_WIDTH] = proj(M_WIDTH, 3 * M_WIDTH).astype(BF16)
    og_ref[...] = _sigmoid(proj(3 * M_WIDTH, 4 * M_WIDTH))

    off = 4 * M_WIDTH
    pq = proj(off, off + A_WIDTH)
    for j in range(A_WIDTH // LANES):
        blk = _rope128(pq[:, j * LANES:(j + 1) * LANES], c, s1, s2) * q_scale
        swapped = pltpu.roll(blk, A_DIM, 1)
        for e in range(2):
            hd = 2 * j + e
            grp = hd // A_REP
            src = blk if e == grp else swapped
            keep = (lane // A_DIM) == grp
            oq_ref[hd] = jnp.where(keep, src, 0.0).astype(BF16)

    off += A_WIDTH
    pk = proj(off, off + 6 * KV_WIDTH)
    kcr, vcr, ksl, vsl, kw, vw = [pk[:, i * LANES:(i + 1) * LANES] for i in range(6)]
    ksl = _rope128(ksl, c, s1, s2)
    kw = _rope128(kw, c, s1, s2)
    owin_ref[:, 0:LANES] = kw
    owin_ref[:, LANES:2 * LANES] = vw
    if attn_refs:
        okk_ref, ovst_ref, ovwt_ref, okvt_ref = attn_refs
        orows_ref[:, 0:LANES] = kcr
        orows_ref[:, LANES:2 * LANES] = vcr
        vsl_t = vsl.T
        for i, a_t in enumerate((kcr.T, vcr.T, ksl.T, vsl_t)):
            okvt_ref[i * LANES:(i + 1) * LANES, :] = a_t
        okk_ref[:, 0:LANES] = ksl.astype(BF16)
        okk_ref[:, LANES:2 * LANES] = kw.astype(BF16)
        ovst_ref[0] = vsl_t.astype(BF16)
        for i in range(ovwt_ref.shape[0]):
            ovwt_ref[i] = vw[i * LANES:(i + 1) * LANES].T.astype(BF16)
    else:
        for i, a in enumerate((kcr, vcr, ksl, vsl)):
            orows_ref[:, i * LANES:(i + 1) * LANES] = a

    off += 6 * KV_WIDTH
    pm = proj(off, off + LANES) + bias_ref[...]
    log_sig = -(jnp.maximum(-pm, 0.0) + jnp.log(1.0 + jnp.exp(-jnp.abs(pm))))
    omisc_ref[...] = jnp.where(lane < M_HEADS, pm, jnp.where(lane < 2 * M_HEADS, log_sig, _sigmoid(pm)))


def _inproj(x, gain, w_perm, bias, tabs, tm, attn=None):
    M, D = x.shape
    nt = tabs[0].shape[0] // tm
    row = lambda i: (i, 0)
    tab = pl.BlockSpec((tm, LANES), lambda i: (i % nt, 0))
    full = lambda a: pl.BlockSpec(a.shape, lambda i: (0,) * a.ndim)
    out_shape = (
        jax.ShapeDtypeStruct((M, 3 * M_WIDTH), BF16),
        jax.ShapeDtypeStruct((M, M_WIDTH), F32),
        jax.ShapeDtypeStruct((A_HEADS, M, LANES), BF16),
        jax.ShapeDtypeStruct((M, (2 if attn else 4) * KV_WIDTH), F32),
        jax.ShapeDtypeStruct((M, 2 * KV_WIDTH), F32),
        jax.ShapeDtypeStruct((M, LANES), F32),
    )
    out_specs = (
        pl.BlockSpec((tm, 3 * M_WIDTH), row),
        pl.BlockSpec((tm, M_WIDTH), row),
        pl.BlockSpec((A_HEADS, tm, LANES), lambda i: (0, i, 0)),
        pl.BlockSpec((tm, (2 if attn else 4) * KV_WIDTH), row),
        pl.BlockSpec((tm, 2 * KV_WIDTH), row),
        pl.BlockSpec((tm, LANES), row),
    )
    in_specs = [pl.BlockSpec((tm, D), row), full(gain), full(w_perm), full(bias), tab, tab, tab]
    operands = [x, gain, w_perm, bias, *tabs]
    aliases = {}
    if attn:
        tk, depth, layer, kvt = attn
        per = tk // tm
        seq = tabs[0].shape[0]
        out_shape += (jax.ShapeDtypeStruct((M, 2 * KV_WIDTH), BF16),
                      jax.ShapeDtypeStruct((M // tk, KV_WIDTH, tk), BF16),
                      jax.ShapeDtypeStruct((M // LANES, KV_WIDTH, LANES), BF16),
                      jax.ShapeDtypeStruct((M // seq, depth, 4 * KV_WIDTH, seq), F32))
        out_specs += (pl.BlockSpec((tm, 2 * KV_WIDTH), row),
                      pl.BlockSpec((1, KV_WIDTH, tm), lambda i: (i // per, 0, i % per)),
                      pl.BlockSpec((tm // LANES, KV_WIDTH, LANES), lambda i: (i, 0, 0)),
                      pl.BlockSpec((None, None, 4 * KV_WIDTH, tm), lambda i: (i // nt, layer, 0, i % nt)))
        if kvt is not None:
            in_specs.append(pl.BlockSpec(memory_space=pl.ANY))
            operands.append(kvt)
            aliases = {len(operands) - 1: len(out_shape) - 1}
    return pl.pallas_call(
        functools.partial(_inproj_kernel, A_DIM ** -0.5 * (LOG2_E if attn else 1.0), len(operands)),
        grid=(M // tm,),
        in_specs=in_specs,
        out_specs=out_specs,
        out_shape=out_shape,
        input_output_aliases=aliases,
        compiler_params=_params("parallel"),
        name="inproj",
    )(*operands)


def _mlstm_kernel(q_ref, k_ref, v_ref, og_ref, misc_ref, gt_ref, mg_ref, c0_ref, n0_ref, m0_ref,
                  hm_ref, c_ref, n_ref, m_ref):
    L = q_ref.shape[1]

    @pl.when(pl.program_id(1) == 0)
    def _():
        c_ref[...] = c0_ref[...]
        n_ref[...] = n0_ref[...]
        m_ref[...] = m0_ref[...]

    gc = misc_ref[0]
    gr = gt_ref[0]
    row = lax.broadcasted_iota(jnp.int32, (L, L), 0)
    col = lax.broadcasted_iota(jnp.int32, (L, L), 1)
    causal = row >= col
    b_col = _dot_exact(causal.astype(F32), gc)
    b_row = _dot_exact(gr, (row <= col).astype(F32))
    for h in range(M_HEADS):
        sl = slice(h * M_DIM, (h + 1) * M_DIM)
        q, k, v = q_ref[0, :, sl], k_ref[0, :, sl], v_ref[0, :, sl]
        c_prev, n_prev, m_prev = c_ref[0, h], n_ref[0, h:h + 1, :], m_ref[0, h:h + 1, 0:1]
        bc = b_col[:, M_HEADS + h:M_HEADS + h + 1]
        ic = gc[:, h:h + 1]
        br = b_row[M_HEADS + h:M_HEADS + h + 1, :]
        ir = gr[h:h + 1, :]
        d = jnp.where(causal, bc - br + ir, NEG)
        m_inter = bc + m_prev
        m_t = jnp.maximum(m_inter, jnp.max(d, axis=-1, keepdims=True))
        w_inter = jnp.exp(m_inter - m_t)
        p = _dot_nt(q, k) * jnp.exp(d - m_t)
        num = w_inter * _dot(q, c_prev.astype(BF16)) + _dot(p.astype(BF16), v)
        den = (w_inter * jnp.sum(q.astype(F32) * n_prev, axis=-1, keepdims=True)
               + jnp.sum(p, axis=-1, keepdims=True))
        hh = num / jnp.maximum(jnp.abs(den), jnp.exp(-m_t))
        m_new = m_t[L - 1:L, :]
        w_new = jnp.exp(bc[L - 1:L, :] - bc + ic - m_new)
        decay = jnp.exp(m_inter[L - 1:L, :] - m_new)
        kf, vf = k.astype(F32), v.astype(F32)
        c_ref[0, h] = decay * c_prev + lax.dot_general(k, (w_new * vf).astype(BF16), _TN,
                                                       preferred_element_type=F32)
        n_ref[0, h:h + 1, :] = decay * n_prev + jnp.sum(w_new * kf, axis=0, keepdims=True)
        m_ref[0, h:h + 1, :] = jnp.broadcast_to(m_new, (1, M_DIM))
        hn = hh * lax.rsqrt(jnp.mean(hh * hh, axis=-1, keepdims=True) + EPS)
        hm_ref[0, :, sl] = ((hn * mg_ref[:, sl]) * og_ref[0, :, sl]).astype(BF16)


def _mlstm(qkv, og, misc, gt, mg, c0, n0, m0, L):
    B, T, _ = qkv.shape
    seq = lambda j: pl.BlockSpec((1, L, M_WIDTH), lambda b, c: (b, c, j))
    st4 = pl.BlockSpec((1, M_HEADS, M_DIM, M_DIM), lambda b, c: (b, 0, 0, 0))
    st3 = pl.BlockSpec((1, M_HEADS, M_DIM), lambda b, c: (b, 0, 0))
    return pl.pallas_call(
        _mlstm_kernel,
        grid=(B, T // L),
        in_specs=[seq(0), seq(1), seq(2), seq(0),
                  pl.BlockSpec((1, L, LANES), lambda b, c: (b, c, 0)),
                  pl.BlockSpec((1, 2 * M_HEADS, L), lambda b, c: (b, 0, c)),
                  pl.BlockSpec((1, M_WIDTH), lambda b, c: (0, 0)),
                  st4, st3, st3],
        out_specs=(seq(0), st4, st3, st3),
        out_shape=(jax.ShapeDtypeStruct((B, T, M_WIDTH), BF16),
                   jax.ShapeDtypeStruct(c0.shape, F32),
                   jax.ShapeDtypeStruct(n0.shape, F32),
                   jax.ShapeDtypeStruct(m0.shape, F32)),
        compiler_params=_params("parallel", "arbitrary"),
        name="mlstm",
    )(qkv, qkv, qkv, og, misc, gt, mg, c0, n0, m0)


def _cmp_accumulate(load, w_ref, s):
    acc = None
    for r in range(0, D_CMP, 2):
        lhs = jnp.concatenate([load(r), load(r + 1)], axis=1).astype(BF16)
        term = _dot(lhs, w_ref[s, r // 2])
        acc = term if acc is None else acc + term
    return acc


def _cmp_prompt_kernel(k_ref, v_ref, w_ref, fs_ref):
    n = k_ref.shape[0] // D_CMP
    for s, ref in enumerate((k_ref, v_ref)):
        fs_ref[s] = _cmp_accumulate(lambda r: ref[pl.ds(r, n, stride=D_CMP), :], w_ref, s)


def _cmp_prompt(rows, w_cmp, tr):
    M = rows.shape[0]
    return pl.pallas_call(
        _cmp_prompt_kernel,
        grid=(M // tr,),
        in_specs=[pl.BlockSpec((tr, KV_WIDTH), lambda i: (i, 0)), pl.BlockSpec((tr, KV_WIDTH), lambda i: (i, 1)),
                  pl.BlockSpec(w_cmp.shape, lambda i: (0, 0, 0, 0))],
        out_specs=pl.BlockSpec((2, tr // D_CMP, 2 * LANES), lambda i: (0, i, 0)),
        out_shape=jax.ShapeDtypeStruct((2, M // D_CMP, 2 * LANES), F32),
        compiler_params=_params("parallel"),
        name="cmp_prompt",
    )(rows, rows, w_cmp)


def _cmp_sample_kernel(n_group, pt_ref, *refs):
    page_refs, w_ref, fs_ref, rows_s = refs[:n_group], refs[n_group], refs[n_group + 1], refs[n_group + 2:]
    page = page_refs[0].shape[2]
    n = n_group * page // D_CMP
    for s in range(2):
        for i, p in enumerate(page_refs):
            rows_s[s][i * page:(i + 1) * page, :] = p[s].T
    for s in range(2):
        fs_ref[s, 0] = _cmp_accumulate(lambda r: rows_s[s][pl.ds(r, n, stride=D_CMP), :], w_ref, s)


def _cmp_sample(cache_t, page_table, layer, w_cmp, n_group):
    B, n_pages = page_table.shape
    page = cache_t.shape[4]
    per_page = page // D_CMP

    def page_spec(i):
        return pl.BlockSpec((None, None, 2, KV_WIDTH, page),
                            lambda b, p, pt: (pt[b, p * n_group + i], layer, 0, 0, 0))

    grid_spec = pltpu.PrefetchScalarGridSpec(
        num_scalar_prefetch=1,
        grid=(B, n_pages // n_group),
        in_specs=[page_spec(i) for i in range(n_group)]
        + [pl.BlockSpec(w_cmp.shape, lambda b, p, pt: (0, 0, 0, 0))],
        out_specs=pl.BlockSpec((2, 1, n_group * per_page, 2 * LANES), lambda b, p, pt: (0, b, p, 0)),
        scratch_shapes=[pltpu.VMEM((n_group * page, KV_WIDTH), F32), pltpu.VMEM((n_group * page, KV_WIDTH), F32)],
    )
    return pl.pallas_call(
        functools.partial(_cmp_sample_kernel, n_group),
        grid_spec=grid_spec,
        out_shape=jax.ShapeDtypeStruct((2, B, n_pages * per_page, 2 * LANES), F32),
        compiler_params=_params("parallel", "arbitrary"),
        name="cmp_sample",
    )(page_table, *([cache_t] * n_group), w_cmp)


def _compressed_kv(fs_ref, bias_ref, tabs, b=0):
    n = fs_ref.shape[2]
    out = []
    for s in range(2):
        fs = fs_ref[s, b]
        out.append(fs[:, 0:LANES] + pltpu.roll(fs[:, LANES:2 * LANES], n - 1, 0) + bias_ref[s:s + 1, :])
    return _rope128(out[0], *tabs), out[1]


def _masked_softmax(s, valid, axis):
    m = jnp.max(jnp.where(valid, s, NEG), axis=axis, keepdims=True)
    e = jnp.where(valid, jnp.exp(s - m), 0.0)
    l = jnp.sum(e, axis=axis, keepdims=True)
    return e * (1.0 / jnp.where(l > 0.0, l, 1.0))


def _bias_softmax(s, axis, keep=None):
    m = jnp.max(s, axis=axis, keepdims=True)
    e = jnp.exp2(s - m)
    scale = 1.0 / jnp.sum(e, axis=axis, keepdims=True)
    return e * (scale if keep is None else scale * keep)


def _split3(x):
    h1 = x.astype(BF16)
    r1 = x - h1.astype(F32)
    h2 = r1.astype(BF16)
    return h1, h2, (r1 - h2.astype(F32)).astype(BF16)


def _dot_split3(a, x):
    h1, h2, h3 = _split3(x)
    return _dot(a, h1) + _dot(a, h2) + _dot(a, h3)


def _dot_split3_rhs(x, a):
    h1, h2, h3 = _split3(x)
    return _dot(h1, a) + _dot(h2, a) + _dot(h3, a)


def _top_blocks(score, n_top, axis):
    pos = lax.broadcasted_iota(jnp.int32, score.shape, axis).astype(F32)
    picks = []
    for _ in range(n_top):
        mx = jnp.max(score, axis=axis, keepdims=True)
        idx = jnp.min(jnp.where(score == mx, pos, float(score.shape[axis])), axis=axis, keepdims=True)
        picks.append(idx)
        score = jnp.where(pos == idx, NEG, score)
    return picks, score


def _nsa_prompt_kernel(tk, q_ref, gt_ref, fs_ref, cb_ref, kc_c, kc_s1, kc_s2, ovt_ref,
                       ks_ref, kw_ref, vst_ref, vwt_ref, oa_ref,
                       kc_s, vct_s, sel_s, m_s, acc_s, sc_s, part_s):
    n_cmp = fs_ref.shape[2]
    n_sel = ovt_ref.shape[0]
    nq = Q_BLOCK
    cols = A_REP * nq
    per_tile = tk // L_SEL
    qi = pl.program_id(1)
    q0 = qi * nq

    @pl.when(qi == 0)
    def _():
        kc, vc = _compressed_kv(fs_ref, cb_ref, (kc_c[...], kc_s1[...], kc_s2[...]))
        kc_s[...] = kc.astype(BF16)
        vct_s[...] = vc.T.astype(BF16)

    t_pos = q0 + lax.broadcasted_iota(jnp.int32, (1, nq), 1)
    gates = gt_ref[0]

    def heads(a):
        return jnp.concatenate([a] * A_REP, axis=1)

    def all_heads(a):
        return jnp.concatenate([a] * A_HEADS, axis=1)

    cmp_end = lax.broadcasted_iota(jnp.int32, (n_cmp, 1), 0) * D_CMP + (L_CMP - 1)
    c_bias = jnp.where(cmp_end <= t_pos, 0.0, NEG)
    c_keep = jnp.where(t_pos >= L_CMP - 1, 1.0, 0.0)
    blk = lax.broadcasted_iota(jnp.int32, (n_sel, 1), 0)
    cur = t_pos // L_SEL
    forced = (blk == 0) | (blk == cur) | (blk == cur - 1)
    started = blk * L_SEL <= t_pos
    w_start = jnp.maximum(q0 - WINDOW, 0)
    w_len = WINDOW + nq
    rel = t_pos - (w_start + lax.broadcasted_iota(jnp.int32, (w_len, 1), 0))
    w_bias = jnp.where((rel >= 0) & (rel < WINDOW), 0.0, NEG)
    n_tiles = (q0 + nq + tk - 1) // tk
    key_iota = lax.broadcasted_iota(jnp.int32, (tk, 1), 0)

    groups = range(A_KV)
    vs = [slice(g * A_DIM, (g + 1) * A_DIM) for g in groups]
    gs = [slice(g * cols, (g + 1) * cols) for g in groups]
    q_all = q_ref[...].reshape(A_HEADS * nq, LANES)

    p_c = _bias_softmax(_dot_nt(kc_s[...], q_all) + all_heads(c_bias), 0, all_heads(c_keep))
    p_c16 = p_c.astype(BF16)
    o_c = [_dot(vct_s[vs[g], :], p_c16[:, gs[g]]) for g in groups]

    for g in groups:
        p_sum = p_c[:, g * cols:g * cols + nq]
        for r in range(1, A_REP):
            p_sum = p_sum + p_c[:, g * cols + r * nq:g * cols + (r + 1) * nq]
        imp = _dot_split3(ovt_ref[...], p_sum)
        score = jnp.where(forced, FORCE_SCORE, jnp.where(started, imp, -1.0))
        _, left = _top_blocks(score, min(N_TOP, n_sel), 0)
        sel_s[g] = jnp.where(left < -2.0, 0.0, NEG)

    ws = pl.multiple_of(w_start, nq)
    p_w = _bias_softmax(_dot_nt(kw_ref[0, pl.ds(ws, w_len), :], q_all) + all_heads(w_bias), 0).astype(BF16)
    wc = w_start // LANES
    vw_t = jnp.concatenate([vwt_ref[0, wc + c] for c in range(w_len // LANES)], axis=1)
    for g in groups:
        o_w = _dot(vw_t[vs[g], :], p_w[:, gs[g]])
        for r in range(A_REP):
            hd = g * A_REP + r
            base = 2 * M_HEADS + hd * N_BRANCH
            cs = slice(r * nq, (r + 1) * nq)
            part_s[hd * A_DIM:(hd + 1) * A_DIM, :] = (gates[base:base + 1] * o_c[g][:, cs]
                                                      + gates[base + 2:base + 3] * o_w[:, cs])

    m_s[...] = jnp.full(m_s.shape, NEG, F32)
    acc_s[...] = jnp.zeros(acc_s.shape, F32)
    ones_rows = jnp.ones((acc_s.shape[1] - A_DIM, tk), BF16)

    last_tile = ks_ref.shape[1] // tk - 1

    def scores(kt):
        k0 = pl.multiple_of(jnp.minimum(kt, last_tile) * tk, tk)
        return _dot_nt(ks_ref[0, pl.ds(k0, tk), :], q_all)

    def attend(kt, slot):
        kc = jnp.minimum(kt, last_tile)
        causal = kt * tk + key_iota <= t_pos
        bias = []
        for g in groups:
            chosen = jnp.concatenate(
                [jnp.broadcast_to(sel_s[g, pl.ds(kc * per_tile + i, 1), :], (L_SEL, nq)) for i in range(per_tile)],
                axis=0)
            bias.append(heads(jnp.where(causal, chosen, NEG)))
        s = sc_s[slot] + jnp.concatenate(bias, axis=1)
        m_old = m_s[...]
        m_new = jnp.maximum(m_old, jnp.max(s, axis=0, keepdims=True))
        e = jnp.exp2(s - m_new)
        a = jnp.exp2(m_old - m_new)
        e16 = e.astype(BF16)
        for g in groups:
            vals = jnp.concatenate([vst_ref[0, kc, vs[g], :], ones_rows], axis=0)
            acc_s[g] = a[:, gs[g]] * acc_s[g] + _dot(vals, e16[:, gs[g]])
        m_s[...] = m_new

    sc_s[0] = scores(0)

    def pair(j, carry):
        sc_s[1] = scores(2 * j + 1)
        attend(2 * j, 0)
        sc_s[0] = scores(2 * j + 2)
        attend(2 * j + 1, 1)
        return carry

    lax.fori_loop(0, (n_tiles + 1) // 2, pair, 0)

    heads_out = []
    for g in groups:
        l = acc_s[g, A_DIM:A_DIM + 1, :]
        o_s = acc_s[g, 0:A_DIM, :] * (1.0 / jnp.where(l > 0.0, l, 1.0))
        for r in range(A_REP):
            hd = g * A_REP + r
            gate = gates[2 * M_HEADS + hd * N_BRANCH + 1:2 * M_HEADS + hd * N_BRANCH + 2]
            heads_out.append(part_s[hd * A_DIM:(hd + 1) * A_DIM, :] + gate * o_s[:, r * nq:(r + 1) * nq])
    oa_ref[0] = jnp.concatenate(heads_out, axis=0).T.astype(BF16)


def _nsa_prompt(q8, gt, fs, cbias, kc_tabs, ovt, kk, vst, vwt, B, T, tk):
    nqb = T // Q_BLOCK
    n_cmp = fs.shape[2]
    n_sel = ovt.shape[0]
    cols = A_REP * Q_BLOCK
    const = lambda a: pl.BlockSpec(a.shape, lambda b, i: (0,) * a.ndim)
    keys = lambda j: pl.BlockSpec((1, T, LANES), lambda b, i: (b, 0, j))
    per_b = lambda a: pl.BlockSpec((1,) + a.shape[1:], lambda b, i: (b, 0, 0, 0))
    return pl.pallas_call(
        functools.partial(_nsa_prompt_kernel, tk),
        grid=(B, nqb),
        in_specs=[pl.BlockSpec((A_HEADS, Q_BLOCK, LANES), lambda b, i: (0, b * nqb + i, 0)),
                  pl.BlockSpec((1, gt.shape[1], Q_BLOCK), lambda b, i: (b, 0, i)),
                  pl.BlockSpec((2, 1, n_cmp, 2 * LANES), lambda b, i: (0, b, 0, 0)),
                  const(cbias), const(kc_tabs[0]), const(kc_tabs[1]), const(kc_tabs[2]),
                  const(ovt), keys(0), keys(1), per_b(vst), per_b(vwt)],
        out_specs=pl.BlockSpec((1, Q_BLOCK, A_WIDTH), lambda b, i: (b, i, 0)),
        out_shape=jax.ShapeDtypeStruct((B, T, A_WIDTH), BF16),
        scratch_shapes=[pltpu.VMEM((n_cmp, LANES), BF16), pltpu.VMEM((LANES, n_cmp), BF16),
                        pltpu.VMEM((A_KV, n_sel, Q_BLOCK), F32),
                        pltpu.VMEM((1, A_KV * cols), F32),
                        pltpu.VMEM((A_KV, A_DIM + 16, cols), F32),
                        pltpu.VMEM((2, tk, A_KV * cols), F32),
                        pltpu.VMEM((A_WIDTH, Q_BLOCK), F32)],
        compiler_params=_params("parallel", "arbitrary"),
        name="nsa_prompt",
    )(q8, gt, fs, cbias, *kc_tabs, ovt, kk, kk, vst, vwt)


def _nsa_sample_select_kernel(q_pos, n_sel, q_ref, fs_ref, cb_ref, kc_c, kc_s1, kc_s2, ov_ref, oc_ref, idx_ref):
    n_seq, n_cmp = fs_ref.shape[1], fs_ref.shape[2]
    tabs = (kc_c[...], kc_s1[...], kc_s2[...])
    cmp_end = lax.broadcasted_iota(jnp.int32, (A_HEADS, n_cmp), 1) * D_CMP + (L_CMP - 1)
    head = lax.broadcasted_iota(jnp.int32, (A_HEADS, n_cmp), 0)
    imps = []
    for b in range(n_seq):
        kc, vc = _compressed_kv(fs_ref, cb_ref, tabs, b)
        p_c = _masked_softmax(_dot_nt(q_ref[b], kc.astype(BF16)), cmp_end <= q_pos, 1)
        oc_ref[b] = _dot(p_c.astype(BF16), vc.astype(BF16))
        p_sum = jnp.zeros((A_HEADS, n_cmp), F32)
        for g in range(A_KV):
            pg = jnp.sum(jnp.where(head // A_REP == g, p_c, 0.0), axis=0, keepdims=True)
            p_sum = jnp.where(head == g, pg, p_sum)
        imps.append(_dot_split3_rhs(p_sum, ov_ref[...]))
    imp = jnp.concatenate(imps, axis=0)
    n_pad = ov_ref.shape[1]
    blk = lax.broadcasted_iota(jnp.int32, (A_HEADS * n_seq, n_pad), 1)
    cur = q_pos // L_SEL
    forced = (blk == 0) | (blk == cur) | (blk == cur - 1)
    score = jnp.where(forced, FORCE_SCORE, jnp.where(blk * L_SEL <= q_pos, imp, -1.0))
    score = jnp.where(blk < n_sel, score, 2 * NEG)
    picks, _ = _top_blocks(score, min(N_TOP, n_sel), 1)
    lane = lax.broadcasted_iota(jnp.int32, (A_HEADS * n_seq, LANES), 1)
    out = jnp.zeros((A_HEADS * n_seq, LANES), jnp.int32)
    for j, idx in enumerate(picks):
        out = jnp.where(lane == j, idx.astype(jnp.int32), out)
    for b in range(n_seq):
        idx_ref[b] = out[b * A_HEADS:(b + 1) * A_HEADS]


def _nsa_sample_select(q8, fs, cbias, kc_tabs, ov, q_pos, n_sel, n_seq):
    B = q8.shape[0]
    n_cmp = fs.shape[2]
    const = lambda a: pl.BlockSpec(a.shape, lambda b: (0,) * a.ndim)
    per_b = pl.BlockSpec((n_seq, A_HEADS, LANES), lambda b: (b, 0, 0))
    return pl.pallas_call(
        functools.partial(_nsa_sample_select_kernel, q_pos, n_sel),
        grid=(B // n_seq,),
        in_specs=[per_b, pl.BlockSpec((2, n_seq, n_cmp, 2 * LANES), lambda b: (0, b, 0, 0)),
                  const(cbias), const(kc_tabs[0]), const(kc_tabs[1]), const(kc_tabs[2]), const(ov)],
        out_specs=(per_b, per_b),
        out_shape=(jax.ShapeDtypeStruct((B, A_HEADS, LANES), F32),
                   jax.ShapeDtypeStruct((B, A_HEADS, LANES), jnp.int32)),
        compiler_params=_params("parallel"),
        name="nsa_sample_select",
    )(q8, fs, cbias, *kc_tabs, ov)


def _r16(a):
    return a.astype(BF16).astype(F32)


def _nsa_sample_attend_kernel(q_pos, n_top, pt_ref, ix_ref, q_ref, new_ref, newc_ref, oc_ref, misc_ref, win_ref,
                              *refs):
    page_refs, (oa_ref, nwin_ref) = refs[:A_KV * n_top], refs[A_KV * n_top:]
    b = pl.program_id(0)
    q = q_ref[0]
    qf = q.astype(F32)
    head = lax.broadcasted_iota(jnp.int32, (A_HEADS, 1), 0)
    new = new_ref[0]
    page = page_refs[0].shape[2]
    per_page = page // L_SEL
    lane = lax.broadcasted_iota(jnp.int32, (A_HEADS, page), 1)
    s_tok = jnp.sum(qf * _r16(new[0:1, :]), axis=-1, keepdims=True)

    o_s = jnp.zeros((A_HEADS, LANES), F32)
    for g in range(A_KV):
        parts, n_fresh = [], 0
        for j in range(n_top):
            blk = ix_ref[b, g, j]
            fresh = blk * L_SEL >= q_pos
            limit = jnp.where(fresh, -1, q_pos)
            s = _dot(q, page_refs[g * n_top + j][0].astype(BF16))
            s_pos = (blk // per_page) * page + lane
            ok = jnp.where(lane // L_SEL == blk % per_page, s_pos, limit + 1) <= limit
            parts.append(jnp.where(ok, s, NEG))
            n_fresh = n_fresh + jnp.where(fresh, 1, 0)
        s_all = jnp.concatenate(parts, axis=1)
        has_tok = jnp.where(head >= 0, n_fresh, 0) > 0
        m = jnp.maximum(jnp.max(s_all, axis=-1, keepdims=True), jnp.where(has_tok, s_tok, NEG))
        e = jnp.exp(s_all - m)
        e_tok = jnp.where(has_tok, jnp.exp(s_tok - m), 0.0)
        den = jnp.sum(e, axis=-1, keepdims=True) + e_tok
        e16 = e.astype(BF16)
        acc = _r16(e_tok) * _r16(new[1:2, :])
        for j in range(n_top):
            acc = acc + _dot_nt(e16[:, j * page:(j + 1) * page], page_refs[g * n_top + j][1].astype(BF16))
        o_s = jnp.where(head // A_REP == g, acc / den, o_s)

    wb = win_ref.shape[2]
    kwt, vwt = win_ref[0], win_ref[1]
    s_old = _dot(q, kwt.astype(BF16))
    s_new = jnp.sum(qf * _r16(new[2:3, :]), axis=-1, keepdims=True)
    rel = wb - lax.broadcasted_iota(jnp.int32, (A_HEADS, wb), 1)
    ok = (rel >= 0) & (rel < WINDOW) & (q_pos - rel >= 0)
    m = jnp.maximum(jnp.max(jnp.where(ok, s_old, NEG), axis=-1, keepdims=True), s_new)
    e_old = jnp.where(ok, jnp.exp(s_old - m), 0.0)
    e_new = jnp.exp(s_new - m)
    den = jnp.sum(e_old, axis=-1, keepdims=True) + e_new
    o_w = _dot_nt((e_old / den).astype(BF16), vwt.astype(BF16)) + _r16(e_new / den) * _r16(new[3:4, :])
    gates = misc_ref[0]
    oa_ref[0] = gates[:, 0:1] * oc_ref[0] + gates[:, 1:2] * o_s + gates[:, 2:3] * o_w
    newc = newc_ref[0]
    last = lax.broadcasted_iota(jnp.int32, (LANES, wb), 1) == wb - 1
    nwin_ref[0] = jnp.where(last, newc[:, 2:3], pltpu.roll(kwt, wb - 1, 1))
    nwin_ref[1] = jnp.where(last, newc[:, 3:4], pltpu.roll(vwt, wb - 1, 1))


def _nsa_sample_attend(page_table, idx, q8, new_rows, new_cols, o_c, gate_rows, win_t, cache_t, layer, q_pos):
    B, n_pages = page_table.shape
    n_top = idx.shape[2]
    page = cache_t.shape[4]
    per_page = page // L_SEL
    wb = win_t.shape[4]

    def page_spec(g, j):
        def index(b, pt, ix):
            pg = jnp.clip(ix[b, g, j] // per_page, 0, n_pages - 1)
            return (pt[b, pg], layer, 1, 0, 0)
        return pl.BlockSpec((None, None, 2, KV_WIDTH, page), index)

    per_b = pl.BlockSpec((1, A_HEADS, LANES), lambda b, pt, ix: (b, 0, 0))
    pages = [page_spec(g, j) for g in range(A_KV) for j in range(n_top)]
    grid_spec = pltpu.PrefetchScalarGridSpec(
        num_scalar_prefetch=2,
        grid=(B,),
        in_specs=[per_b, per_b, pl.BlockSpec((1, KV_WIDTH, A_HEADS), lambda b, pt, ix: (b, 0, 0)), per_b, per_b,
                  pl.BlockSpec((None, None, 2, KV_WIDTH, wb), lambda b, pt, ix: (b, layer, 0, 0, 0))] + pages,
        out_specs=(per_b, pl.BlockSpec((None, 2, KV_WIDTH, wb), lambda b, pt, ix: (b, 0, 0, 0))),
    )
    return pl.pallas_call(
        functools.partial(_nsa_sample_attend_kernel, q_pos, n_top),
        grid_spec=grid_spec,
        out_shape=(jax.ShapeDtypeStruct((B, A_HEADS, LANES), F32),
                   jax.ShapeDtypeStruct((B, 2, KV_WIDTH, wb), F32)),
        compiler_params=_params("parallel"),
        name="nsa_sample_attend",
    )(page_table, idx, q8, new_rows, new_cols, o_c, gate_rows, win_t, *([cache_t] * len(pages)))


def _ffn_kernel(final, tf, x_ref, hm_ref, oa_ref, wo_ref, nf_ref, wg_ref, wu_ref, wd_ref, nfin_ref, out_ref, act_s):
    half = hm_ref.shape[1]
    x1 = x_ref[...] + _dot(hm_ref[...], wo_ref[0:half, :]) + _dot(oa_ref[...], wo_ref[half:, :])
    ms = jnp.mean(x1 * x1, axis=-1, keepdims=True)
    xn = ((x1 * lax.rsqrt(ms + EPS)) * nf_ref[...]).astype(BF16)
    for c in range(wg_ref.shape[1] // tf):
        sl = slice(c * tf, (c + 1) * tf)
        gate = _dot(xn, wg_ref[:, sl])
        act_s[:, sl] = ((gate * _sigmoid(gate)) * _dot(xn, wu_ref[:, sl])).astype(BF16)
    y = x1 + _dot(act_s[...], wd_ref[...])
    if final:
        ms = jnp.mean(y * y, axis=-1, keepdims=True)
        y = (y * lax.rsqrt(ms + EPS)) * nfin_ref[...]
    out_ref[...] = y


def _ffn(x, hm, oa, wo, nf, wg, wu, wd, nfin, final, tm, tf):
    M, D = x.shape
    F = wg.shape[1]
    row = lambda w: pl.BlockSpec((tm, w), lambda i: (i, 0))
    const = lambda a: pl.BlockSpec(a.shape, lambda i: (0,) * a.ndim, pipeline_mode=pl.Buffered(1))
    return pl.pallas_call(
        functools.partial(_ffn_kernel, final, tf),
        grid=(M // tm,),
        in_specs=[row(D), row(hm.shape[1]), row(oa.shape[1]), const(wo), const(nf),
                  const(wg), const(wu), const(wd), const(nfin)],
        out_specs=row(D),
        out_shape=jax.ShapeDtypeStruct((M, D), F32),
        scratch_shapes=[pltpu.VMEM((tm, F), BF16)],
        compiler_params=_params("parallel"),
        name="ffn",
    )(x, hm, oa, wo, nf, wg, wu, wd, nfin)


def _permute_w_in(w):
    a = 4 * M_WIDTH
    b = a + 2 * M_HEADS
    c = b + A_WIDTH + 6 * KV_WIDTH
    pad = jnp.zeros((w.shape[0], LANES - N_MISC), w.dtype)
    return jnp.concatenate([w[:, :a], w[:, b:c], w[:, a:b], w[:, c:], pad], axis=1).astype(BF16)


def _cmp_weights(wk, wv):
    def one(w):
        z = jnp.zeros((D_CMP, A_DIM, A_DIM), w.dtype)
        halves = []
        for part in (w[:D_CMP], w[D_CMP:]):
            top = jnp.concatenate([part, z], axis=2)
            bot = jnp.concatenate([z, part], axis=2)
            halves.append(jnp.concatenate([top, bot], axis=1))
        return jnp.concatenate(halves, axis=2)
    return jnp.stack([one(wk), one(wv)]).astype(BF16).reshape(2, D_CMP // 2, 2 * KV_WIDTH, 2 * LANES)


def _overlap(n_cmp_pad, n_sel_pad, n_cmp, n_sel):
    lo_c = np.arange(n_cmp_pad)[:, None] * D_CMP
    lo_s = np.arange(n_sel_pad)[None, :] * L_SEL
    ov = np.clip(np.minimum(lo_c + L_CMP, lo_s + L_SEL) - np.maximum(lo_c, lo_s), 0, None).astype(np.float32) / L_CMP
    ov[n_cmp:, :] = 0.0
    ov[:, n_sel:] = 0.0
    return jnp.asarray(ov)


def _pick(n, prefs):
    for p in prefs:
        if n % p == 0:
            return p
    return n


def kernel(x_prompt, x_sample, cache_nsa_kv, page_table, state_win_kv, state_mlstm_c, state_mlstm_n, state_mlstm_m, norm_mix, w_in, b_if, m_norm, w_cmp_k, b_cmp_k, w_cmp_v, b_cmp_v, w_out, norm_ffn, w_gate, w_up, w_down, norm_final):
    B, T, D = x_prompt.shape
    Bs, Ts, _ = x_sample.shape
    depth = w_in.shape[0]
    n_pool, _, n_slots, page = cache_nsa_kv.shape[:4]
    n_pages = page_table.shape[1]
    past = n_pages * page
    wb = state_win_kv.shape[3]
    assert Ts == 1 and T % Q_BLOCK == 0 and T >= WINDOW + Q_BLOCK and wb == WINDOW
    assert past % page == 0 and page % L_SEL == 0 and page % D_CMP == 0
    assert ((past + Ts) // D_CMP) * D_CMP <= past

    Mp = B * T
    tm_p = _pick(Mp, (512, 256, 128))
    tm_f = _pick(Mp, (512, 256, 128))
    ff = w_gate.shape[2]
    tf = _pick(ff, (256, 128))
    l_chunk = _pick(T, (256, 128, 64))
    ls_pad = 16
    tk = _pick(T, (512, 256, 128))
    assert tk % tm_p == 0 and tm_p % LANES == 0
    n_group = _pick(n_pages, (16, 8, 4, 2, 1))
    tr = _pick(T, (2048, 1024, 512, 256))

    n_chunk_p = T // D_CMP
    n_cmp_p, n_sel_p = n_chunk_p - 1, -(-T // L_SEL)
    tabs_p = _rope_tables(jnp.arange(T))
    kc_tabs_p = _rope_tables(jnp.arange(n_chunk_p) * D_CMP)
    assert L_CMP <= 256
    ovt_p = _overlap(n_chunk_p, n_sel_p, n_cmp_p, n_sel_p).T.astype(BF16)
    n_chunk_s = (past + Ts) // D_CMP
    n_cmp_s, n_sel_s = n_chunk_s - 1, -(-(past + Ts) // L_SEL)
    n_sel_pad = -(-n_sel_s // LANES) * LANES
    tabs_s = _rope_tables(jnp.full((Bs,), past))
    kc_tabs_s = _rope_tables(jnp.arange(n_chunk_s) * D_CMP)
    ov_s = _overlap(n_chunk_s, n_sel_pad, n_cmp_s, n_sel_s).astype(BF16)
    cache_t = cache_nsa_kv.transpose(0, 1, 2, 4, 5, 3).reshape(n_pool, depth, n_slots, KV_WIDTH, page)
    win_t = state_win_kv.transpose(0, 1, 2, 4, 5, 3).reshape(Bs, depth, 2, KV_WIDTH, wb)

    xp = x_prompt.reshape(Mp, D)
    xs = x_sample.reshape(Bs, D)
    rows_s, win_p, win_s, c_p, c_s, n_p, n_s, m_p, m_s = ([] for _ in range(9))
    y_p = y_s = kvt = None
    nfin = norm_final.reshape(1, D)
    for l in range(depth):
        w_perm = _permute_w_in(w_in[l])
        gain = norm_mix[l].reshape(1, D)
        bias = jnp.zeros((1, LANES), F32).at[0, :2 * M_HEADS].set(b_if[l].reshape(-1))
        mg = m_norm[l].reshape(1, M_WIDTH)
        w_cmp = _cmp_weights(w_cmp_k[l], w_cmp_v[l])
        cbias = jnp.stack([jnp.tile(b_cmp_k[l], A_KV), jnp.tile(b_cmp_v[l], A_KV)])
        wo, nf = w_out[l].astype(BF16), norm_ffn[l].reshape(1, D)
        wg, wu, wd = w_gate[l].astype(BF16), w_up[l].astype(BF16), w_down[l].astype(BF16)
        final = l == depth - 1

        qkv, og, q8, rows, win, misc, kk, vst, vwt, kvt = _inproj(xp, gain, w_perm, bias, tabs_p, tm_p,
                                                                    (tk, depth, l, kvt))
        misc3 = misc.reshape(B, T, LANES)
        gt = misc3[:, :, :N_MISC].transpose(0, 2, 1)
        hm, c_new, n_new, m_new = _mlstm(
            qkv.reshape(B, T, -1), og.reshape(B, T, -1), misc3, gt, mg,
            jnp.zeros((B, M_HEADS, M_DIM, M_DIM), F32), jnp.zeros((B, M_HEADS, M_DIM), F32),
            jnp.full((B, M_HEADS, M_DIM), NEG, F32), l_chunk)
        fs = _cmp_prompt(rows, w_cmp, tr).reshape(2, B, n_chunk_p, 2 * LANES)
        oa = _nsa_prompt(q8, gt, fs, cbias, kc_tabs_p, ovt_p, kk.reshape(B, T, -1),
                         vst.reshape(B, T // tk, KV_WIDTH, tk), vwt.reshape(B, T // LANES, KV_WIDTH, LANES), B, T, tk)
        xp_new = _ffn(xp, hm.reshape(Mp, -1), oa.reshape(Mp, -1), wo, nf, wg, wu, wd, nfin, final, tm_f, tf)
        if final:
            y_p = xp_new
        xp = xp_new
        win_p.append(win.reshape(B, T, 2, A_KV, A_DIM)[:, T - wb:].transpose(0, 2, 1, 3, 4))
        c_p.append(c_new); n_p.append(n_new); m_p.append(m_new[:, :, 0])

        qkv, og, q8, rows, win, misc = _inproj(xs, gain, w_perm, bias, tabs_s, Bs)
        pad_t = lambda a: jnp.pad(a[:, None, :], ((0, 0), (0, ls_pad - 1), (0, 0)))
        inert = jnp.zeros((ls_pad, LANES), F32).at[1:, :M_HEADS].set(NEG)
        misc_pad = pad_t(misc) + inert[None]
        gt = misc_pad[:, :, :2 * M_HEADS].transpose(0, 2, 1)
        m0 = jnp.broadcast_to(state_mlstm_m[:, l, :, None].astype(F32), (Bs, M_HEADS, M_DIM))
        hm, c_new, n_new, m_new = _mlstm(
            pad_t(qkv), pad_t(og), misc_pad, gt, mg,
            state_mlstm_c[:, l].astype(F32), state_mlstm_n[:, l].astype(F32), m0, ls_pad)
        hm = hm[:, 0]
        fs = _cmp_sample(cache_t, page_table, l, w_cmp, n_group)
        q8s = q8.transpose(1, 0, 2)
        o_c, idx = _nsa_sample_select(q8s, fs, cbias, kc_tabs_s, ov_s, past, n_sel_s, _pick(Bs, (4, 2, 1)))
        idx = idx[:, :A_KV, :min(N_TOP, n_sel_s)]
        new_rows = jnp.pad(jnp.stack([rows[:, 2 * LANES:3 * LANES], rows[:, 3 * LANES:], win[:, :LANES],
                                      win[:, LANES:]], axis=1), ((0, 0), (0, A_HEADS - 4), (0, 0)))
        gate_rows = jnp.pad(misc[:, 2 * M_HEADS:N_MISC].reshape(Bs, A_HEADS, N_BRANCH),
                            ((0, 0), (0, 0), (0, LANES - N_BRANCH)))
        oa8, nwin = _nsa_sample_attend(page_table, idx, q8s, new_rows, new_rows.transpose(0, 2, 1), o_c, gate_rows,
                                       win_t, cache_t, l, past)
        grp = (jnp.arange(A_HEADS) // A_REP)[None, :, None, None]
        oa = jnp.take_along_axis(oa8.reshape(Bs, A_HEADS, A_KV, A_DIM), jnp.broadcast_to(grp, (Bs, A_HEADS, 1, A_DIM)),
                                 axis=2).reshape(Bs, A_WIDTH).astype(BF16)
        xs_new = _ffn(xs, hm, oa, wo, nf, wg, wu, wd, nfin, final, Bs, tf)
        if final:
            y_s = xs_new
        xs = xs_new
        rows_s.append(rows.reshape(Bs, Ts, n_slots, A_KV, A_DIM))
        win_s.append(nwin.reshape(Bs, 2, A_KV, A_DIM, wb).transpose(0, 1, 4, 2, 3))
        c_s.append(c_new); n_s.append(n_new); m_s.append(m_new[:, :, 0])

    return (y_p.reshape(B, T, D), y_s.reshape(Bs, Ts, D),
            kvt.reshape(B, depth, n_slots, A_KV, A_DIM, T).transpose(0, 5, 1, 2, 3, 4), jnp.stack(rows_s, axis=2),
            jnp.stack(win_p, axis=1), jnp.stack(win_s, axis=1),
            jnp.stack(c_p, axis=1), jnp.stack(c_s, axis=1),
            jnp.stack(n_p, axis=1), jnp.stack(n_s, axis=1),
            jnp.stack(m_p, axis=1), jnp.stack(m_s, axis=1))
```

```python
import functools

import jax
import jax.numpy as jnp
import numpy as np
from jax import lax
from jax.experimental import pallas as pl
from jax.experimental.pallas import tpu as pltpu

F32 = jnp.float32
BF16 = jnp.bfloat16

M_HEADS = 4
M_DIM = 128
M_WIDTH = M_HEADS * M_DIM
A_HEADS = 8
A_DIM = 64
A_KV = 2
A_REP = A_HEADS // A_KV
A_WIDTH = A_HEADS * A_DIM
KV_WIDTH = A_KV * A_DIM
L_CMP = 32
D_CMP = 16
L_SEL = 64
N_TOP = 16
WINDOW = 512
Q_BLOCK = 128
N_BRANCH = 3
ROPE_THETA = 500000.0
ROPE_DIM = A_DIM // 4
EPS = 1e-6
NEG = -1e30
LOG2_E = 1.4426950408889634
FORCE_SCORE = 1e4
LANES = 128
N_MISC = 2 * M_HEADS + N_BRANCH * A_HEADS
D_PERM = 4 * M_WIDTH + A_WIDTH + 6 * KV_WIDTH + LANES
VMEM_LIMIT = 56 * 1024 * 1024

_NT = (((1,), (1,)), ((), ()))
_TN = (((0,), (0,)), ((), ()))


def _params(*sem):
    return pltpu.CompilerParams(dimension_semantics=sem, vmem_limit_bytes=VMEM_LIMIT)


def _sigmoid(x):
    return 1.0 / (1.0 + jnp.exp(-x))


def _dot(a, b):
    return jnp.dot(a, b, preferred_element_type=F32)


def _dot_exact(a, b):
    return jnp.dot(a, b, preferred_element_type=F32, precision=lax.Precision.HIGHEST)


def _dot_nt(a, b):
    return lax.dot_general(a, b, _NT, preferred_element_type=F32)


def _rope128(v, c, s1, s2):
    half = ROPE_DIM // 2
    return v * c + pltpu.roll(v, LANES - half, 1) * s1 + pltpu.roll(v, half, 1) * s2


def _rope_tables(pos):
    half = ROPE_DIM // 2
    inv = ROPE_THETA ** (-jnp.arange(half, dtype=F32) / half)
    ang = pos.astype(F32)[:, None] * inv[None, :]
    cos, sin = jnp.cos(ang), jnp.sin(ang)
    n = pos.shape[0]
    one = jnp.ones((n, A_DIM - ROPE_DIM), F32)
    zero = jnp.zeros((n, A_DIM - ROPE_DIM), F32)
    zh = jnp.zeros((n, half), F32)
    c = jnp.concatenate([cos, cos, one], axis=1)
    s1 = jnp.concatenate([-sin, zh, zero], axis=1)
    s2 = jnp.concatenate([zh, sin, zero], axis=1)
    tile = lambda a: jnp.concatenate([a, a], axis=1)
    return tile(c), tile(s1), tile(s2)


def _inproj_kernel(q_scale, n_in, *refs):
    x_ref, g_ref, w_ref, bias_ref, rc_ref, rs1_ref, rs2_ref = refs[:7]
    om_ref, og_ref, oq_ref, orows_ref, owin_ref, omisc_ref = refs[n_in:n_in + 6]
    attn_refs = refs[n_in + 6:]
    x = x_ref[...]
    ms = jnp.mean(x * x, axis=-1, keepdims=True)
    xn = ((x * lax.rsqrt(ms + EPS)) * g_ref[...]).astype(BF16)
    c, s1, s2 = rc_ref[...], rs1_ref[...], rs2_ref[...]
    lane = lax.broadcasted_iota(jnp.int32, (x.shape[0], LANES), 1)

    def proj(a, b):
        return _dot(xn, w_ref[:, a:b])

    om_ref[:, 0:M_WIDTH] = (proj(0, M_WIDTH) * (M_DIM ** -0.5)).astype(BF16)
    om_ref[:, M_WIDTH:3 * M_WIDTH] = proj(M_WIDTH, 3 * M_WIDTH).astype(BF16)
    og_ref[...] = _sigmoid(proj(3 * M_WIDTH, 4 * M_WIDTH))

    off = 4 * M_WIDTH
    pq = proj(off, off + A_WIDTH)
    for j in range(A_WIDTH // LANES):
        blk = _rope128(pq[:, j * LANES:(j + 1) * LANES], c, s1, s2) * q_scale
        swapped = pltpu.roll(blk, A_DIM, 1)
        for e in range(2):
            hd = 2 * j + e
            grp = hd // A_REP
            src = blk if e == grp else swapped
            keep = (lane // A_DIM) == grp
            oq_ref[hd] = jnp.where(keep, src, 0.0).astype(BF16)

    off += A_WIDTH
    pk = proj(off, off + 6 * KV_WIDTH)
    kcr, vcr, ksl, vsl, kw, vw = [pk[:, i * LANES:(i + 1) * LANES] for i in range(6)]
    ksl = _rope128(ksl, c, s1, s2)
    kw = _rope128(kw, c, s1, s2)
    owin_ref[:, 0:LANES] = kw
    owin_ref[:, LANES:2 * LANES] = vw
    if attn_refs:
        okk_ref, ovst_ref, ovwt_ref, okvt_ref = attn_refs
        orows_ref[:, 0:LANES] = kcr
        orows_ref[:, LANES:2 * LANES] = vcr
        vsl_t = vsl.T
        for i, a_t in enumerate((kcr.T, vcr.T, ksl.T, vsl_t)):
            okvt_ref[i * LANES:(i + 1) * LANES, :] = a_t
        okk_ref[:, 0:LANES] = ksl.astype(BF16)
        okk_ref[:, LANES:2 * LANES] = kw.astype(BF16)
        ovst_ref[0] = vsl_t.astype(BF16)
        for i in range(ovwt_ref.shape[0]):
            ovwt_ref[i] = vw[i * LANES:(i + 1) * LANES].T.astype(BF16)
    else:
        for i, a in enumerate((kcr, vcr, ksl, vsl)):
            orows_ref[:, i * LANES:(i + 1) * LANES] = a

    off += 6 * KV_WIDTH
    pm = proj(off, off + LANES) + bias_ref[...]
    log_sig = -(jnp.maximum(-pm, 0.0) + jnp.log(1.0 + jnp.exp(-jnp.abs(pm))))
    omisc_ref[...] = jnp.where(lane < M_HEADS, pm, jnp.where(lane < 2 * M_HEADS, log_sig, _sigmoid(pm)))


def _inproj(x, gain, w_perm, bias, tabs, tm, attn=None):
    M, D = x.shape
    nt = tabs[0].shape[0] // tm
    row = lambda i: (i, 0)
    tab = pl.BlockSpec((tm, LANES), lambda i: (i % nt, 0))
    full = lambda a: pl.BlockSpec(a.shape, lambda i: (0,) * a.ndim)
    out_shape = (
        jax.ShapeDtypeStruct((M, 3 * M_WIDTH), BF16),
        jax.ShapeDtypeStruct((M, M_WIDTH), F32),
        jax.ShapeDtypeStruct((A_HEADS, M, LANES), BF16),
        jax.ShapeDtypeStruct((M, (2 if attn else 4) * KV_WIDTH), F32),
        jax.ShapeDtypeStruct((M, 2 * KV_WIDTH), F32),
        jax.ShapeDtypeStruct((M, LANES), F32),
    )
    out_specs = (
        pl.BlockSpec((tm, 3 * M_WIDTH), row),
        pl.BlockSpec((tm, M_WIDTH), row),
        pl.BlockSpec((A_HEADS, tm, LANES), lambda i: (0, i, 0)),
        pl.BlockSpec((tm, (2 if attn else 4) * KV_WIDTH), row),
        pl.BlockSpec((tm, 2 * KV_WIDTH), row),
        pl.BlockSpec((tm, LANES), row),
    )
    in_specs = [pl.BlockSpec((tm, D), row), full(gain), full(w_perm), full(bias), tab, tab, tab]
    operands = [x, gain, w_perm, bias, *tabs]
    aliases = {}
    if attn:
        tk, depth, layer, kvt = attn
        per = tk // tm
        seq = tabs[0].shape[0]
        out_shape += (jax.ShapeDtypeStruct((M, 2 * KV_WIDTH), BF16),
                      jax.ShapeDtypeStruct((M // tk, KV_WIDTH, tk), BF16),
                      jax.ShapeDtypeStruct((M // LANES, KV_WIDTH, LANES), BF16),
                      jax.ShapeDtypeStruct((M // seq, depth, 4 * KV_WIDTH, seq), F32))
        out_specs += (pl.BlockSpec((tm, 2 * KV_WIDTH), row),
                      pl.BlockSpec((1, KV_WIDTH, tm), lambda i: (i // per, 0, i % per)),
                      pl.BlockSpec((tm // LANES, KV_WIDTH, LANES), lambda i: (i, 0, 0)),
                      pl.BlockSpec((None, None, 4 * KV_WIDTH, tm), lambda i: (i // nt, layer, 0, i % nt)))
        if kvt is not None:
            in_specs.append(pl.BlockSpec(memory_space=pl.ANY))
            operands.append(kvt)
            aliases = {len(operands) - 1: len(out_shape) - 1}
    return pl.pallas_call(
        functools.partial(_inproj_kernel, A_DIM ** -0.5 * (LOG2_E if attn else 1.0), len(operands)),
        grid=(M // tm,),
        in_specs=in_specs,
        out_specs=out_specs,
        out_shape=out_shape,
        input_output_aliases=aliases,
        compiler_params=_params("parallel"),
        name="inproj",
    )(*operands)


def _mlstm_kernel(q_ref, k_ref, v_ref, og_ref, misc_ref, gt_ref, mg_ref, c0_ref, n0_ref, m0_ref,
                  hm_ref, c_ref, n_ref, m_ref):
    L = q_ref.shape[1]

    @pl.when(pl.program_id(1) == 0)
    def _():
        c_ref[...] = c0_ref[...]
        n_ref[...] = n0_ref[...]
        m_ref[...] = m0_ref[...]

    gc = misc_ref[0]
    gr = gt_ref[0]
    row = lax.broadcasted_iota(jnp.int32, (L, L), 0)
    col = lax.broadcasted_iota(jnp.int32, (L, L), 1)
    causal = row >= col
    b_col = _dot_exact(causal.astype(F32), gc)
    b_row = _dot_exact(gr, (row <= col).astype(F32))
    for h in range(M_HEADS):
        sl = slice(h * M_DIM, (h + 1) * M_DIM)
        q, k, v = q_ref[0, :, sl], k_ref[0, :, sl], v_ref[0, :, sl]
        c_prev, n_prev, m_prev = c_ref[0, h], n_ref[0, h:h + 1, :], m_ref[0, h:h + 1, 0:1]
        bc = b_col[:, M_HEADS + h:M_HEADS + h + 1]
        ic = gc[:, h:h + 1]
        br = b_row[M_HEADS + h:M_HEADS + h + 1, :]
        ir = gr[h:h + 1, :]
        d = jnp.where(causal, bc - br + ir, NEG)
        m_inter = bc + m_prev
        m_t = jnp.maximum(m_inter, jnp.max(d, axis=-1, keepdims=True))
        w_inter = jnp.exp(m_inter - m_t)
        p = _dot_nt(q, k) * jnp.exp(d - m_t)
        num = w_inter * _dot(q, c_prev.astype(BF16)) + _dot(p.astype(BF16), v)
        den = (w_inter * jnp.sum(q.astype(F32) * n_prev, axis=-1, keepdims=True)
               + jnp.sum(p, axis=-1, keepdims=True))
        hh = num / jnp.maximum(jnp.abs(den), jnp.exp(-m_t))
        m_new = m_t[L - 1:L, :]
        w_new = jnp.exp(bc[L - 1:L, :] - bc + ic - m_new)
        decay = jnp.exp(m_inter[L - 1:L, :] - m_new)
        kf, vf = k.astype(F32), v.astype(F32)
        c_ref[0, h] = decay * c_prev + lax.dot_general(k, (w_new * vf).astype(BF16), _TN,
                                                       preferred_element_type=F32)
        n_ref[0, h:h + 1, :] = decay * n_prev + jnp.sum(w_new * kf, axis=0, keepdims=True)
        m_ref[0, h:h + 1, :] = jnp.broadcast_to(m_new, (1, M_DIM))
        hn = hh * lax.rsqrt(jnp.mean(hh * hh, axis=-1, keepdims=True) + EPS)
        hm_ref[0, :, sl] = ((hn * mg_ref[:, sl]) * og_ref[0, :, sl]).astype(BF16)


def _mlstm(qkv, og, misc, gt, mg, c0, n0, m0, L):
    B, T, _ = qkv.shape
    seq = lambda j: pl.BlockSpec((1, L, M_WIDTH), lambda b, c: (b, c, j))
    st4 = pl.BlockSpec((1, M_HEADS, M_DIM, M_DIM), lambda b, c: (b, 0, 0, 0))
    st3 = pl.BlockSpec((1, M_HEADS, M_DIM), lambda b, c: (b, 0, 0))
    return pl.pallas_call(
        _mlstm_kernel,
        grid=(B, T // L),
        in_specs=[seq(0), seq(1), seq(2), seq(0),
                  pl.BlockSpec((1, L, LANES), lambda b, c: (b, c, 0)),
                  pl.BlockSpec((1, 2 * M_HEADS, L), lambda b, c: (b, 0, c)),
                  pl.BlockSpec((1, M_WIDTH), lambda b, c: (0, 0)),
                  st4, st3, st3],
        out_specs=(seq(0), st4, st3, st3),
        out_shape=(jax.ShapeDtypeStruct((B, T, M_WIDTH), BF16),
                   jax.ShapeDtypeStruct(c0.shape, F32),
                   jax.ShapeDtypeStruct(n0.shape, F32),
                   jax.ShapeDtypeStruct(m0.shape, F32)),
        compiler_params=_params("parallel", "arbitrary"),
        name="mlstm",
    )(qkv, qkv, qkv, og, misc, gt, mg, c0, n0, m0)


def _cmp_accumulate(load, w_ref, s):
    acc = None
    for r in range(0, D_CMP, 2):
        lhs = jnp.concatenate([load(r), load(r + 1)], axis=1).astype(BF16)
        term = _dot(lhs, w_ref[s, r // 2])
        acc = term if acc is None else acc + term
    return acc


def _cmp_prompt_kernel(k_ref, v_ref, w_ref, fs_ref):
    n = k_ref.shape[0] // D_CMP
    for s, ref in enumerate((k_ref, v_ref)):
        fs_ref[s] = _cmp_accumulate(lambda r: ref[pl.ds(r, n, stride=D_CMP), :], w_ref, s)


def _cmp_prompt(rows, w_cmp, tr):
    M = rows.shape[0]
    return pl.pallas_call(
        _cmp_prompt_kernel,
        grid=(M // tr,),
        in_specs=[pl.BlockSpec((tr, KV_WIDTH), lambda i: (i, 0)), pl.BlockSpec((tr, KV_WIDTH), lambda i: (i, 1)),
                  pl.BlockSpec(w_cmp.shape, lambda i: (0, 0, 0, 0))],
        out_specs=pl.BlockSpec((2, tr // D_CMP, 2 * LANES), lambda i: (0, i, 0)),
        out_shape=jax.ShapeDtypeStruct((2, M // D_CMP, 2 * LANES), F32),
        compiler_params=_params("parallel"),
        name="cmp_prompt",
    )(rows, rows, w_cmp)


def _cmp_sample_kernel(n_group, pt_ref, *refs):
    page_refs, w_ref, fs_ref, rows_s = refs[:n_group], refs[n_group], refs[n_group + 1], refs[n_group + 2:]
    page = page_refs[0].shape[2]
    n = n_group * page // D_CMP
    for s in range(2):
        for i, p in enumerate(page_refs):
            rows_s[s][i * page:(i + 1) * page, :] = p[s].T
    for s in range(2):
        fs_ref[s, 0] = _cmp_accumulate(lambda r: rows_s[s][pl.ds(r, n, stride=D_CMP), :], w_ref, s)


def _cmp_sample(cache_t, page_table, layer, w_cmp, n_group):
    B, n_pages = page_table.shape
    page = cache_t.shape[4]
    per_page = page // D_CMP

    def page_spec(i):
        return pl.BlockSpec((None, None, 2, KV_WIDTH, page),
                            lambda b, p, pt: (pt[b, p * n_group + i], layer, 0, 0, 0))

    grid_spec = pltpu.PrefetchScalarGridSpec(
        num_scalar_prefetch=1,
        grid=(B, n_pages // n_group),
        in_specs=[page_spec(i) for i in range(n_group)]
        + [pl.BlockSpec(w_cmp.shape, lambda b, p, pt: (0, 0, 0, 0))],
        out_specs=pl.BlockSpec((2, 1, n_group * per_page, 2 * LANES), lambda b, p, pt: (0, b, p, 0)),
        scratch_shapes=[pltpu.VMEM((n_group * page, KV_WIDTH), F32), pltpu.VMEM((n_group * page, KV_WIDTH), F32)],
    )
    return pl.pallas_call(
        functools.partial(_cmp_sample_kernel, n_group),
        grid_spec=grid_spec,
        out_shape=jax.ShapeDtypeStruct((2, B, n_pages * per_page, 2 * LANES), F32),
        compiler_params=_params("parallel", "arbitrary"),
        name="cmp_sample",
    )(page_table, *([cache_t] * n_group), w_cmp)


def _compressed_kv(fs_ref, bias_ref, tabs, b=0):
    n = fs_ref.shape[2]
    out = []
    for s in range(2):
        fs = fs_ref[s, b]
        out.append(fs[:, 0:LANES] + pltpu.roll(fs[:, LANES:2 * LANES], n - 1, 0) + bias_ref[s:s + 1, :])
    return _rope128(out[0], *tabs), out[1]


def _masked_softmax(s, valid, axis):
    m = jnp.max(jnp.where(valid, s, NEG), axis=axis, keepdims=True)
    e = jnp.where(valid, jnp.exp(s - m), 0.0)
    l = jnp.sum(e, axis=axis, keepdims=True)
    return e * (1.0 / jnp.where(l > 0.0, l, 1.0))


def _bias_softmax(s, axis, keep=None):
    m = jnp.max(s, axis=axis, keepdims=True)
    e = jnp.exp2(s - m)
    scale = 1.0 / jnp.sum(e, axis=axis, keepdims=True)
    return e * (scale if keep is None else scale * keep)


def _split3(x):
    h1 = x.astype(BF16)
    r1 = x - h1.astype(F32)
    h2 = r1.astype(BF16)
    return h1, h2, (r1 - h2.astype(F32)).astype(BF16)


def _dot_split3(a, x):
    h1, h2, h3 = _split3(x)
    return _dot(a, h1) + _dot(a, h2) + _dot(a, h3)


def _dot_split3_rhs(x, a):
    h1, h2, h3 = _split3(x)
    return _dot(h1, a) + _dot(h2, a) + _dot(h3, a)


def _top_blocks(score, n_top, axis):
    pos = lax.broadcasted_iota(jnp.int32, score.shape, axis).astype(F32)
    picks = []
    for _ in range(n_top):
        mx = jnp.max(score, axis=axis, keepdims=True)
        idx = jnp.min(jnp.where(score == mx, pos, float(score.shape[axis])), axis=axis, keepdims=True)
        picks.append(idx)
        score = jnp.where(pos == idx, NEG, score)
    return picks, score


def _nsa_prompt_kernel(tk, q_ref, gt_ref, fs_ref, cb_ref, kc_c, kc_s1, kc_s2, ovt_ref,
                       ks_ref, kw_ref, vst_ref, vwt_ref, oa_ref,
                       kc_s, vct_s, sel_s, m_s, acc_s, sc_s, part_s, oc_s, imp_s):
    n_cmp = fs_ref.shape[2]
    n_sel = ovt_ref.shape[0]
    nq = Q_BLOCK
    cols = A_REP * nq
    per_tile = tk // L_SEL
    qi = pl.program_id(1)
    q0 = qi * nq

    @pl.when(qi == 0)
    def _():
        kc, vc = _compressed_kv(fs_ref, cb_ref, (kc_c[...], kc_s1[...], kc_s2[...]))
        kc_s[...] = kc.astype(BF16)
        vct_s[...] = vc.T.astype(BF16)

    t_pos = q0 + lax.broadcasted_iota(jnp.int32, (1, nq), 1)
    gates = gt_ref[0]

    def heads(a):
        return jnp.concatenate([a] * A_REP, axis=1)

    def all_heads(a):
        return jnp.concatenate([a] * A_HEADS, axis=1)

    cmp_end = lax.broadcasted_iota(jnp.int32, (n_cmp, 1), 0) * D_CMP + (L_CMP - 1)
    c_bias = jnp.where(cmp_end <= t_pos, 0.0, NEG)
    c_keep = jnp.where(t_pos >= L_CMP - 1, 1.0, 0.0)
    blk = lax.broadcasted_iota(jnp.int32, (n_sel, 1), 0)
    t_both = jnp.concatenate([t_pos] * A_KV, axis=1)
    cur = t_both // L_SEL
    forced = (blk == 0) | (blk == cur) | (blk == cur - 1)
    started = blk * L_SEL <= t_both
    w_start = jnp.maximum(q0 - WINDOW, 0)
    w_len = WINDOW + nq
    rel = t_pos - (w_start + lax.broadcasted_iota(jnp.int32, (w_len, 1), 0))
    w_bias = jnp.where((rel >= 0) & (rel < WINDOW), 0.0, NEG)
    n_tiles = (q0 + nq + tk - 1) // tk
    key_iota = lax.broadcasted_iota(jnp.int32, (tk, 1), 0)

    groups = range(A_KV)
    vs = [slice(g * A_DIM, (g + 1) * A_DIM) for g in groups]
    gs = [slice(g * cols, (g + 1) * cols) for g in groups]
    q_all = q_ref[...].reshape(A_HEADS * nq, LANES)

    n_cls = n_cmp // LANES if n_cmp % LANES == 0 else 1
    unit = n_cmp // n_cls
    cls = jnp.minimum(((q0 + nq) // D_CMP + unit - 1) // unit, n_cls)

    def compressed(rows):
        p_c = _bias_softmax(_dot_nt(kc_s[0:rows, :], q_all) + all_heads(c_bias[0:rows]), 0, all_heads(c_keep))
        p_c16 = p_c.astype(BF16)
        p_sums = []
        for g in groups:
            oc_s[g] = _dot(vct_s[vs[g], 0:rows], p_c16[:, gs[g]])
            p_sum = p_c[:, g * cols:g * cols + nq]
            for r in range(1, A_REP):
                p_sum = p_sum + p_c[:, g * cols + r * nq:g * cols + (r + 1) * nq]
            p_sums.append(p_sum)
        imp_s[...] = _dot_split3(ovt_ref[:, 0:rows], jnp.concatenate(p_sums, axis=1))

    for k in range(1, n_cls + 1):
        pl.when(cls == k)(functools.partial(compressed, k * unit))
    o_c = [oc_s[g] for g in groups]

    score = jnp.where(forced, FORCE_SCORE, jnp.where(started, imp_s[...], -1.0))
    _, left_all = _top_blocks(score, min(N_TOP, n_sel), 0)
    for g in groups:
        left = left_all[:, g * nq:(g + 1) * nq]
        sel_s[g] = jnp.where(left < -2.0, 0.0, NEG)

    ws = pl.multiple_of(w_start, nq)
    p_w = _bias_softmax(_dot_nt(kw_ref[0, pl.ds(ws, w_len), :], q_all) + all_heads(w_bias), 0).astype(BF16)
    wc = w_start // LANES
    vw_t = jnp.concatenate([vwt_ref[0, wc + c] for c in range(w_len // LANES)], axis=1)
    for g in groups:
        o_w = _dot(vw_t[vs[g], :], p_w[:, gs[g]])
        for r in range(A_REP):
            hd = g * A_REP + r
            base = 2 * M_HEADS + hd * N_BRANCH
            cs = slice(r * nq, (r + 1) * nq)
            part_s[hd * A_DIM:(hd + 1) * A_DIM, :] = (gates[base:base + 1] * o_c[g][:, cs]
                                                      + gates[base + 2:base + 3] * o_w[:, cs])

    m_s[...] = jnp.full(m_s.shape, NEG, F32)
    acc_s[...] = jnp.zeros(acc_s.shape, F32)
    ones_rows = jnp.ones((acc_s.shape[1] - A_DIM, tk), BF16)

    last_tile = ks_ref.shape[1] // tk - 1

    def scores(kt):
        k0 = pl.multiple_of(jnp.minimum(kt, last_tile) * tk, tk)
        return _dot_nt(ks_ref[0, pl.ds(k0, tk), :], q_all)

    def attend(kt, slot):
        kc = jnp.minimum(kt, last_tile)
        causal = kt * tk + key_iota <= t_pos
        bias = []
        for g in groups:
            chosen = jnp.concatenate(
                [jnp.broadcast_to(sel_s[g, pl.ds(kc * per_tile + i, 1), :], (L_SEL, nq)) for i in range(per_tile)],
                axis=0)
            bias.append(heads(jnp.where(causal, chosen, NEG)))
        s = sc_s[slot] + jnp.concatenate(bias, axis=1)
        m_old = m_s[...]
        m_new = jnp.maximum(m_old, jnp.max(s, axis=0, keepdims=True))
        e = jnp.exp2(s - m_new)
        a = jnp.exp2(m_old - m_new)
        e16 = e.astype(BF16)
        for g in groups:
            vals = jnp.concatenate([vst_ref[0, kc, vs[g], :], ones_rows], axis=0)
            acc_s[g] = a[:, gs[g]] * acc_s[g] + _dot(vals, e16[:, gs[g]])
        m_s[...] = m_new

    sc_s[0] = scores(0)

    def pair(j, carry):
        sc_s[1] = scores(2 * j + 1)
        attend(2 * j, 0)
        sc_s[0] = scores(2 * j + 2)
        attend(2 * j + 1, 1)
        return carry

    lax.fori_loop(0, (n_tiles + 1) // 2, pair, 0)

    heads_out = []
    for g in groups:
        l = acc_s[g, A_DIM:A_DIM + 1, :]
        o_s = acc_s[g, 0:A_DIM, :] * (1.0 / jnp.where(l > 0.0, l, 1.0))
        for r in range(A_REP):
            hd = g * A_REP + r
            gate = gates[2 * M_HEADS + hd * N_BRANCH + 1:2 * M_HEADS + hd * N_BRANCH + 2]
            heads_out.append(part_s[hd * A_DIM:(hd + 1) * A_DIM, :] + gate * o_s[:, r * nq:(r + 1) * nq])
    oa_ref[0] = jnp.concatenate(heads_out, axis=0).T.astype(BF16)


def _nsa_prompt(q8, gt, fs, cbias, kc_tabs, ovt, kk, vst, vwt, B, T, tk):
    nqb = T // Q_BLOCK
    n_cmp = fs.shape[2]
    n_sel = ovt.shape[0]
    cols = A_REP * Q_BLOCK
    const = lambda a: pl.BlockSpec(a.shape, lambda b, i: (0,) * a.ndim)
    keys = lambda j: pl.BlockSpec((1, T, LANES), lambda b, i: (b, 0, j))
    per_b = lambda a: pl.BlockSpec((1,) + a.shape[1:], lambda b, i: (b, 0, 0, 0))
    return pl.pallas_call(
        functools.partial(_nsa_prompt_kernel, tk),
        grid=(B, nqb),
        in_specs=[pl.BlockSpec((A_HEADS, Q_BLOCK, LANES), lambda b, i: (0, b * nqb + i, 0)),
                  pl.BlockSpec((1, gt.shape[1], Q_BLOCK), lambda b, i: (b, 0, i)),
                  pl.BlockSpec((2, 1, n_cmp, 2 * LANES), lambda b, i: (0, b, 0, 0)),
                  const(cbias), const(kc_tabs[0]), const(kc_tabs[1]), const(kc_tabs[2]),
                  const(ovt), keys(0), keys(1), per_b(vst), per_b(vwt)],
        out_specs=pl.BlockSpec((1, Q_BLOCK, A_WIDTH), lambda b, i: (b, i, 0)),
        out_shape=jax.ShapeDtypeStruct((B, T, A_WIDTH), BF16),
        scratch_shapes=[pltpu.VMEM((n_cmp, LANES), BF16), pltpu.VMEM((LANES, n_cmp), BF16),
                        pltpu.VMEM((A_KV, n_sel, Q_BLOCK), F32),
                        pltpu.VMEM((1, A_KV * cols), F32),
                        pltpu.VMEM((A_KV, A_DIM + 16, cols), F32),
                        pltpu.VMEM((2, tk, A_KV * cols), F32),
                        pltpu.VMEM((A_WIDTH, Q_BLOCK), F32),
                        pltpu.VMEM((A_KV, A_DIM, cols), F32),
                        pltpu.VMEM((n_sel, A_KV * Q_BLOCK), F32)],
        compiler_params=_params("parallel", "arbitrary"),
        name="nsa_prompt",
    )(q8, gt, fs, cbias, *kc_tabs, ovt, kk, kk, vst, vwt)


def _nsa_sample_select_kernel(q_pos, n_sel, q_ref, fs_ref, cb_ref, kc_c, kc_s1, kc_s2, ov_ref, oc_ref, idx_ref):
    n_seq, n_cmp = fs_ref.shape[1], fs_ref.shape[2]
    tabs = (kc_c[...], kc_s1[...], kc_s2[...])
    cmp_end = lax.broadcasted_iota(jnp.int32, (A_HEADS, n_cmp), 1) * D_CMP + (L_CMP - 1)
    head = lax.broadcasted_iota(jnp.int32, (A_HEADS, n_cmp), 0)
    imps = []
    for b in range(n_seq):
        kc, vc = _compressed_kv(fs_ref, cb_ref, tabs, b)
        p_c = _masked_softmax(_dot_nt(q_ref[b], kc.astype(BF16)), cmp_end <= q_pos, 1)
        oc_ref[b] = _dot(p_c.astype(BF16), vc.astype(BF16))
        p_sum = jnp.zeros((A_HEADS, n_cmp), F32)
        for g in range(A_KV):
            pg = jnp.sum(jnp.where(head // A_REP == g, p_c, 0.0), axis=0, keepdims=True)
            p_sum = jnp.where(head == g, pg, p_sum)
        imps.append(_dot_split3_rhs(p_sum, ov_ref[...]))
    imp = jnp.concatenate(imps, axis=0)
    n_pad = ov_ref.shape[1]
    blk = lax.broadcasted_iota(jnp.int32, (A_HEADS * n_seq, n_pad), 1)
    cur = q_pos // L_SEL
    forced = (blk == 0) | (blk == cur) | (blk == cur - 1)
    score = jnp.where(forced, FORCE_SCORE, jnp.where(blk * L_SEL <= q_pos, imp, -1.0))
    score = jnp.where(blk < n_sel, score, 2 * NEG)
    picks, _ = _top_blocks(score, min(N_TOP, n_sel), 1)
    lane = lax.broadcasted_iota(jnp.int32, (A_HEADS * n_seq, LANES), 1)
    out = jnp.zeros((A_HEADS * n_seq, LANES), jnp.int32)
    for j, idx in enumerate(picks):
        out = jnp.where(lane == j, idx.astype(jnp.int32), out)
    for b in range(n_seq):
        idx_ref[b] = out[b * A_HEADS:(b + 1) * A_HEADS]


def _nsa_sample_select(q8, fs, cbias, kc_tabs, ov, q_pos, n_sel, n_seq):
    B = q8.shape[0]
    n_cmp = fs.shape[2]
    const = lambda a: pl.BlockSpec(a.shape, lambda b: (0,) * a.ndim)
    per_b = pl.BlockSpec((n_seq, A_HEADS, LANES), lambda b: (b, 0, 0))
    return pl.pallas_call(
        functools.partial(_nsa_sample_select_kernel, q_pos, n_sel),
        grid=(B // n_seq,),
        in_specs=[per_b, pl.BlockSpec((2, n_seq, n_cmp, 2 * LANES), lambda b: (0, b, 0, 0)),
                  const(cbias), const(kc_tabs[0]), const(kc_tabs[1]), const(kc_tabs[2]), const(ov)],
        out_specs=(per_b, per_b),
        out_shape=(jax.ShapeDtypeStruct((B, A_HEADS, LANES), F32),
                   jax.ShapeDtypeStruct((B, A_HEADS, LANES), jnp.int32)),
        compiler_params=_params("parallel"),
        name="nsa_sample_select",
    )(q8, fs, cbias, *kc_tabs, ov)


def _r16(a):
    return a.astype(BF16).astype(F32)


def _nsa_sample_attend_kernel(q_pos, n_top, pt_ref, ix_ref, q_ref, new_ref, newc_ref, oc_ref, misc_ref, win_ref,
                              *refs):
    page_refs, (oa_ref, nwin_ref) = refs[:A_KV * n_top], refs[A_KV * n_top:]
    b = pl.program_id(0)
    q = q_ref[0]
    qf = q.astype(F32)
    head = lax.broadcasted_iota(jnp.int32, (A_HEADS, 1), 0)
    new = new_ref[0]
    page = page_refs[0].shape[2]
    per_page = page // L_SEL
    lane = lax.broadcasted_iota(jnp.int32, (A_HEADS, page), 1)
    s_tok = jnp.sum(qf * _r16(new[0:1, :]), axis=-1, keepdims=True)

    o_s = jnp.zeros((A_HEADS, LANES), F32)
    for g in range(A_KV):
        parts, n_fresh = [], 0
        for j in range(n_top):
            blk = ix_ref[b, g, j]
            fresh = blk * L_SEL >= q_pos
            limit = jnp.where(fresh, -1, q_pos)
            s = _dot(q, page_refs[g * n_top + j][0].astype(BF16))
            s_pos = (blk // per_page) * page + lane
            ok = jnp.where(lane // L_SEL == blk % per_page, s_pos, limit + 1) <= limit
            parts.append(jnp.where(ok, s, NEG))
            n_fresh = n_fresh + jnp.where(fresh, 1, 0)
        s_all = jnp.concatenate(parts, axis=1)
        has_tok = jnp.where(head >= 0, n_fresh, 0) > 0
        m = jnp.maximum(jnp.max(s_all, axis=-1, keepdims=True), jnp.where(has_tok, s_tok, NEG))
        e = jnp.exp(s_all - m)
        e_tok = jnp.where(has_tok, jnp.exp(s_tok - m), 0.0)
        den = jnp.sum(e, axis=-1, keepdims=True) + e_tok
        e16 = e.astype(BF16)
        acc = _r16(e_tok) * _r16(new[1:2, :])
        for j in range(n_top):
            acc = acc + _dot_nt(e16[:, j * page:(j + 1) * page], page_refs[g * n_top + j][1].astype(BF16))
        o_s = jnp.where(head // A_REP == g, acc / den, o_s)

    wb = win_ref.shape[2]
    kwt, vwt = win_ref[0], win_ref[1]
    s_old = _dot(q, kwt.astype(BF16))
    s_new = jnp.sum(qf * _r16(new[2:3, :]), axis=-1, keepdims=True)
    rel = wb - lax.broadcasted_iota(jnp.int32, (A_HEADS, wb), 1)
    ok = (rel >= 0) & (rel < WINDOW) & (q_pos - rel >= 0)
    m = jnp.maximum(jnp.max(jnp.where(ok, s_old, NEG), axis=-1, keepdims=True), s_new)
    e_old = jnp.where(ok, jnp.exp(s_old - m), 0.0)
    e_new = jnp.exp(s_new - m)
    den = jnp.sum(e_old, axis=-1, keepdims=True) + e_new
    o_w = _dot_nt((e_old / den).astype(BF16), vwt.astype(BF16)) + _r16(e_new / den) * _r16(new[3:4, :])
    gates = misc_ref[0]
    oa_ref[0] = gates[:, 0:1] * oc_ref[0] + gates[:, 1:2] * o_s + gates[:, 2:3] * o_w
    newc = newc_ref[0]
    last = lax.broadcasted_iota(jnp.int32, (LANES, wb), 1) == wb - 1
    nwin_ref[0] = jnp.where(last, newc[:, 2:3], pltpu.roll(kwt, wb - 1, 1))
    nwin_ref[1] = jnp.where(last, newc[:, 3:4], pltpu.roll(vwt, wb - 1, 1))


def _nsa_sample_attend(page_table, idx, q8, new_rows, new_cols, o_c, gate_rows, win_t, cache_t, layer, q_pos):
    B, n_pages = page_table.shape
    n_top = idx.shape[2]
    page = cache_t.shape[4]
    per_page = page // L_SEL
    wb = win_t.shape[4]

    def page_spec(g, j):
        def index(b, pt, ix):
            pg = jnp.clip(ix[b, g, j] // per_page, 0, n_pages - 1)
            return (pt[b, pg], layer, 1, 0, 0)
        return pl.BlockSpec((None, None, 2, KV_WIDTH, page), index)

    per_b = pl.BlockSpec((1, A_HEADS, LANES), lambda b, pt, ix: (b, 0, 0))
    pages = [page_spec(g, j) for g in range(A_KV) for j in range(n_top)]
    grid_spec = pltpu.PrefetchScalarGridSpec(
        num_scalar_prefetch=2,
        grid=(B,),
        in_specs=[per_b, per_b, pl.BlockSpec((1, KV_WIDTH, A_HEADS), lambda b, pt, ix: (b, 0, 0)), per_b, per_b,
                  pl.BlockSpec((None, None, 2, KV_WIDTH, wb), lambda b, pt, ix: (b, layer, 0, 0, 0))] + pages,
        out_specs=(per_b, pl.BlockSpec((None, 2, KV_WIDTH, wb), lambda b, pt, ix: (b, 0, 0, 0))),
    )
    return pl.pallas_call(
        functools.partial(_nsa_sample_attend_kernel, q_pos, n_top),
        grid_spec=grid_spec,
        out_shape=(jax.ShapeDtypeStruct((B, A_HEADS, LANES), F32),
                   jax.ShapeDtypeStruct((B, 2, KV_WIDTH, wb), F32)),
        compiler_params=_params("parallel"),
        name="nsa_sample_attend",
    )(page_table, idx, q8, new_rows, new_cols, o_c, gate_rows, win_t, *([cache_t] * len(pages)))


def _ffn_kernel(final, tf, x_ref, hm_ref, oa_ref, wo_ref, nf_ref, wg_ref, wu_ref, wd_ref, nfin_ref, out_ref, act_s):
    half = hm_ref.shape[1]
    x1 = x_ref[...] + _dot(hm_ref[...], wo_ref[0:half, :]) + _dot(oa_ref[...], wo_ref[half:, :])
    ms = jnp.mean(x1 * x1, axis=-1, keepdims=True)
    xn = ((x1 * lax.rsqrt(ms + EPS)) * nf_ref[...]).astype(BF16)
    for c in range(wg_ref.shape[1] // tf):
        sl = slice(c * tf, (c + 1) * tf)
        gate = _dot(xn, wg_ref[:, sl])
        act_s[:, sl] = ((gate * _sigmoid(gate)) * _dot(xn, wu_ref[:, sl])).astype(BF16)
    y = x1 + _dot(act_s[...], wd_ref[...])
    if final:
        ms = jnp.mean(y * y, axis=-1, keepdims=True)
        y = (y * lax.rsqrt(ms + EPS)) * nfin_ref[...]
    out_ref[...] = y


def _ffn(x, hm, oa, wo, nf, wg, wu, wd, nfin, final, tm, tf):
    M, D = x.shape
    F = wg.shape[1]
    row = lambda w: pl.BlockSpec((tm, w), lambda i: (i, 0))
    const = lambda a: pl.BlockSpec(a.shape, lambda i: (0,) * a.ndim, pipeline_mode=pl.Buffered(1))
    return pl.pallas_call(
        functools.partial(_ffn_kernel, final, tf),
        grid=(M // tm,),
        in_specs=[row(D), row(hm.shape[1]), row(oa.shape[1]), const(wo), const(nf),
                  const(wg), const(wu), const(wd), const(nfin)],
        out_specs=row(D),
        out_shape=jax.ShapeDtypeStruct((M, D), F32),
        scratch_shapes=[pltpu.VMEM((tm, F), BF16)],
        compiler_params=_params("parallel"),
        name="ffn",
    )(x, hm, oa, wo, nf, wg, wu, wd, nfin)


def _permute_w_in(w):
    a = 4 * M_WIDTH
    b = a + 2 * M_HEADS
    c = b + A_WIDTH + 6 * KV_WIDTH
    pad = jnp.zeros((w.shape[0], LANES - N_MISC), w.dtype)
    return jnp.concatenate([w[:, :a], w[:, b:c], w[:, a:b], w[:, c:], pad], axis=1).astype(BF16)


def _cmp_weights(wk, wv):
    def one(w):
        z = jnp.zeros((D_CMP, A_DIM, A_DIM), w.dtype)
        halves = []
        for part in (w[:D_CMP], w[D_CMP:]):
            top = jnp.concatenate([part, z], axis=2)
            bot = jnp.concatenate([z, part], axis=2)
            halves.append(jnp.concatenate([top, bot], axis=1))
        return jnp.concatenate(halves, axis=2)
    return jnp.stack([one(wk), one(wv)]).astype(BF16).reshape(2, D_CMP // 2, 2 * KV_WIDTH, 2 * LANES)


def _overlap(n_cmp_pad, n_sel_pad, n_cmp, n_sel):
    lo_c = np.arange(n_cmp_pad)[:, None] * D_CMP
    lo_s = np.arange(n_sel_pad)[None, :] * L_SEL
    ov = np.clip(np.minimum(lo_c + L_CMP, lo_s + L_SEL) - np.maximum(lo_c, lo_s), 0, None).astype(np.float32) / L_CMP
    ov[n_cmp:, :] = 0.0
    ov[:, n_sel:] = 0.0
    return jnp.asarray(ov)


def _pick(n, prefs):
    for p in prefs:
        if n % p == 0:
            return p
    return n


def kernel(x_prompt, x_sample, cache_nsa_kv, page_table, state_win_kv, state_mlstm_c, state_mlstm_n, state_mlstm_m, norm_mix, w_in, b_if, m_norm, w_cmp_k, b_cmp_k, w_cmp_v, b_cmp_v, w_out, norm_ffn, w_gate, w_up, w_down, norm_final):
    B, T, D = x_prompt.shape
    Bs, Ts, _ = x_sample.shape
    depth = w_in.shape[0]
    n_pool, _, n_slots, page = cache_nsa_kv.shape[:4]
    n_pages = page_table.shape[1]
    past = n_pages * page
    wb = state_win_kv.shape[3]
    assert Ts == 1 and T % Q_BLOCK == 0 and T >= WINDOW + Q_BLOCK and wb == WINDOW
    assert past % page == 0 and page % L_SEL == 0 and page % D_CMP == 0
    assert ((past + Ts) // D_CMP) * D_CMP <= past

    Mp = B * T
    tm_p = _pick(Mp, (512, 256, 128))
    tm_f = _pick(Mp, (512, 256, 128))
    ff = w_gate.shape[2]
    tf = _pick(ff, (256, 128))
    l_chunk = _pick(T, (256, 128, 64))
    ls_pad = 16
    tk = _pick(T, (512, 256, 128))
    assert tk % tm_p == 0 and tm_p % LANES == 0
    n_group = _pick(n_pages, (16, 8, 4, 2, 1))
    tr = _pick(T, (2048, 1024, 512, 256))

    n_chunk_p = T // D_CMP
    n_cmp_p, n_sel_p = n_chunk_p - 1, -(-T // L_SEL)
    tabs_p = _rope_tables(jnp.arange(T))
    kc_tabs_p = _rope_tables(jnp.arange(n_chunk_p) * D_CMP)
    assert L_CMP <= 256
    ovt_p = _overlap(n_chunk_p, n_sel_p, n_cmp_p, n_sel_p).T.astype(BF16)
    n_chunk_s = (past + Ts) // D_CMP
    n_cmp_s, n_sel_s = n_chunk_s - 1, -(-(past + Ts) // L_SEL)
    n_sel_pad = -(-n_sel_s // LANES) * LANES
    tabs_s = _rope_tables(jnp.full((Bs,), past))
    kc_tabs_s = _rope_tables(jnp.arange(n_chunk_s) * D_CMP)
    ov_s = _overlap(n_chunk_s, n_sel_pad, n_cmp_s, n_sel_s).astype(BF16)
    cache_t = cache_nsa_kv.transpose(0, 1, 2, 4, 5, 3).reshape(n_pool, depth, n_slots, KV_WIDTH, page)
    win_t = state_win_kv.transpose(0, 1, 2, 4, 5, 3).reshape(Bs, depth, 2, KV_WIDTH, wb)

    xp = x_prompt.reshape(Mp, D)
    xs = x_sample.reshape(Bs, D)
    rows_s, win_p, win_s, c_p, c_s, n_p, n_s, m_p, m_s = ([] for _ in range(9))
    y_p = y_s = kvt = None
    nfin = norm_final.reshape(1, D)
    for l in range(depth):
        w_perm = _permute_w_in(w_in[l])
        gain = norm_mix[l].reshape(1, D)
        bias = jnp.zeros((1, LANES), F32).at[0, :2 * M_HEADS].set(b_if[l].reshape(-1))
        mg = m_norm[l].reshape(1, M_WIDTH)
        w_cmp = _cmp_weights(w_cmp_k[l], w_cmp_v[l])
        cbias = jnp.stack([jnp.tile(b_cmp_k[l], A_KV), jnp.tile(b_cmp_v[l], A_KV)])
        wo, nf = w_out[l].astype(BF16), norm_ffn[l].reshape(1, D)
        wg, wu, wd = w_gate[l].astype(BF16), w_up[l].astype(BF16), w_down[l].astype(BF16)
        final = l == depth - 1

        qkv, og, q8, rows, win, misc, kk, vst, vwt, kvt = _inproj(xp, gain, w_perm, bias, tabs_p, tm_p,
                                                                    (tk, depth, l, kvt))
        misc3 = misc.reshape(B, T, LANES)
        gt = misc3[:, :, :N_MISC].transpose(0, 2, 1)
        hm, c_new, n_new, m_new = _mlstm(
            qkv.reshape(B, T, -1), og.reshape(B, T, -1), misc3, gt, mg,
            jnp.zeros((B, M_HEADS, M_DIM, M_DIM), F32), jnp.zeros((B, M_HEADS, M_DIM), F32),
            jnp.full((B, M_HEADS, M_DIM), NEG, F32), l_chunk)
        fs = _cmp_prompt(rows, w_cmp, tr).reshape(2, B, n_chunk_p, 2 * LANES)
        oa = _nsa_prompt(q8, gt, fs, cbias, kc_tabs_p, ovt_p, kk.reshape(B, T, -1),
                         vst.reshape(B, T // tk, KV_WIDTH, tk), vwt.reshape(B, T // LANES, KV_WIDTH, LANES), B, T, tk)
        xp_new = _ffn(xp, hm.reshape(Mp, -1), oa.reshape(Mp, -1), wo, nf, wg, wu, wd, nfin, final, tm_f, tf)
        if final:
            y_p = xp_new
        xp = xp_new
        win_last = win.reshape(B, T, 2 * KV_WIDTH)[:, T - wb:]
        win_p.append(win_last.reshape(B, wb, 2, A_KV, A_DIM).transpose(0, 2, 1, 3, 4))
        c_p.append(c_new); n_p.append(n_new); m_p.append(m_new[:, :, 0])

        qkv, og, q8, rows, win, misc = _inproj(xs, gain, w_perm, bias, tabs_s, Bs)
        pad_t = lambda a: jnp.pad(a[:, None, :], ((0, 0), (0, ls_pad - 1), (0, 0)))
        inert = jnp.zeros((ls_pad, LANES), F32).at[1:, :M_HEADS].set(NEG)
        misc_pad = pad_t(misc) + inert[None]
        gt = misc_pad[:, :, :2 * M_HEADS].transpose(0, 2, 1)
        m0 = jnp.broadcast_to(state_mlstm_m[:, l, :, None].astype(F32), (Bs, M_HEADS, M_DIM))
        hm, c_new, n_new, m_new = _mlstm(
            pad_t(qkv), pad_t(og), misc_pad, gt, mg,
            state_mlstm_c[:, l].astype(F32), state_mlstm_n[:, l].astype(F32), m0, ls_pad)
        hm = hm[:, 0]
        fs = _cmp_sample(cache_t, page_table, l, w_cmp, n_group)
        q8s = q8.transpose(1, 0, 2)
        o_c, idx = _nsa_sample_select(q8s, fs, cbias, kc_tabs_s, ov_s, past, n_sel_s, _pick(Bs, (4, 2, 1)))
        idx = idx[:, :A_KV, :min(N_TOP, n_sel_s)]
        new_rows = jnp.pad(jnp.stack([rows[:, 2 * LANES:3 * LANES], rows[:, 3 * LANES:], win[:, :LANES],
                                      win[:, LANES:]], axis=1), ((0, 0), (0, A_HEADS - 4), (0, 0)))
        gate_rows = jnp.pad(misc[:, 2 * M_HEADS:N_MISC].reshape(Bs, A_HEADS, N_BRANCH),
                            ((0, 0), (0, 0), (0, LANES - N_BRANCH)))
        oa8, nwin = _nsa_sample_attend(page_table, idx, q8s, new_rows, new_rows.transpose(0, 2, 1), o_c, gate_rows,
                                       win_t, cache_t, l, past)
        grp = (jnp.arange(A_HEADS) // A_REP)[None, :, None, None]
        oa = jnp.take_along_axis(oa8.reshape(Bs, A_HEADS, A_KV, A_DIM), jnp.broadcast_to(grp, (Bs, A_HEADS, 1, A_DIM)),
                                 axis=2).reshape(Bs, A_WIDTH).astype(BF16)
        xs_new = _ffn(xs, hm, oa, wo, nf, wg, wu, wd, nfin, final, Bs, tf)
        if final:
            y_s = xs_new
        xs = xs_new
        rows_s.append(rows.reshape(Bs, Ts, n_slots, A_KV, A_DIM))
        win_s.append(nwin.reshape(Bs, 2, A_KV, A_DIM, wb).transpose(0, 1, 4, 2, 3))
        c_s.append(c_new); n_s.append(n_new); m_s.append(m_new[:, :, 0])

    return (y_p.reshape(B, T, D), y_s.reshape(Bs, Ts, D),
            kvt.reshape(B, depth, n_slots, A_KV, A_DIM, T).transpose(0, 5, 1, 2, 3, 4), jnp.stack(rows_s, axis=2),
            jnp.stack(win_p, axis=1), jnp.stack(win_s, axis=1),
            jnp.stack(c_p, axis=1), jnp.stack(c_s, axis=1),
            jnp.stack(n_p, axis=1), jnp.stack(n_s, axis=1),
            jnp.stack(m_p, axis=1), jnp.stack(m_s, axis=1))
```

```python
import functools

import jax
import jax.numpy as jnp
import numpy as np
from jax import lax
from jax.experimental import pallas as pl
from jax.experimental.pallas import tpu as pltpu

F32 = jnp.float32
BF16 = jnp.bfloat16

M_HEADS = 4
M_DIM = 128
M_WIDTH = M_HEADS * M_DIM
A_HEADS = 8
A_DIM = 64
A_KV = 2
A_REP = A_HEADS // A_KV
A_WIDTH = A_HEADS * A_DIM
KV_WIDTH = A_KV * A_DIM
L_CMP = 32
D_CMP = 16
L_SEL = 64
N_TOP = 16
WINDOW = 512
Q_BLOCK = 128
N_BRANCH = 3
ROPE_THETA = 500000.0
ROPE_DIM = A_DIM // 4
EPS = 1e-6
NEG = -1e30
LOG2_E = 1.4426950408889634
FORCE_SCORE = 1e4
LANES = 128
N_MISC = 2 * M_HEADS + N_BRANCH * A_HEADS
D_PERM = 4 * M_WIDTH + A_WIDTH + 6 * KV_WIDTH + LANES
VMEM_LIMIT = 56 * 1024 * 1024

_NT = (((1,), (1,)), ((), ()))
_TN = (((0,), (0,)), ((), ()))


def _params(*sem):
    return pltpu.CompilerParams(dimension_semantics=sem, vmem_limit_bytes=VMEM_LIMIT)


def _sigmoid(x):
    return 1.0 / (1.0 + jnp.exp(-x))


def _dot(a, b):
    return jnp.dot(a, b, preferred_element_type=F32)


def _dot_exact(a, b):
    return jnp.dot(a, b, preferred_element_type=F32, precision=lax.Precision.HIGHEST)


def _dot_nt(a, b):
    return lax.dot_general(a, b, _NT, preferred_element_type=F32)


def _rope128(v, c, s1, s2):
    half = ROPE_DIM // 2
    return v * c + pltpu.roll(v, LANES - half, 1) * s1 + pltpu.roll(v, half, 1) * s2


def _rope_tables(pos):
    half = ROPE_DIM // 2
    inv = ROPE_THETA ** (-jnp.arange(half, dtype=F32) / half)
    ang = pos.astype(F32)[:, None] * inv[None, :]
    cos, sin = jnp.cos(ang), jnp.sin(ang)
    n = pos.shape[0]
    one = jnp.ones((n, A_DIM - ROPE_DIM), F32)
    zero = jnp.zeros((n, A_DIM - ROPE_DIM), F32)
    zh = jnp.zeros((n, half), F32)
    c = jnp.concatenate([cos, cos, one], axis=1)
    s1 = jnp.concatenate([-sin, zh, zero], axis=1)
    s2 = jnp.concatenate([zh, sin, zero], axis=1)
    tile = lambda a: jnp.concatenate([a, a], axis=1)
    return tile(c), tile(s1), tile(s2)


def _inproj_kernel(q_scale, n_in, *refs):
    x_ref, g_ref, w_ref, bias_ref, rc_ref, rs1_ref, rs2_ref = refs[:7]
    om_ref, og_ref, oq_ref, orows_ref, owin_ref, omisc_ref = refs[n_in:n_in + 6]
    attn_refs = refs[n_in + 6:]
    x = x_ref[...]
    ms = jnp.mean(x * x, axis=-1, keepdims=True)
    xn = ((x * lax.rsqrt(ms + EPS)) * g_ref[...]).astype(BF16)
    c, s1, s2 = rc_ref[...], rs1_ref[...], rs2_ref[...]
    lane = lax.broadcasted_iota(jnp.int32, (x.shape[0], LANES), 1)

    def proj(a, b):
        return _dot(xn, w_ref[:, a:b])

    om_ref[:, 0:M_WIDTH] = (proj(0, M_WIDTH) * (M_DIM ** -0.5)).astype(BF16)
    om_ref[:, M_WIDTH:3 * M_WIDTH] = proj(M_WIDTH, 3 * M_WIDTH).astype(BF16)
    og_ref[...] = _sigmoid(proj(3 * M_WIDTH, 4 * M_WIDTH))

    off = 4 * M_WIDTH
    pq = proj(off, off + A_WIDTH)
    for j in range(A_WIDTH // LANES):
        blk = _rope128(pq[:, j * LANES:(j + 1) * LANES], c, s1, s2) * q_scale
        swapped = pltpu.roll(blk, A_DIM, 1)
        for e in range(2):
            hd = 2 * j + e
            grp = hd // A_REP
            src = blk if e == grp else swapped
            keep = (lane // A_DIM) == grp
            oq_ref[hd] = jnp.where(keep, src, 0.0).astype(BF16)

    off += A_WIDTH
    pk = proj(off, off + 6 * KV_WIDTH)
    kcr, vcr, ksl, vsl, kw, vw = [pk[:, i * LANES:(i + 1) * LANES] for i in range(6)]
    ksl = _rope128(ksl, c, s1, s2)
    kw = _rope128(kw, c, s1, s2)
    owin_ref[:, 0:LANES] = kw
    owin_ref[:, LANES:2 * LANES] = vw
    if attn_refs:
        okk_ref, ovst_ref, ovwt_ref, okvt_ref = attn_refs
        orows_ref[:, 0:LANES] = kcr
        orows_ref[:, LANES:2 * LANES] = vcr
        vsl_t = vsl.T
        for i, a_t in enumerate((kcr.T, vcr.T, ksl.T, vsl_t)):
            okvt_ref[i * LANES:(i + 1) * LANES, :] = a_t
        okk_ref[:, 0:LANES] = ksl.astype(BF16)
        okk_ref[:, LANES:2 * LANES] = kw.astype(BF16)
        ovst_ref[0] = vsl_t.astype(BF16)
        for i in range(ovwt_ref.shape[0]):
            ovwt_ref[i] = vw[i * LANES:(i + 1) * LANES].T.astype(BF16)
    else:
        for i, a in enumerate((kcr, vcr, ksl, vsl)):
            orows_ref[:, i * LANES:(i + 1) * LANES] = a

    off += 6 * KV_WIDTH
    pm = proj(off, off + LANES) + bias_ref[...]
    log_sig = -(jnp.maximum(-pm, 0.0) + jnp.log(1.0 + jnp.exp(-jnp.abs(pm))))
    omisc_ref[...] = jnp.where(lane < M_HEADS, pm, jnp.where(lane < 2 * M_HEADS, log_sig, _sigmoid(pm)))


def _inproj(x, gain, w_perm, bias, tabs, tm, attn=None):
    M, D = x.shape
    nt = tabs[0].shape[0] // tm
    row = lambda i: (i, 0)
    tab = pl.BlockSpec((tm, LANES), lambda i: (i % nt, 0))
    full = lambda a: pl.BlockSpec(a.shape, lambda i: (0,) * a.ndim)
    out_shape = (
        jax.ShapeDtypeStruct((M, 3 * M_WIDTH), BF16),
        jax.ShapeDtypeStruct((M, M_WIDTH), F32),
        jax.ShapeDtypeStruct((A_HEADS, M, LANES), BF16),
        jax.ShapeDtypeStruct((M, (2 if attn else 4) * KV_WIDTH), F32),
        jax.ShapeDtypeStruct((M, 2 * KV_WIDTH), F32),
        jax.ShapeDtypeStruct((M, LANES), F32),
    )
    out_specs = (
        pl.BlockSpec((tm, 3 * M_WIDTH), row),
        pl.BlockSpec((tm, M_WIDTH), row),
        pl.BlockSpec((A_HEADS, tm, LANES), lambda i: (0, i, 0)),
        pl.BlockSpec((tm, (2 if attn else 4) * KV_WIDTH), row),
        pl.BlockSpec((tm, 2 * KV_WIDTH), row),
        pl.BlockSpec((tm, LANES), row),
    )
    in_specs = [pl.BlockSpec((tm, D), row), full(gain), full(w_perm), full(bias), tab, tab, tab]
    operands = [x, gain, w_perm, bias, *tabs]
    aliases = {}
    if attn:
        tk, depth, layer, kvt = attn
        per = tk // tm
        seq = tabs[0].shape[0]
        out_shape += (jax.ShapeDtypeStruct((M, 2 * KV_WIDTH), BF16),
                      jax.ShapeDtypeStruct((M // tk, KV_WIDTH, tk), BF16),
                      jax.ShapeDtypeStruct((M // LANES, KV_WIDTH, LANES), BF16),
                      jax.ShapeDtypeStruct((M // seq, depth, 4 * KV_WIDTH, seq), F32))
        out_specs += (pl.BlockSpec((tm, 2 * KV_WIDTH), row),
                      pl.BlockSpec((1, KV_WIDTH, tm), lambda i: (i // per, 0, i % per)),
                      pl.BlockSpec((tm // LANES, KV_WIDTH, LANES), lambda i: (i, 0, 0)),
                      pl.BlockSpec((None, None, 4 * KV_WIDTH, tm), lambda i: (i // nt, layer, 0, i % nt)))
        in_specs.append(pl.BlockSpec(memory_space=pl.ANY))
        operands.append(kvt)
        aliases = {len(operands) - 1: len(out_shape) - 1}
    return pl.pallas_call(
        functools.partial(_inproj_kernel, A_DIM ** -0.5 * (LOG2_E if attn else 1.0), len(operands)),
        grid=(M // tm,),
        in_specs=in_specs,
        out_specs=out_specs,
        out_shape=out_shape,
        input_output_aliases=aliases,
        compiler_params=_params("parallel"),
        name="inproj",
    )(*operands)


def _mlstm_kernel(q_ref, k_ref, v_ref, og_ref, misc_ref, gt_ref, mg_ref, c0_ref, n0_ref, m0_ref,
                  hm_ref, c_ref, n_ref, m_ref):
    L = q_ref.shape[1]

    @pl.when(pl.program_id(1) == 0)
    def _():
        c_ref[...] = c0_ref[...]
        n_ref[...] = n0_ref[...]
        m_ref[...] = m0_ref[...]

    gc = misc_ref[0]
    gr = gt_ref[0]
    row = lax.broadcasted_iota(jnp.int32, (L, L), 0)
    col = lax.broadcasted_iota(jnp.int32, (L, L), 1)
    causal = row >= col
    b_col = _dot_exact(causal.astype(F32), gc)
    b_row = _dot_exact(gr, (row <= col).astype(F32))
    for h in range(M_HEADS):
        sl = slice(h * M_DIM, (h + 1) * M_DIM)
        q, k, v = q_ref[0, :, sl], k_ref[0, :, sl], v_ref[0, :, sl]
        c_prev, n_prev, m_prev = c_ref[0, h], n_ref[0, h:h + 1, :], m_ref[0, h:h + 1, 0:1]
        bc = b_col[:, M_HEADS + h:M_HEADS + h + 1]
        ic = gc[:, h:h + 1]
        br = b_row[M_HEADS + h:M_HEADS + h + 1, :]
        ir = gr[h:h + 1, :]
        d = jnp.where(causal, bc - br + ir, NEG)
        m_inter = bc + m_prev
        m_t = jnp.maximum(m_inter, jnp.max(d, axis=-1, keepdims=True))
        w_inter = jnp.exp(m_inter - m_t)
        p = _dot_nt(q, k) * jnp.exp(d - m_t)
        num = w_inter * _dot(q, c_prev.astype(BF16)) + _dot(p.astype(BF16), v)
        den = (w_inter * jnp.sum(q.astype(F32) * n_prev, axis=-1, keepdims=True)
               + jnp.sum(p, axis=-1, keepdims=True))
        hh = num / jnp.maximum(jnp.abs(den), jnp.exp(-m_t))
        m_new = m_t[L - 1:L, :]
        w_new = jnp.exp(bc[L - 1:L, :] - bc + ic - m_new)
        decay = jnp.exp(m_inter[L - 1:L, :] - m_new)
        kf, vf = k.astype(F32), v.astype(F32)
        c_ref[0, h] = decay * c_prev + lax.dot_general(k, (w_new * vf).astype(BF16), _TN,
                                                       preferred_element_type=F32)
        n_ref[0, h:h + 1, :] = decay * n_prev + jnp.sum(w_new * kf, axis=0, keepdims=True)
        m_ref[0, h:h + 1, :] = jnp.broadcast_to(m_new, (1, M_DIM))
        hn = hh * lax.rsqrt(jnp.mean(hh * hh, axis=-1, keepdims=True) + EPS)
        hm_ref[0, :, sl] = ((hn * mg_ref[:, sl]) * og_ref[0, :, sl]).astype(BF16)


def _mlstm(qkv, og, misc, gt, mg, c0, n0, m0, L):
    B, T, _ = qkv.shape
    seq = lambda j: pl.BlockSpec((1, L, M_WIDTH), lambda b, c: (b, c, j))
    st4 = pl.BlockSpec((1, M_HEADS, M_DIM, M_DIM), lambda b, c: (b, 0, 0, 0))
    st3 = pl.BlockSpec((1, M_HEADS, M_DIM), lambda b, c: (b, 0, 0))
    return pl.pallas_call(
        _mlstm_kernel,
        grid=(B, T // L),
        in_specs=[seq(0), seq(1), seq(2), seq(0),
                  pl.BlockSpec((1, L, LANES), lambda b, c: (b, c, 0)),
                  pl.BlockSpec((1, 2 * M_HEADS, L), lambda b, c: (b, 0, c)),
                  pl.BlockSpec((1, M_WIDTH), lambda b, c: (0, 0)),
                  st4, st3, st3],
        out_specs=(seq(0), st4, st3, st3),
        out_shape=(jax.ShapeDtypeStruct((B, T, M_WIDTH), BF16),
                   jax.ShapeDtypeStruct(c0.shape, F32),
                   jax.ShapeDtypeStruct(n0.shape, F32),
                   jax.ShapeDtypeStruct(m0.shape, F32)),
        compiler_params=_params("parallel", "arbitrary"),
        name="mlstm",
    )(qkv, qkv, qkv, og, misc, gt, mg, c0, n0, m0)


def _cmp_accumulate(load, w_ref, s):
    acc = None
    for r in range(0, D_CMP, 2):
        lhs = jnp.concatenate([load(r), load(r + 1)], axis=1).astype(BF16)
        term = _dot(lhs, w_ref[s, r // 2])
        acc = term if acc is None else acc + term
    return acc


def _cmp_prompt_kernel(k_ref, v_ref, w_ref, fs_ref):
    n = k_ref.shape[0] // D_CMP
    for s, ref in enumerate((k_ref, v_ref)):
        fs_ref[s] = _cmp_accumulate(lambda r: ref[pl.ds(r, n, stride=D_CMP), :], w_ref, s)


def _cmp_prompt(rows, w_cmp, tr):
    M = rows.shape[0]
    return pl.pallas_call(
        _cmp_prompt_kernel,
        grid=(M // tr,),
        in_specs=[pl.BlockSpec((tr, KV_WIDTH), lambda i: (i, 0)), pl.BlockSpec((tr, KV_WIDTH), lambda i: (i, 1)),
                  pl.BlockSpec(w_cmp.shape, lambda i: (0, 0, 0, 0))],
        out_specs=pl.BlockSpec((2, tr // D_CMP, 2 * LANES), lambda i: (0, i, 0)),
        out_shape=jax.ShapeDtypeStruct((2, M // D_CMP, 2 * LANES), F32),
        compiler_params=_params("parallel"),
        name="cmp_prompt",
    )(rows, rows, w_cmp)


def _cmp_sample_kernel(n_group, pt_ref, *refs):
    page_refs, w_ref, fs_ref, rows_s = refs[:n_group], refs[n_group], refs[n_group + 1], refs[n_group + 2:]
    page = page_refs[0].shape[2]
    n = n_group * page // D_CMP
    for s in range(2):
        for i, p in enumerate(page_refs):
            rows_s[s][i * page:(i + 1) * page, :] = p[s].T
    for s in range(2):
        fs_ref[s, 0] = _cmp_accumulate(lambda r: rows_s[s][pl.ds(r, n, stride=D_CMP), :], w_ref, s)


def _cmp_sample(cache_t, page_table, layer, w_cmp, n_group):
    B, n_pages = page_table.shape
    page = cache_t.shape[4]
    per_page = page // D_CMP

    def page_spec(i):
        return pl.BlockSpec((None, None, 2, KV_WIDTH, page),
                            lambda b, p, pt: (pt[b, p * n_group + i], layer, 0, 0, 0))

    grid_spec = pltpu.PrefetchScalarGridSpec(
        num_scalar_prefetch=1,
        grid=(B, n_pages // n_group),
        in_specs=[page_spec(i) for i in range(n_group)]
        + [pl.BlockSpec(w_cmp.shape, lambda b, p, pt: (0, 0, 0, 0))],
        out_specs=pl.BlockSpec((2, 1, n_group * per_page, 2 * LANES), lambda b, p, pt: (0, b, p, 0)),
        scratch_shapes=[pltpu.VMEM((n_group * page, KV_WIDTH), F32), pltpu.VMEM((n_group * page, KV_WIDTH), F32)],
    )
    return pl.pallas_call(
        functools.partial(_cmp_sample_kernel, n_group),
        grid_spec=grid_spec,
        out_shape=jax.ShapeDtypeStruct((2, B, n_pages * per_page, 2 * LANES), F32),
        compiler_params=_params("parallel", "arbitrary"),
        name="cmp_sample",
    )(page_table, *([cache_t] * n_group), w_cmp)


def _compressed_kv(fs_ref, bias_ref, tabs, b=0):
    n = fs_ref.shape[2]
    out = []
    for s in range(2):
        fs = fs_ref[s, b]
        out.append(fs[:, 0:LANES] + pltpu.roll(fs[:, LANES:2 * LANES], n - 1, 0) + bias_ref[s:s + 1, :])
    return _rope128(out[0], *tabs), out[1]


def _masked_softmax(s, valid, axis):
    m = jnp.max(jnp.where(valid, s, NEG), axis=axis, keepdims=True)
    e = jnp.where(valid, jnp.exp(s - m), 0.0)
    l = jnp.sum(e, axis=axis, keepdims=True)
    return e * (1.0 / jnp.where(l > 0.0, l, 1.0))


def _bias_softmax(s, axis, keep=None):
    m = jnp.max(s, axis=axis, keepdims=True)
    e = jnp.exp2(s - m)
    scale = 1.0 / jnp.sum(e, axis=axis, keepdims=True)
    return e * (scale if keep is None else scale * keep)


def _split3(x):
    h1 = x.astype(BF16)
    r1 = x - h1.astype(F32)
    h2 = r1.astype(BF16)
    return h1, h2, (r1 - h2.astype(F32)).astype(BF16)


def _dot_split3(a, x):
    h1, h2, h3 = _split3(x)
    return _dot(a, h1) + _dot(a, h2) + _dot(a, h3)


def _dot_split3_rhs(x, a):
    h1, h2, h3 = _split3(x)
    return _dot(h1, a) + _dot(h2, a) + _dot(h3, a)


def _top_blocks(score, n_top, axis):
    pos = lax.broadcasted_iota(jnp.int32, score.shape, axis).astype(F32)
    picks = []
    for _ in range(n_top):
        mx = jnp.max(score, axis=axis, keepdims=True)
        idx = jnp.min(jnp.where(score == mx, pos, float(score.shape[axis])), axis=axis, keepdims=True)
        picks.append(idx)
        score = jnp.where(pos == idx, NEG, score)
    return picks, score


def _nsa_prompt_kernel(tk, q_ref, gt_ref, fs_ref, cb_ref, kc_c, kc_s1, kc_s2, ovt_ref,
                       ks_ref, kw_ref, vst_ref, vwt_ref, oa_ref,
                       kc_s, vct_s, sel_s, m_s, acc_s, sc_s, part_s, oc_s, imp_s):
    n_cmp = fs_ref.shape[2]
    n_sel = ovt_ref.shape[0]
    nq = Q_BLOCK
    cols = A_REP * nq
    per_tile = tk // L_SEL
    qi = pl.program_id(1)
    q0 = qi * nq

    @pl.when(qi == 0)
    def _():
        kc, vc = _compressed_kv(fs_ref, cb_ref, (kc_c[...], kc_s1[...], kc_s2[...]))
        kc_s[...] = kc.astype(BF16)
        vct_s[...] = vc.T.astype(BF16)

    t_pos = q0 + lax.broadcasted_iota(jnp.int32, (1, nq), 1)
    gates = gt_ref[0]

    def heads(a):
        return jnp.concatenate([a] * A_REP, axis=1)

    def all_heads(a):
        return jnp.concatenate([a] * A_HEADS, axis=1)

    cmp_end = lax.broadcasted_iota(jnp.int32, (n_cmp, 1), 0) * D_CMP + (L_CMP - 1)
    c_bias = jnp.where(cmp_end <= t_pos, 0.0, NEG)
    c_keep = jnp.where(t_pos >= L_CMP - 1, 1.0, 0.0)
    blk = lax.broadcasted_iota(jnp.int32, (n_sel, 1), 0)
    t_both = jnp.concatenate([t_pos] * A_KV, axis=1)
    cur = t_both // L_SEL
    forced = (blk == 0) | (blk == cur) | (blk == cur - 1)
    started = blk * L_SEL <= t_both
    w_start = jnp.maximum(q0 - WINDOW, 0)
    w_len = WINDOW + nq
    rel = t_pos - (w_start + lax.broadcasted_iota(jnp.int32, (w_len, 1), 0))
    w_bias = jnp.where((rel >= 0) & (rel < WINDOW), 0.0, NEG)
    n_tiles = (q0 + nq + tk - 1) // tk
    key_iota = lax.broadcasted_iota(jnp.int32, (tk, 1), 0)

    groups = range(A_KV)
    vs = [slice(g * A_DIM, (g + 1) * A_DIM) for g in groups]
    gs = [slice(g * cols, (g + 1) * cols) for g in groups]
    q_all = q_ref[...].reshape(A_HEADS * nq, LANES)

    n_cls = n_cmp // LANES if n_cmp % LANES == 0 else 1
    unit = n_cmp // n_cls
    cls = jnp.minimum(((q0 + nq) // D_CMP + unit - 1) // unit, n_cls)

    def compressed(rows):
        p_c = _bias_softmax(_dot_nt(kc_s[0:rows, :], q_all) + all_heads(c_bias[0:rows]), 0, all_heads(c_keep))
        p_c16 = p_c.astype(BF16)
        p_sums = []
        for g in groups:
            oc_s[g] = _dot(vct_s[vs[g], 0:rows], p_c16[:, gs[g]])
            p_sum = p_c[:, g * cols:g * cols + nq]
            for r in range(1, A_REP):
                p_sum = p_sum + p_c[:, g * cols + r * nq:g * cols + (r + 1) * nq]
            p_sums.append(p_sum)
        imp_s[...] = _dot_split3(ovt_ref[:, 0:rows], jnp.concatenate(p_sums, axis=1))

    for k in range(1, n_cls + 1):
        pl.when(cls == k)(functools.partial(compressed, k * unit))
    o_c = [oc_s[g] for g in groups]

    score = jnp.where(forced, FORCE_SCORE, jnp.where(started, imp_s[...], -1.0))
    _, left_all = _top_blocks(score, min(N_TOP, n_sel), 0)
    for g in groups:
        left = left_all[:, g * nq:(g + 1) * nq]
        sel_s[g] = jnp.where(left < -2.0, 0.0, NEG)

    ws = pl.multiple_of(w_start, nq)
    p_w = _bias_softmax(_dot_nt(kw_ref[0, pl.ds(ws, w_len), :], q_all) + all_heads(w_bias), 0).astype(BF16)
    wc = w_start // LANES
    vw_t = jnp.concatenate([vwt_ref[0, wc + c] for c in range(w_len // LANES)], axis=1)
    for g in groups:
        o_w = _dot(vw_t[vs[g], :], p_w[:, gs[g]])
        for r in range(A_REP):
            hd = g * A_REP + r
            base = 2 * M_HEADS + hd * N_BRANCH
            cs = slice(r * nq, (r + 1) * nq)
            part_s[hd * A_DIM:(hd + 1) * A_DIM, :] = (gates[base:base + 1] * o_c[g][:, cs]
                                                      + gates[base + 2:base + 3] * o_w[:, cs])

    m_s[...] = jnp.full(m_s.shape, NEG, F32)
    acc_s[...] = jnp.zeros(acc_s.shape, F32)
    ones_rows = jnp.ones((acc_s.shape[1] - A_DIM, tk), BF16)

    last_tile = ks_ref.shape[1] // tk - 1

    def scores(kt):
        k0 = pl.multiple_of(jnp.minimum(kt, last_tile) * tk, tk)
        return _dot_nt(ks_ref[0, pl.ds(k0, tk), :], q_all)

    def attend(kt, slot):
        kc = jnp.minimum(kt, last_tile)
        causal = kt * tk + key_iota <= t_pos
        bias = []
        for g in groups:
            chosen = jnp.concatenate(
                [jnp.broadcast_to(sel_s[g, pl.ds(kc * per_tile + i, 1), :], (L_SEL, nq)) for i in range(per_tile)],
                axis=0)
            bias.append(heads(jnp.where(causal, chosen, NEG)))
        s = sc_s[slot] + jnp.concatenate(bias, axis=1)
        m_old = m_s[...]
        m_new = jnp.maximum(m_old, jnp.max(s, axis=0, keepdims=True))
        e = jnp.exp2(s - m_new)
        a = jnp.exp2(m_old - m_new)
        e16 = e.astype(BF16)
        for g in groups:
            vals = jnp.concatenate([vst_ref[0, kc, vs[g], :], ones_rows], axis=0)
            acc_s[g] = a[:, gs[g]] * acc_s[g] + _dot(vals, e16[:, gs[g]])
        m_s[...] = m_new

    sc_s[0] = scores(0)

    def pair(j, carry):
        sc_s[1] = scores(2 * j + 1)
        attend(2 * j, 0)
        sc_s[0] = scores(2 * j + 2)
        attend(2 * j + 1, 1)
        return carry

    lax.fori_loop(0, n_tiles // 2, pair, 0)

    @pl.when(n_tiles % 2 == 1)
    def _():
        attend(n_tiles - 1, 0)

    heads_out = []
    for g in groups:
        l = acc_s[g, A_DIM:A_DIM + 1, :]
        o_s = acc_s[g, 0:A_DIM, :] * (1.0 / jnp.where(l > 0.0, l, 1.0))
        for r in range(A_REP):
            hd = g * A_REP + r
            gate = gates[2 * M_HEADS + hd * N_BRANCH + 1:2 * M_HEADS + hd * N_BRANCH + 2]
            heads_out.append(part_s[hd * A_DIM:(hd + 1) * A_DIM, :] + gate * o_s[:, r * nq:(r + 1) * nq])
    oa_ref[0] = jnp.concatenate(heads_out, axis=0).T.astype(BF16)


def _nsa_prompt(q8, gt, fs, cbias, kc_tabs, ovt, kk, vst, vwt, B, T, tk):
    nqb = T // Q_BLOCK
    n_cmp = fs.shape[2]
    n_sel = ovt.shape[0]
    cols = A_REP * Q_BLOCK
    const = lambda a: pl.BlockSpec(a.shape, lambda b, i: (0,) * a.ndim)
    keys = lambda j: pl.BlockSpec((1, T, LANES), lambda b, i: (b, 0, j))
    per_b = lambda a: pl.BlockSpec((1,) + a.shape[1:], lambda b, i: (b, 0, 0, 0))
    return pl.pallas_call(
        functools.partial(_nsa_prompt_kernel, tk),
        grid=(B, nqb),
        in_specs=[pl.BlockSpec((A_HEADS, Q_BLOCK, LANES), lambda b, i: (0, b * nqb + i, 0)),
                  pl.BlockSpec((1, gt.shape[1], Q_BLOCK), lambda b, i: (b, 0, i)),
                  pl.BlockSpec((2, 1, n_cmp, 2 * LANES), lambda b, i: (0, b, 0, 0)),
                  const(cbias), const(kc_tabs[0]), const(kc_tabs[1]), const(kc_tabs[2]),
                  const(ovt), keys(0), keys(1), per_b(vst), per_b(vwt)],
        out_specs=pl.BlockSpec((1, Q_BLOCK, A_WIDTH), lambda b, i: (b, i, 0)),
        out_shape=jax.ShapeDtypeStruct((B, T, A_WIDTH), BF16),
        scratch_shapes=[pltpu.VMEM((n_cmp, LANES), BF16), pltpu.VMEM((LANES, n_cmp), BF16),
                        pltpu.VMEM((A_KV, n_sel, Q_BLOCK), F32),
                        pltpu.VMEM((1, A_KV * cols), F32),
                        pltpu.VMEM((A_KV, A_DIM + 16, cols), F32),
                        pltpu.VMEM((2, tk, A_KV * cols), F32),
                        pltpu.VMEM((A_WIDTH, Q_BLOCK), F32),
                        pltpu.VMEM((A_KV, A_DIM, cols), F32),
                        pltpu.VMEM((n_sel, A_KV * Q_BLOCK), F32)],
        compiler_params=_params("parallel", "arbitrary"),
        name="nsa_prompt",
    )(q8, gt, fs, cbias, *kc_tabs, ovt, kk, kk, vst, vwt)


def _nsa_sample_select_kernel(q_pos, n_sel, q_ref, fs_ref, cb_ref, kc_c, kc_s1, kc_s2, ov_ref, oc_ref, idx_ref):
    n_seq, n_cmp = fs_ref.shape[1], fs_ref.shape[2]
    tabs = (kc_c[...], kc_s1[...], kc_s2[...])
    cmp_end = lax.broadcasted_iota(jnp.int32, (A_HEADS, n_cmp), 1) * D_CMP + (L_CMP - 1)
    head = lax.broadcasted_iota(jnp.int32, (A_HEADS, n_cmp), 0)
    imps = []
    for b in range(n_seq):
        kc, vc = _compressed_kv(fs_ref, cb_ref, tabs, b)
        p_c = _masked_softmax(_dot_nt(q_ref[b], kc.astype(BF16)), cmp_end <= q_pos, 1)
        oc_ref[b] = _dot(p_c.astype(BF16), vc.astype(BF16))
        p_sum = jnp.zeros((A_HEADS, n_cmp), F32)
        for g in range(A_KV):
            pg = jnp.sum(jnp.where(head // A_REP == g, p_c, 0.0), axis=0, keepdims=True)
            p_sum = jnp.where(head == g, pg, p_sum)
        imps.append(_dot_split3_rhs(p_sum, ov_ref[...]))
    imp = jnp.concatenate(imps, axis=0)
    n_pad = ov_ref.shape[1]
    blk = lax.broadcasted_iota(jnp.int32, (A_HEADS * n_seq, n_pad), 1)
    cur = q_pos // L_SEL
    forced = (blk == 0) | (blk == cur) | (blk == cur - 1)
    score = jnp.where(forced, FORCE_SCORE, jnp.where(blk * L_SEL <= q_pos, imp, -1.0))
    score = jnp.where(blk < n_sel, score, 2 * NEG)
    picks, _ = _top_blocks(score, min(N_TOP, n_sel), 1)
    lane = lax.broadcasted_iota(jnp.int32, (A_HEADS * n_seq, LANES), 1)
    out = jnp.zeros((A_HEADS * n_seq, LANES), jnp.int32)
    for j, idx in enumerate(picks):
        out = jnp.where(lane == j, idx.astype(jnp.int32), out)
    for b in range(n_seq):
        idx_ref[b] = out[b * A_HEADS:(b + 1) * A_HEADS]


def _nsa_sample_select(q8, fs, cbias, kc_tabs, ov, q_pos, n_sel, n_seq):
    B = q8.shape[0]
    n_cmp = fs.shape[2]
    const = lambda a: pl.BlockSpec(a.shape, lambda b: (0,) * a.ndim)
    per_b = pl.BlockSpec((n_seq, A_HEADS, LANES), lambda b: (b, 0, 0))
    return pl.pallas_call(
        functools.partial(_nsa_sample_select_kernel, q_pos, n_sel),
        grid=(B // n_seq,),
        in_specs=[per_b, pl.BlockSpec((2, n_seq, n_cmp, 2 * LANES), lambda b: (0, b, 0, 0)),
                  const(cbias), const(kc_tabs[0]), const(kc_tabs[1]), const(kc_tabs[2]), const(ov)],
        out_specs=(per_b, per_b),
        out_shape=(jax.ShapeDtypeStruct((B, A_HEADS, LANES), F32),
                   jax.ShapeDtypeStruct((B, A_HEADS, LANES), jnp.int32)),
        compiler_params=_params("parallel"),
        name="nsa_sample_select",
    )(q8, fs, cbias, *kc_tabs, ov)


def _r16(a):
    return a.astype(BF16).astype(F32)


def _nsa_sample_attend_kernel(q_pos, n_top, pt_ref, ix_ref, q_ref, new_ref, newc_ref, oc_ref, misc_ref, win_ref,
                              *refs):
    page_refs, (oa_ref, nwin_ref) = refs[:A_KV * n_top], refs[A_KV * n_top:]
    b = pl.program_id(0)
    q = q_ref[0]
    qf = q.astype(F32)
    head = lax.broadcasted_iota(jnp.int32, (A_HEADS, 1), 0)
    new = new_ref[0]
    page = page_refs[0].shape[2]
    per_page = page // L_SEL
    lane = lax.broadcasted_iota(jnp.int32, (A_HEADS, page), 1)
    s_tok = jnp.sum(qf * _r16(new[0:1, :]), axis=-1, keepdims=True)

    o_s = jnp.zeros((A_HEADS, LANES), F32)
    for g in range(A_KV):
        parts, n_fresh = [], 0
        for j in range(n_top):
            blk = ix_ref[b, g, j]
            fresh = blk * L_SEL >= q_pos
            limit = jnp.where(fresh, -1, q_pos)
            s = _dot(q, page_refs[g * n_top + j][0].astype(BF16))
            s_pos = (blk // per_page) * page + lane
            ok = jnp.where(lane // L_SEL == blk % per_page, s_pos, limit + 1) <= limit
            parts.append(jnp.where(ok, s, NEG))
            n_fresh = n_fresh + jnp.where(fresh, 1, 0)
        s_all = jnp.concatenate(parts, axis=1)
        has_tok = jnp.where(head >= 0, n_fresh, 0) > 0
        m = jnp.maximum(jnp.max(s_all, axis=-1, keepdims=True), jnp.where(has_tok, s_tok, NEG))
        e = jnp.exp(s_all - m)
        e_tok = jnp.where(has_tok, jnp.exp(s_tok - m), 0.0)
        den = jnp.sum(e, axis=-1, keepdims=True) + e_tok
        e16 = e.astype(BF16)
        acc = _r16(e_tok) * _r16(new[1:2, :])
        for j in range(n_top):
            acc = acc + _dot_nt(e16[:, j * page:(j + 1) * page], page_refs[g * n_top + j][1].astype(BF16))
        o_s = jnp.where(head // A_REP == g, acc / den, o_s)

    wb = win_ref.shape[2]
    kwt, vwt = win_ref[0], win_ref[1]
    s_old = _dot(q, kwt.astype(BF16))
    s_new = jnp.sum(qf * _r16(new[2:3, :]), axis=-1, keepdims=True)
    rel = wb - lax.broadcasted_iota(jnp.int32, (A_HEADS, wb), 1)
    ok = (rel >= 0) & (rel < WINDOW) & (q_pos - rel >= 0)
    m = jnp.maximum(jnp.max(jnp.where(ok, s_old, NEG), axis=-1, keepdims=True), s_new)
    e_old = jnp.where(ok, jnp.exp(s_old - m), 0.0)
    e_new = jnp.exp(s_new - m)
    den = jnp.sum(e_old, axis=-1, keepdims=True) + e_new
    o_w = _dot_nt((e_old / den).astype(BF16), vwt.astype(BF16)) + _r16(e_new / den) * _r16(new[3:4, :])
    gates = misc_ref[0]
    oa_ref[0] = gates[:, 0:1] * oc_ref[0] + gates[:, 1:2] * o_s + gates[:, 2:3] * o_w
    newc = newc_ref[0]
    last = lax.broadcasted_iota(jnp.int32, (LANES, wb), 1) == wb - 1
    nwin_ref[0] = jnp.where(last, newc[:, 2:3], pltpu.roll(kwt, wb - 1, 1))
    nwin_ref[1] = jnp.where(last, newc[:, 3:4], pltpu.roll(vwt, wb - 1, 1))


def _nsa_sample_attend(page_table, idx, q8, new_rows, new_cols, o_c, gate_rows, win_t, cache_t, layer, q_pos):
    B, n_pages = page_table.shape
    n_top = idx.shape[2]
    page = cache_t.shape[4]
    per_page = page // L_SEL
    wb = win_t.shape[4]

    def page_spec(g, j):
        def index(b, pt, ix):
            pg = jnp.clip(ix[b, g, j] // per_page, 0, n_pages - 1)
            return (pt[b, pg], layer, 1, 0, 0)
        return pl.BlockSpec((None, None, 2, KV_WIDTH, page), index)

    per_b = pl.BlockSpec((1, A_HEADS, LANES), lambda b, pt, ix: (b, 0, 0))
    pages = [page_spec(g, j) for g in range(A_KV) for j in range(n_top)]
    grid_spec = pltpu.PrefetchScalarGridSpec(
        num_scalar_prefetch=2,
        grid=(B,),
        in_specs=[per_b, per_b, pl.BlockSpec((1, KV_WIDTH, A_HEADS), lambda b, pt, ix: (b, 0, 0)), per_b, per_b,
                  pl.BlockSpec((None, None, 2, KV_WIDTH, wb), lambda b, pt, ix: (b, layer, 0, 0, 0))] + pages,
        out_specs=(per_b, pl.BlockSpec((None, 2, KV_WIDTH, wb), lambda b, pt, ix: (b, 0, 0, 0))),
    )
    return pl.pallas_call(
        functools.partial(_nsa_sample_attend_kernel, q_pos, n_top),
        grid_spec=grid_spec,
        out_shape=(jax.ShapeDtypeStruct((B, A_HEADS, LANES), F32),
                   jax.ShapeDtypeStruct((B, 2, KV_WIDTH, wb), F32)),
        compiler_params=_params("parallel"),
        name="nsa_sample_attend",
    )(page_table, idx, q8, new_rows, new_cols, o_c, gate_rows, win_t, *([cache_t] * len(pages)))


def _ffn_kernel(final, tf, x_ref, hm_ref, oa_ref, wo_ref, nf_ref, wg_ref, wu_ref, wd_ref, nfin_ref, out_ref, act_s):
    half = hm_ref.shape[1]
    x1 = x_ref[...] + _dot(hm_ref[...], wo_ref[0:half, :]) + _dot(oa_ref[...], wo_ref[half:, :])
    ms = jnp.mean(x1 * x1, axis=-1, keepdims=True)
    xn = ((x1 * lax.rsqrt(ms + EPS)) * nf_ref[...]).astype(BF16)
    for c in range(wg_ref.shape[1] // tf):
        sl = slice(c * tf, (c + 1) * tf)
        gate = _dot(xn, wg_ref[:, sl])
        act_s[:, sl] = ((gate * _sigmoid(gate)) * _dot(xn, wu_ref[:, sl])).astype(BF16)
    y = x1 + _dot(act_s[...], wd_ref[...])
    if final:
        ms = jnp.mean(y * y, axis=-1, keepdims=True)
        y = (y * lax.rsqrt(ms + EPS)) * nfin_ref[...]
    out_ref[...] = y


def _ffn(x, hm, oa, wo, nf, wg, wu, wd, nfin, final, tm, tf):
    M, D = x.shape
    F = wg.shape[1]
    row = lambda w: pl.BlockSpec((tm, w), lambda i: (i, 0))
    const = lambda a: pl.BlockSpec(a.shape, lambda i: (0,) * a.ndim, pipeline_mode=pl.Buffered(1))
    return pl.pallas_call(
        functools.partial(_ffn_kernel, final, tf),
        grid=(M // tm,),
        in_specs=[row(D), row(hm.shape[1]), row(oa.shape[1]), const(wo), const(nf),
                  const(wg), const(wu), const(wd), const(nfin)],
        out_specs=row(D),
        out_shape=jax.ShapeDtypeStruct((M, D), F32),
        scratch_shapes=[pltpu.VMEM((tm, F), BF16)],
        compiler_params=_params("parallel"),
        name="ffn",
    )(x, hm, oa, wo, nf, wg, wu, wd, nfin)


def _permute_w_in(w):
    a = 4 * M_WIDTH
    b = a + 2 * M_HEADS
    c = b + A_WIDTH + 6 * KV_WIDTH
    pad = jnp.zeros((w.shape[0], LANES - N_MISC), w.dtype)
    return jnp.concatenate([w[:, :a], w[:, b:c], w[:, a:b], w[:, c:], pad], axis=1).astype(BF16)


def _cmp_weights(wk, wv):
    def one(w):
        z = jnp.zeros((D_CMP, A_DIM, A_DIM), w.dtype)
        halves = []
        for part in (w[:D_CMP], w[D_CMP:]):
            top = jnp.concatenate([part, z], axis=2)
            bot = jnp.concatenate([z, part], axis=2)
            halves.append(jnp.concatenate([top, bot], axis=1))
        return jnp.concatenate(halves, axis=2)
    return jnp.stack([one(wk), one(wv)]).astype(BF16).reshape(2, D_CMP // 2, 2 * KV_WIDTH, 2 * LANES)


def _overlap(n_cmp_pad, n_sel_pad, n_cmp, n_sel):
    lo_c = np.arange(n_cmp_pad)[:, None] * D_CMP
    lo_s = np.arange(n_sel_pad)[None, :] * L_SEL
    ov = np.clip(np.minimum(lo_c + L_CMP, lo_s + L_SEL) - np.maximum(lo_c, lo_s), 0, None).astype(np.float32) / L_CMP
    ov[n_cmp:, :] = 0.0
    ov[:, n_sel:] = 0.0
    return jnp.asarray(ov)


def _pick(n, prefs):
    for p in prefs:
        if n % p == 0:
            return p
    return n


def kernel(x_prompt, x_sample, cache_nsa_kv, page_table, state_win_kv, state_mlstm_c, state_mlstm_n, state_mlstm_m, norm_mix, w_in, b_if, m_norm, w_cmp_k, b_cmp_k, w_cmp_v, b_cmp_v, w_out, norm_ffn, w_gate, w_up, w_down, norm_final):
    B, T, D = x_prompt.shape
    Bs, Ts, _ = x_sample.shape
    depth = w_in.shape[0]
    n_pool, _, n_slots, page = cache_nsa_kv.shape[:4]
    n_pages = page_table.shape[1]
    past = n_pages * page
    wb = state_win_kv.shape[3]
    assert Ts == 1 and T % Q_BLOCK == 0 and T >= WINDOW + Q_BLOCK and wb == WINDOW
    assert past % page == 0 and page % L_SEL == 0 and page % D_CMP == 0
    assert ((past + Ts) // D_CMP) * D_CMP <= past

    Mp = B * T
    tm_p = _pick(Mp, (512, 256, 128))
    tm_f = _pick(Mp, (512, 256, 128))
    ff = w_gate.shape[2]
    tf = _pick(ff, (256, 128))
    l_chunk = _pick(T, (256, 128, 64))
    ls_pad = 16
    tk = _pick(T, (512, 256, 128))
    assert tk % tm_p == 0 and tm_p % LANES == 0
    n_group = _pick(n_pages, (16, 8, 4, 2, 1))
    tr = _pick(T, (2048, 1024, 512, 256))

    n_chunk_p = T // D_CMP
    n_cmp_p, n_sel_p = n_chunk_p - 1, -(-T // L_SEL)
    tabs_p = _rope_tables(jnp.arange(T))
    kc_tabs_p = _rope_tables(jnp.arange(n_chunk_p) * D_CMP)
    assert L_CMP <= 256
    ovt_p = _overlap(n_chunk_p, n_sel_p, n_cmp_p, n_sel_p).T.astype(BF16)
    n_chunk_s = (past + Ts) // D_CMP
    n_cmp_s, n_sel_s = n_chunk_s - 1, -(-(past + Ts) // L_SEL)
    n_sel_pad = -(-n_sel_s // LANES) * LANES
    tabs_s = _rope_tables(jnp.full((Bs,), past))
    kc_tabs_s = _rope_tables(jnp.arange(n_chunk_s) * D_CMP)
    ov_s = _overlap(n_chunk_s, n_sel_pad, n_cmp_s, n_sel_s).astype(BF16)
    cache_t = cache_nsa_kv.transpose(0, 1, 2, 4, 5, 3).reshape(n_pool, depth, n_slots, KV_WIDTH, page)
    win_t = state_win_kv.transpose(0, 1, 2, 4, 5, 3).reshape(Bs, depth, 2, KV_WIDTH, wb)

    xp = x_prompt.reshape(Mp, D)
    xs = x_sample.reshape(Bs, D)
    rows_s, win_p, win_s, c_p, c_s, n_p, n_s, m_p, m_s = ([] for _ in range(9))
    y_p = y_s = None
    kvt = jnp.zeros((B, depth, n_slots * KV_WIDTH, T), F32)
    nfin = norm_final.reshape(1, D)
    for l in range(depth):
        w_perm = _permute_w_in(w_in[l])
        gain = norm_mix[l].reshape(1, D)
        bias = jnp.zeros((1, LANES), F32).at[0, :2 * M_HEADS].set(b_if[l].reshape(-1))
        mg = m_norm[l].reshape(1, M_WIDTH)
        w_cmp = _cmp_weights(w_cmp_k[l], w_cmp_v[l])
        cbias = jnp.stack([jnp.tile(b_cmp_k[l], A_KV), jnp.tile(b_cmp_v[l], A_KV)])
        wo, nf = w_out[l].astype(BF16), norm_ffn[l].reshape(1, D)
        wg, wu, wd = w_gate[l].astype(BF16), w_up[l].astype(BF16), w_down[l].astype(BF16)
        final = l == depth - 1

        qkv, og, q8, rows, win, misc, kk, vst, vwt, kvt = _inproj(xp, gain, w_perm, bias, tabs_p, tm_p,
                                                                    (tk, depth, l, kvt))
        misc3 = misc.reshape(B, T, LANES)
        gt = misc3[:, :, :N_MISC].transpose(0, 2, 1)
        hm, c_new, n_new, m_new = _mlstm(
            qkv.reshape(B, T, -1), og.reshape(B, T, -1), misc3, gt, mg,
            jnp.zeros((B, M_HEADS, M_DIM, M_DIM), F32), jnp.zeros((B, M_HEADS, M_DIM), F32),
            jnp.full((B, M_HEADS, M_DIM), NEG, F32), l_chunk)
        fs = _cmp_prompt(rows, w_cmp, tr).reshape(2, B, n_chunk_p, 2 * LANES)
        oa = _nsa_prompt(q8, gt, fs, cbias, kc_tabs_p, ovt_p, kk.reshape(B, T, -1),
                         vst.reshape(B, T // tk, KV_WIDTH, tk), vwt.reshape(B, T // LANES, KV_WIDTH, LANES), B, T, tk)
        xp_new = _ffn(xp, hm.reshape(Mp, -1), oa.reshape(Mp, -1), wo, nf, wg, wu, wd, nfin, final, tm_f, tf)
        if final:
            y_p = xp_new
        xp = xp_new
        win_last = win.reshape(B, T, 2 * KV_WIDTH)[:, T - wb:]
        win_p.append(win_last.reshape(B, wb, 2, A_KV, A_DIM).transpose(0, 2, 1, 3, 4))
        c_p.append(c_new); n_p.append(n_new); m_p.append(m_new[:, :, 0])

        qkv, og, q8, rows, win, misc = _inproj(xs, gain, w_perm, bias, tabs_s, Bs)
        pad_t = lambda a: jnp.pad(a[:, None, :], ((0, 0), (0, ls_pad - 1), (0, 0)))
        inert = jnp.zeros((ls_pad, LANES), F32).at[1:, :M_HEADS].set(NEG)
        misc_pad = pad_t(misc) + inert[None]
        gt = misc_pad[:, :, :2 * M_HEADS].transpose(0, 2, 1)
        m0 = jnp.broadcast_to(state_mlstm_m[:, l, :, None].astype(F32), (Bs, M_HEADS, M_DIM))
        hm, c_new, n_new, m_new = _mlstm(
            pad_t(qkv), pad_t(og), misc_pad, gt, mg,
            state_mlstm_c[:, l].astype(F32), state_mlstm_n[:, l].astype(F32), m0, ls_pad)
        hm = hm[:, 0]
        fs = _cmp_sample(cache_t, page_table, l, w_cmp, n_group)
        q8s = q8.transpose(1, 0, 2)
        o_c, idx = _nsa_sample_select(q8s, fs, cbias, kc_tabs_s, ov_s, past, n_sel_s, _pick(Bs, (4, 2, 1)))
        idx = idx[:, :A_KV, :min(N_TOP, n_sel_s)]
        new_rows = jnp.pad(jnp.stack([rows[:, 2 * LANES:3 * LANES], rows[:, 3 * LANES:], win[:, :LANES],
                                      win[:, LANES:]], axis=1), ((0, 0), (0, A_HEADS - 4), (0, 0)))
        gate_rows = jnp.pad(misc[:, 2 * M_HEADS:N_MISC].reshape(Bs, A_HEADS, N_BRANCH),
                            ((0, 0), (0, 0), (0, LANES - N_BRANCH)))
        oa8, nwin = _nsa_sample_attend(page_table, idx, q8s, new_rows, new_rows.transpose(0, 2, 1), o_c, gate_rows,
                                       win_t, cache_t, l, past)
        grp = (jnp.arange(A_HEADS) // A_REP)[None, :, None, None]
        oa = jnp.take_along_axis(oa8.reshape(Bs, A_HEADS, A_KV, A_DIM), jnp.broadcast_to(grp, (Bs, A_HEADS, 1, A_DIM)),
                                 axis=2).reshape(Bs, A_WIDTH).astype(BF16)
        xs_new = _ffn(xs, hm, oa, wo, nf, wg, wu, wd, nfin, final, Bs, tf)
        if final:
            y_s = xs_new
        xs = xs_new
        rows_s.append(rows.reshape(Bs, Ts, n_slots, A_KV, A_DIM))
        win_s.append(nwin.reshape(Bs, 2, A_KV, A_DIM, wb).transpose(0, 1, 4, 2, 3))
        c_s.append(c_new); n_s.append(n_new); m_s.append(m_new[:, :, 0])

    return (y_p.reshape(B, T, D), y_s.reshape(Bs, Ts, D),
            kvt.reshape(B, depth, n_slots, A_KV, A_DIM, T).transpose(0, 5, 1, 2, 3, 4), jnp.stack(rows_s, axis=2),
            jnp.stack(win_p, axis=1), jnp.stack(win_s, axis=1),
            jnp.stack(c_p, axis=1), jnp.stack(c_s, axis=1),
            jnp.stack(n_p, axis=1), jnp.stack(n_s, axis=1),
            jnp.stack(m_p, axis=1), jnp.stack(m_s, axis=1))
```

```python
import functools

import jax
import jax.numpy as jnp
import numpy as np
from jax import lax
from jax.experimental import pallas as pl
from jax.experimental.pallas import tpu as pltpu

F32 = jnp.float32
BF16 = jnp.bfloat16

M_HEADS = 4
M_DIM = 128
M_WIDTH = M_HEADS * M_DIM
A_HEADS = 8
A_DIM = 64
A_KV = 2
A_REP = A_HEADS // A_KV
A_WIDTH = A_HEADS * A_DIM
KV_WIDTH = A_KV * A_DIM
L_CMP = 32
D_CMP = 16
L_SEL = 64
N_TOP = 16
WINDOW = 512
Q_BLOCK = 128
N_BRANCH = 3
ROPE_THETA = 500000.0
ROPE_DIM = A_DIM // 4
EPS = 1e-6
NEG = -1e30
LOG2_E = 1.4426950408889634
FORCE_SCORE = 1e4
LANES = 128
N_MISC = 2 * M_HEADS + N_BRANCH * A_HEADS
D_PERM = 4 * M_WIDTH + A_WIDTH + 6 * KV_WIDTH + LANES
VMEM_LIMIT = 56 * 1024 * 1024

_NT = (((1,), (1,)), ((), ()))
_TN = (((0,), (0,)), ((), ()))


def _params(*sem):
    return pltpu.CompilerParams(dimension_semantics=sem, vmem_limit_bytes=VMEM_LIMIT)


def _sigmoid(x):
    return 1.0 / (1.0 + jnp.exp(-x))


def _dot(a, b):
    return jnp.dot(a, b, preferred_element_type=F32)


def _dot_exact(a, b):
    return jnp.dot(a, b, preferred_element_type=F32, precision=lax.Precision.HIGHEST)


def _dot_nt(a, b):
    return lax.dot_general(a, b, _NT, preferred_element_type=F32)


def _rope128(v, c, s1, s2):
    half = ROPE_DIM // 2
    return v * c + pltpu.roll(v, LANES - half, 1) * s1 + pltpu.roll(v, half, 1) * s2


def _rope_tables(pos):
    half = ROPE_DIM // 2
    inv = ROPE_THETA ** (-jnp.arange(half, dtype=F32) / half)
    ang = pos.astype(F32)[:, None] * inv[None, :]
    cos, sin = jnp.cos(ang), jnp.sin(ang)
    n = pos.shape[0]
    one = jnp.ones((n, A_DIM - ROPE_DIM), F32)
    zero = jnp.zeros((n, A_DIM - ROPE_DIM), F32)
    zh = jnp.zeros((n, half), F32)
    c = jnp.concatenate([cos, cos, one], axis=1)
    s1 = jnp.concatenate([-sin, zh, zero], axis=1)
    s2 = jnp.concatenate([zh, sin, zero], axis=1)
    tile = lambda a: jnp.concatenate([a, a], axis=1)
    return tile(c), tile(s1), tile(s2)


def _inproj_kernel(q_scale, n_in, *refs):
    x_ref, g_ref, w_ref, bias_ref, rc_ref, rs1_ref, rs2_ref = refs[:7]
    om_ref, og_ref, oq_ref, orows_ref, owin_ref, omisc_ref = refs[n_in:n_in + 6]
    attn_refs = refs[n_in + 6:]
    x = x_ref[...]
    ms = jnp.mean(x * x, axis=-1, keepdims=True)
    xn = ((x * lax.rsqrt(ms + EPS)) * g_ref[...]).astype(BF16)
    c, s1, s2 = rc_ref[...], rs1_ref[...], rs2_ref[...]
    lane = lax.broadcasted_iota(jnp.int32, (x.shape[0], LANES), 1)

    def proj(a, b):
        return _dot(xn, w_ref[:, a:b])

    om_ref[:, 0:M_WIDTH] = (proj(0, M_WIDTH) * (M_DIM ** -0.5)).astype(BF16)
    om_ref[:, M_WIDTH:3 * M_WIDTH] = proj(M_WIDTH, 3 * M_WIDTH).astype(BF16)
    og_ref[...] = _sigmoid(proj(3 * M_WIDTH, 4 * M_WIDTH))

    off = 4 * M_WIDTH
    pq = proj(off, off + A_WIDTH)
    for j in range(A_WIDTH // LANES):
        blk = _rope128(pq[:, j * LANES:(j + 1) * LANES], c, s1, s2) * q_scale
        swapped = pltpu.roll(blk, A_DIM, 1)
        for e in range(2):
            hd = 2 * j + e
            grp = hd // A_REP
            src = blk if e == grp else swapped
            keep = (lane // A_DIM) == grp
            oq_ref[hd] = jnp.where(keep, src, 0.0).astype(BF16)

    off += A_WIDTH
    pk = proj(off, off + 6 * KV_WIDTH)
    kcr, vcr, ksl, vsl, kw, vw = [pk[:, i * LANES:(i + 1) * LANES] for i in range(6)]
    ksl = _rope128(ksl, c, s1, s2)
    kw = _rope128(kw, c, s1, s2)
    owin_ref[:, 0:LANES] = kw
    owin_ref[:, LANES:2 * LANES] = vw
    if attn_refs:
        okk_ref, ovst_ref, ovwt_ref, okvt_ref = attn_refs
        orows_ref[:, 0:LANES] = kcr
        orows_ref[:, LANES:2 * LANES] = vcr
        vsl_t = vsl.T
        for i, a_t in enumerate((kcr.T, vcr.T, ksl.T, vsl_t)):
            okvt_ref[i * LANES:(i + 1) * LANES, :] = a_t
        okk_ref[:, 0:LANES] = ksl.astype(BF16)
        okk_ref[:, LANES:2 * LANES] = kw.astype(BF16)
        ovst_ref[0] = vsl_t.astype(BF16)
        for i in range(ovwt_ref.shape[0]):
            ovwt_ref[i] = vw[i * LANES:(i + 1) * LANES].T.astype(BF16)
    else:
        for i, a in enumerate((kcr, vcr, ksl, vsl)):
            orows_ref[:, i * LANES:(i + 1) * LANES] = a

    off += 6 * KV_WIDTH
    pm = proj(off, off + LANES) + bias_ref[...]
    log_sig = -(jnp.maximum(-pm, 0.0) + jnp.log(1.0 + jnp.exp(-jnp.abs(pm))))
    omisc_ref[...] = jnp.where(lane < M_HEADS, pm, jnp.where(lane < 2 * M_HEADS, log_sig, _sigmoid(pm)))


def _inproj(x, gain, w_perm, bias, tabs, tm, attn=None):
    M, D = x.shape
    nt = tabs[0].shape[0] // tm
    row = lambda i: (i, 0)
    tab = pl.BlockSpec((tm, LANES), lambda i: (i % nt, 0))
    full = lambda a: pl.BlockSpec(a.shape, lambda i: (0,) * a.ndim)
    out_shape = (
        jax.ShapeDtypeStruct((M, 3 * M_WIDTH), BF16),
        jax.ShapeDtypeStruct((M, M_WIDTH), F32),
        jax.ShapeDtypeStruct((A_HEADS, M, LANES), BF16),
        jax.ShapeDtypeStruct((M, (2 if attn else 4) * KV_WIDTH), F32),
        jax.ShapeDtypeStruct((M, 2 * KV_WIDTH), F32),
        jax.ShapeDtypeStruct((M, LANES), F32),
    )
    out_specs = (
        pl.BlockSpec((tm, 3 * M_WIDTH), row),
        pl.BlockSpec((tm, M_WIDTH), row),
        pl.BlockSpec((A_HEADS, tm, LANES), lambda i: (0, i, 0)),
        pl.BlockSpec((tm, (2 if attn else 4) * KV_WIDTH), row),
        pl.BlockSpec((tm, 2 * KV_WIDTH), row),
        pl.BlockSpec((tm, LANES), row),
    )
    in_specs = [pl.BlockSpec((tm, D), row), full(gain), full(w_perm), full(bias), tab, tab, tab]
    operands = [x, gain, w_perm, bias, *tabs]
    aliases = {}
    if attn:
        tk, depth, layer, kvt = attn
        per = tk // tm
        seq = tabs[0].shape[0]
        out_shape += (jax.ShapeDtypeStruct((M, 2 * KV_WIDTH), BF16),
                      jax.ShapeDtypeStruct((M // tk, KV_WIDTH, tk), BF16),
                      jax.ShapeDtypeStruct((M // LANES, KV_WIDTH, LANES), BF16),
                      jax.ShapeDtypeStruct((M // seq, depth, 4 * KV_WIDTH, seq), F32))
        out_specs += (pl.BlockSpec((tm, 2 * KV_WIDTH), row),
                      pl.BlockSpec((1, KV_WIDTH, tm), lambda i: (i // per, 0, i % per)),
                      pl.BlockSpec((tm // LANES, KV_WIDTH, LANES), lambda i: (i, 0, 0)),
                      pl.BlockSpec((None, None, 4 * KV_WIDTH, tm), lambda i: (i // nt, layer, 0, i % nt)))
        in_specs.append(pl.BlockSpec(memory_space=pl.ANY))
        operands.append(kvt)
        aliases = {len(operands) - 1: len(out_shape) - 1}
    return pl.pallas_call(
        functools.partial(_inproj_kernel, A_DIM ** -0.5 * (LOG2_E if attn else 1.0), len(operands)),
        grid=(M // tm,),
        in_specs=in_specs,
        out_specs=out_specs,
        out_shape=out_shape,
        input_output_aliases=aliases,
        compiler_params=_params("parallel"),
        name="inproj",
    )(*operands)


def _mlstm_kernel(q_ref, k_ref, v_ref, og_ref, misc_ref, gt_ref, mg_ref, c0_ref, n0_ref, m0_ref,
                  hm_ref, c_ref, n_ref, m_ref):
    L = q_ref.shape[1]

    @pl.when(pl.program_id(1) == 0)
    def _():
        c_ref[...] = c0_ref[...]
        n_ref[...] = n0_ref[...]
        m_ref[...] = m0_ref[...]

    gc = misc_ref[0]
    gr = gt_ref[0]
    row = lax.broadcasted_iota(jnp.int32, (L, L), 0)
    col = lax.broadcasted_iota(jnp.int32, (L, L), 1)
    causal = row >= col
    b_col = _dot_exact(causal.astype(F32), gc)
    b_row = _dot_exact(gr, (row <= col).astype(F32))
    for h in range(M_HEADS):
        sl = slice(h * M_DIM, (h + 1) * M_DIM)
        q, k, v = q_ref[0, :, sl], k_ref[0, :, sl], v_ref[0, :, sl]
        c_prev, n_prev, m_prev = c_ref[0, h], n_ref[0, h:h + 1, :], m_ref[0, h:h + 1, 0:1]
        bc = b_col[:, M_HEADS + h:M_HEADS + h + 1]
        ic = gc[:, h:h + 1]
        br = b_row[M_HEADS + h:M_HEADS + h + 1, :]
        ir = gr[h:h + 1, :]
        d = jnp.where(causal, bc - br + ir, NEG)
        m_inter = bc + m_prev
        m_t = jnp.maximum(m_inter, jnp.max(d, axis=-1, keepdims=True))
        w_inter = jnp.exp(m_inter - m_t)
        p = _dot_nt(q, k) * jnp.exp(d - m_t)
        num = w_inter * _dot(q, c_prev.astype(BF16)) + _dot(p.astype(BF16), v)
        den = (w_inter * jnp.sum(q.astype(F32) * n_prev, axis=-1, keepdims=True)
               + jnp.sum(p, axis=-1, keepdims=True))
        hh = num / jnp.maximum(jnp.abs(den), jnp.exp(-m_t))
        m_new = m_t[L - 1:L, :]
        w_new = jnp.exp(bc[L - 1:L, :] - bc + ic - m_new)
        decay = jnp.exp(m_inter[L - 1:L, :] - m_new)
        kf, vf = k.astype(F32), v.astype(F32)
        c_ref[0, h] = decay * c_prev + lax.dot_general(k, (w_new * vf).astype(BF16), _TN,
                                                       preferred_element_type=F32)
        n_ref[0, h:h + 1, :] = decay * n_prev + jnp.sum(w_new * kf, axis=0, keepdims=True)
        m_ref[0, h:h + 1, :] = jnp.broadcast_to(m_new, (1, M_DIM))
        hn = hh * lax.rsqrt(jnp.mean(hh * hh, axis=-1, keepdims=True) + EPS)
        hm_ref[0, :, sl] = ((hn * mg_ref[:, sl]) * og_ref[0, :, sl]).astype(BF16)


def _mlstm_t_kernel(q_ref, k_ref, v_ref, og_ref, misc_ref, gt_ref, mg_ref, c0_ref, n0_ref, m0_ref,
                    hm_ref, c_ref, n_ref, m_ref):
    L = q_ref.shape[1]

    @pl.when(pl.program_id(1) == 0)
    def _():
        c_ref[...] = c0_ref[...]
        n_ref[...] = n0_ref[...]
        m_ref[...] = m0_ref[...]

    gc = misc_ref[0]
    gr = gt_ref[0]
    src = lax.broadcasted_iota(jnp.int32, (L, L), 0)
    tgt = lax.broadcasted_iota(jnp.int32, (L, L), 1)
    causal = src <= tgt
    b_col = _dot_exact((src >= tgt).astype(F32), gc)
    b_row = _dot_exact(gr, causal.astype(F32))
    contract_00 = (((0,), (1,)), ((), ()))
    for h in range(M_HEADS):
        sl = slice(h * M_DIM, (h + 1) * M_DIM)
        q, k, v = q_ref[0, :, sl], k_ref[0, :, sl], v_ref[0, :, sl]
        c_prev, n_prev, m_prev = c_ref[0, h], n_ref[0, h:h + 1, :], m_ref[0, h:h + 1, 0:1]
        br = b_row[M_HEADS + h:M_HEADS + h + 1, :]
        ir = gr[h:h + 1, :]
        src_term = b_col[:, M_HEADS + h:M_HEADS + h + 1] - gc[:, h:h + 1]
        d = jnp.where(causal, br - src_term, NEG)
        m_inter = br + m_prev
        m_t = jnp.maximum(m_inter, jnp.max(d, axis=0, keepdims=True))
        w_inter = jnp.exp(m_inter - m_t)
        p = _dot_nt(k, q) * jnp.exp(d - m_t)
        k_t = k.astype(F32).T
        v_t = v.astype(F32).T.astype(BF16)
        num = (w_inter * lax.dot_general(c_prev.astype(BF16), q, contract_00, preferred_element_type=F32)
               + _dot(v_t, p.astype(BF16)))
        q_n = _dot_nt(jnp.broadcast_to(n_prev, (8, M_DIM)).astype(BF16), q)[0:1, :]
        den = w_inter * q_n + jnp.sum(p, axis=0, keepdims=True)
        hh = num / jnp.maximum(jnp.abs(den), jnp.exp(-m_t))
        m_new = m_t[:, L - 1:L]
        w_new = jnp.exp(br[:, L - 1:L] - br + ir - m_new)
        decay = jnp.exp(m_inter[:, L - 1:L] - m_new)
        c_ref[0, h] = decay * c_prev + _dot((k_t * w_new).astype(BF16), v)
        n_ref[0, h:h + 1, :] = decay * n_prev + _dot(jnp.broadcast_to(w_new, (8, L)).astype(BF16), k)[0:1, :]
        m_ref[0, h:h + 1, :] = jnp.broadcast_to(m_new, (1, M_DIM))
        hn = hh * lax.rsqrt(jnp.mean(hh * hh, axis=0, keepdims=True) + EPS)
        hm_ref[0, :, sl] = ((hn.T * mg_ref[:, sl]) * og_ref[0, :, sl]).astype(BF16)


def _mlstm(qkv, og, misc, gt, mg, c0, n0, m0, L, transposed):
    B, T, _ = qkv.shape
    seq = lambda j: pl.BlockSpec((1, L, M_WIDTH), lambda b, c: (b, c, j))
    st4 = pl.BlockSpec((1, M_HEADS, M_DIM, M_DIM), lambda b, c: (b, 0, 0, 0))
    st3 = pl.BlockSpec((1, M_HEADS, M_DIM), lambda b, c: (b, 0, 0))
    return pl.pallas_call(
        _mlstm_t_kernel if transposed else _mlstm_kernel,
        grid=(B, T // L),
        in_specs=[seq(0), seq(1), seq(2), seq(0),
                  pl.BlockSpec((1, L, LANES), lambda b, c: (b, c, 0)),
                  pl.BlockSpec((1, 2 * M_HEADS, L), lambda b, c: (b, 0, c)),
                  pl.BlockSpec((1, M_WIDTH), lambda b, c: (0, 0)),
                  st4, st3, st3],
        out_specs=(seq(0), st4, st3, st3),
        out_shape=(jax.ShapeDtypeStruct((B, T, M_WIDTH), BF16),
                   jax.ShapeDtypeStruct(c0.shape, F32),
                   jax.ShapeDtypeStruct(n0.shape, F32),
                   jax.ShapeDtypeStruct(m0.shape, F32)),
        compiler_params=_params("parallel", "arbitrary"),
        name="mlstm",
    )(qkv, qkv, qkv, og, misc, gt, mg, c0, n0, m0)


def _cmp_accumulate(load, w_ref, s):
    acc = None
    for r in range(0, D_CMP, 2):
        lhs = jnp.concatenate([load(r), load(r + 1)], axis=1).astype(BF16)
        term = _dot(lhs, w_ref[s, r // 2])
        acc = term if acc is None else acc + term
    return acc


def _cmp_prompt_kernel(k_ref, v_ref, w_ref, fs_ref):
    n = k_ref.shape[0] // D_CMP
    for s, ref in enumerate((k_ref, v_ref)):
        fs_ref[s] = _cmp_accumulate(lambda r: ref[pl.ds(r, n, stride=D_CMP), :], w_ref, s)


def _cmp_prompt(rows, w_cmp, tr):
    M = rows.shape[0]
    return pl.pallas_call(
        _cmp_prompt_kernel,
        grid=(M // tr,),
        in_specs=[pl.BlockSpec((tr, KV_WIDTH), lambda i: (i, 0)), pl.BlockSpec((tr, KV_WIDTH), lambda i: (i, 1)),
                  pl.BlockSpec(w_cmp.shape, lambda i: (0, 0, 0, 0))],
        out_specs=pl.BlockSpec((2, tr // D_CMP, 2 * LANES), lambda i: (0, i, 0)),
        out_shape=jax.ShapeDtypeStruct((2, M // D_CMP, 2 * LANES), F32),
        compiler_params=_params("parallel"),
        name="cmp_prompt",
    )(rows, rows, w_cmp)


def _cmp_sample_kernel(n_group, pt_ref, *refs):
    page_refs, w_ref, fs_ref, rows_s = refs[:n_group], refs[n_group], refs[n_group + 1], refs[n_group + 2:]
    page = page_refs[0].shape[2]
    n = n_group * page // D_CMP
    for s in range(2):
        for i, p in enumerate(page_refs):
            rows_s[s][i * page:(i + 1) * page, :] = p[s].T
    for s in range(2):
        fs_ref[s, 0] = _cmp_accumulate(lambda r: rows_s[s][pl.ds(r, n, stride=D_CMP), :], w_ref, s)


def _cmp_sample(cache_t, page_table, layer, w_cmp, n_group):
    B, n_pages = page_table.shape
    page = cache_t.shape[4]
    per_page = page // D_CMP

    def page_spec(i):
        return pl.BlockSpec((None, None, 2, KV_WIDTH, page),
                            lambda b, p, pt: (pt[b, p * n_group + i], layer, 0, 0, 0))

    grid_spec = pltpu.PrefetchScalarGridSpec(
        num_scalar_prefetch=1,
        grid=(B, n_pages // n_group),
        in_specs=[page_spec(i) for i in range(n_group)]
        + [pl.BlockSpec(w_cmp.shape, lambda b, p, pt: (0, 0, 0, 0))],
        out_specs=pl.BlockSpec((2, 1, n_group * per_page, 2 * LANES), lambda b, p, pt: (0, b, p, 0)),
        scratch_shapes=[pltpu.VMEM((n_group * page, KV_WIDTH), F32), pltpu.VMEM((n_group * page, KV_WIDTH), F32)],
    )
    return pl.pallas_call(
        functools.partial(_cmp_sample_kernel, n_group),
        grid_spec=grid_spec,
        out_shape=jax.ShapeDtypeStruct((2, B, n_pages * per_page, 2 * LANES), F32),
        compiler_params=_params("parallel", "arbitrary"),
        name="cmp_sample",
    )(page_table, *([cache_t] * n_group), w_cmp)


def _compressed_kv(fs_ref, bias_ref, tabs, b=0):
    n = fs_ref.shape[2]
    out = []
    for s in range(2):
        fs = fs_ref[s, b]
        out.append(fs[:, 0:LANES] + pltpu.roll(fs[:, LANES:2 * LANES], n - 1, 0) + bias_ref[s:s + 1, :])
    return _rope128(out[0], *tabs), out[1]


def _masked_softmax(s, valid, axis):
    m = jnp.max(jnp.where(valid, s, NEG), axis=axis, keepdims=True)
    e = jnp.where(valid, jnp.exp(s - m), 0.0)
    l = jnp.sum(e, axis=axis, keepdims=True)
    return e * (1.0 / jnp.where(l > 0.0, l, 1.0))


def _bias_softmax(s, axis, keep=None):
    m = jnp.max(s, axis=axis, keepdims=True)
    e = jnp.exp2(s - m)
    scale = 1.0 / jnp.sum(e, axis=axis, keepdims=True)
    return e * (scale if keep is None else scale * keep)


def _split3(x):
    h1 = x.astype(BF16)
    r1 = x - h1.astype(F32)
    h2 = r1.astype(BF16)
    return h1, h2, (r1 - h2.astype(F32)).astype(BF16)


def _dot_split3(a, x):
    h1, h2, h3 = _split3(x)
    return _dot(a, h1) + _dot(a, h2) + _dot(a, h3)


def _dot_split3_rhs(x, a):
    h1, h2, h3 = _split3(x)
    return _dot(h1, a) + _dot(h2, a) + _dot(h3, a)


def _top_blocks(score, n_top, axis):
    pos = lax.broadcasted_iota(jnp.int32, score.shape, axis).astype(F32)
    picks = []
    for _ in range(n_top):
        mx = jnp.max(score, axis=axis, keepdims=True)
        idx = jnp.min(jnp.where(score == mx, pos, float(score.shape[axis])), axis=axis, keepdims=True)
        picks.append(idx)
        score = jnp.where(pos == idx, NEG, score)
    return picks, score


def _nsa_prompt_kernel(tk, q_ref, gt_ref, fs_ref, cb_ref, kc_c, kc_s1, kc_s2, ovt_ref,
                       ks_ref, kw_ref, vst_ref, vwt_ref, oa_ref,
                       kc_s, vct_s, sel_s, m_s, acc_s, sc_s, part_s, oc_s, imp_s):
    n_cmp = fs_ref.shape[2]
    n_sel = ovt_ref.shape[0]
    nq = Q_BLOCK
    cols = A_REP * nq
    per_tile = tk // L_SEL
    qi = pl.program_id(1)
    q0 = qi * nq

    @pl.when(qi == 0)
    def _():
        kc, vc = _compressed_kv(fs_ref, cb_ref, (kc_c[...], kc_s1[...], kc_s2[...]))
        kc_s[...] = kc.astype(BF16)
        vct_s[...] = vc.T.astype(BF16)

    t_pos = q0 + lax.broadcasted_iota(jnp.int32, (1, nq), 1)
    gates = gt_ref[0]

    def heads(a):
        return jnp.concatenate([a] * A_REP, axis=1)

    def all_heads(a):
        return jnp.concatenate([a] * A_HEADS, axis=1)

    cmp_end = lax.broadcasted_iota(jnp.int32, (n_cmp, 1), 0) * D_CMP + (L_CMP - 1)
    c_bias = jnp.where(cmp_end <= t_pos, 0.0, NEG)
    c_keep = jnp.where(t_pos >= L_CMP - 1, 1.0, 0.0)
    blk = lax.broadcasted_iota(jnp.int32, (n_sel, 1), 0)
    t_both = jnp.concatenate([t_pos] * A_KV, axis=1)
    cur = t_both // L_SEL
    forced = (blk == 0) | (blk == cur) | (blk == cur - 1)
    started = blk * L_SEL <= t_both
    w_start = jnp.maximum(q0 - WINDOW, 0)
    w_len = WINDOW + nq
    rel = t_pos - (w_start + lax.broadcasted_iota(jnp.int32, (w_len, 1), 0))
    w_bias = jnp.where((rel >= 0) & (rel < WINDOW), 0.0, NEG)
    n_tiles = (q0 + nq + tk - 1) // tk
    key_iota = lax.broadcasted_iota(jnp.int32, (tk, 1), 0)

    groups = range(A_KV)
    vs = [slice(g * A_DIM, (g + 1) * A_DIM) for g in groups]
    gs = [slice(g * cols, (g + 1) * cols) for g in groups]
    q_all = q_ref[...].reshape(A_HEADS * nq, LANES)

    n_cls = n_cmp // LANES if n_cmp % LANES == 0 else 1
    unit = n_cmp // n_cls
    cls = jnp.minimum(((q0 + nq) // D_CMP + unit - 1) // unit, n_cls)

    def compressed(rows):
        p_c = _bias_softmax(_dot_nt(kc_s[0:rows, :], q_all) + all_heads(c_bias[0:rows]), 0, all_heads(c_keep))
        p_c16 = p_c.astype(BF16)
        p_sums = []
        for g in groups:
            oc_s[g] = _dot(vct_s[vs[g], 0:rows], p_c16[:, gs[g]])
            p_sum = p_c[:, g * cols:g * cols + nq]
            for r in range(1, A_REP):
                p_sum = p_sum + p_c[:, g * cols + r * nq:g * cols + (r + 1) * nq]
            p_sums.append(p_sum)
        imp_s[...] = _dot_split3(ovt_ref[:, 0:rows], jnp.concatenate(p_sums, axis=1))

    for k in range(1, n_cls + 1):
        pl.when(cls == k)(functools.partial(compressed, k * unit))
    o_c = [oc_s[g] for g in groups]

    score = jnp.where(forced, FORCE_SCORE, jnp.where(started, imp_s[...], -1.0))
    _, left_all = _top_blocks(score, min(N_TOP, n_sel), 0)
    for g in groups:
        left = left_all[:, g * nq:(g + 1) * nq]
        sel_s[g] = jnp.where(left < -2.0, 0.0, NEG)

    ws = pl.multiple_of(w_start, nq)
    p_w = _bias_softmax(_dot_nt(kw_ref[0, pl.ds(ws, w_len), :], q_all) + all_heads(w_bias), 0).astype(BF16)
    wc = w_start // LANES
    vw_t = jnp.concatenate([vwt_ref[0, wc + c] for c in range(w_len // LANES)], axis=1)
    for g in groups:
        o_w = _dot(vw_t[vs[g], :], p_w[:, gs[g]])
        for r in range(A_REP):
            hd = g * A_REP + r
            base = 2 * M_HEADS + hd * N_BRANCH
            cs = slice(r * nq, (r + 1) * nq)
            part_s[hd * A_DIM:(hd + 1) * A_DIM, :] = (gates[base:base + 1] * o_c[g][:, cs]
                                                      + gates[base + 2:base + 3] * o_w[:, cs])

    m_s[...] = jnp.full(m_s.shape, NEG, F32)
    acc_s[...] = jnp.zeros(acc_s.shape, F32)
    ones_rows = jnp.ones((acc_s.shape[1] - A_DIM, tk), BF16)

    last_tile = ks_ref.shape[1] // tk - 1

    def scores(kt):
        k0 = pl.multiple_of(jnp.minimum(kt, last_tile) * tk, tk)
        return _dot_nt(ks_ref[0, pl.ds(k0, tk), :], q_all)

    def attend(kt, slot):
        kc = jnp.minimum(kt, last_tile)
        causal = kt * tk + key_iota <= t_pos
        bias = []
        for g in groups:
            chosen = jnp.concatenate(
                [jnp.broadcast_to(sel_s[g, pl.ds(kc * per_tile + i, 1), :], (L_SEL, nq)) for i in range(per_tile)],
                axis=0)
            bias.append(heads(jnp.where(causal, chosen, NEG)))
        s = sc_s[slot] + jnp.concatenate(bias, axis=1)
        m_old = m_s[...]
        m_new = jnp.maximum(m_old, jnp.max(s, axis=0, keepdims=True))
        e = jnp.exp2(s - m_new)
        a = jnp.exp2(m_old - m_new)
        e16 = e.astype(BF16)
        for g in groups:
            vals = jnp.concatenate([vst_ref[0, kc, vs[g], :], ones_rows], axis=0)
            acc_s[g] = a[:, gs[g]] * acc_s[g] + _dot(vals, e16[:, gs[g]])
        m_s[...] = m_new

    sc_s[0] = scores(0)

    def pair(j, carry):
        sc_s[1] = scores(2 * j + 1)
        attend(2 * j, 0)
        sc_s[0] = scores(2 * j + 2)
        attend(2 * j + 1, 1)
        return carry

    lax.fori_loop(0, n_tiles // 2, pair, 0)

    @pl.when(n_tiles % 2 == 1)
    def _():
        attend(n_tiles - 1, 0)

    heads_out = []
    for g in groups:
        l = acc_s[g, A_DIM:A_DIM + 1, :]
        o_s = acc_s[g, 0:A_DIM, :] * (1.0 / jnp.where(l > 0.0, l, 1.0))
        for r in range(A_REP):
            hd = g * A_REP + r
            gate = gates[2 * M_HEADS + hd * N_BRANCH + 1:2 * M_HEADS + hd * N_BRANCH + 2]
            heads_out.append(part_s[hd * A_DIM:(hd + 1) * A_DIM, :] + gate * o_s[:, r * nq:(r + 1) * nq])
    oa_ref[0] = jnp.concatenate(heads_out, axis=0).T.astype(BF16)


def _nsa_prompt(q8, gt, fs, cbias, kc_tabs, ovt, kk, vst, vwt, B, T, tk):
    nqb = T // Q_BLOCK
    n_cmp = fs.shape[2]
    n_sel = ovt.shape[0]
    cols = A_REP * Q_BLOCK
    const = lambda a: pl.BlockSpec(a.shape, lambda b, i: (0,) * a.ndim)
    keys = lambda j: pl.BlockSpec((1, T, LANES), lambda b, i: (b, 0, j))
    per_b = lambda a: pl.BlockSpec((1,) + a.shape[1:], lambda b, i: (b, 0, 0, 0))
    return pl.pallas_call(
        functools.partial(_nsa_prompt_kernel, tk),
        grid=(B, nqb),
        in_specs=[pl.BlockSpec((A_HEADS, Q_BLOCK, LANES), lambda b, i: (0, b * nqb + i, 0)),
                  pl.BlockSpec((1, gt.shape[1], Q_BLOCK), lambda b, i: (b, 0, i)),
                  pl.BlockSpec((2, 1, n_cmp, 2 * LANES), lambda b, i: (0, b, 0, 0)),
                  const(cbias), const(kc_tabs[0]), const(kc_tabs[1]), const(kc_tabs[2]),
                  const(ovt), keys(0), keys(1), per_b(vst), per_b(vwt)],
        out_specs=pl.BlockSpec((1, Q_BLOCK, A_WIDTH), lambda b, i: (b, i, 0)),
        out_shape=jax.ShapeDtypeStruct((B, T, A_WIDTH), BF16),
        scratch_shapes=[pltpu.VMEM((n_cmp, LANES), BF16), pltpu.VMEM((LANES, n_cmp), BF16),
                        pltpu.VMEM((A_KV, n_sel, Q_BLOCK), F32),
                        pltpu.VMEM((1, A_KV * cols), F32),
                        pltpu.VMEM((A_KV, A_DIM + 16, cols), F32),
                        pltpu.VMEM((2, tk, A_KV * cols), F32),
                        pltpu.VMEM((A_WIDTH, Q_BLOCK), F32),
                        pltpu.VMEM((A_KV, A_DIM, cols), F32),
                        pltpu.VMEM((n_sel, A_KV * Q_BLOCK), F32)],
        compiler_params=_params("parallel", "arbitrary"),
        name="nsa_prompt",
    )(q8, gt, fs, cbias, *kc_tabs, ovt, kk, kk, vst, vwt)


def _nsa_sample_select_kernel(q_pos, n_sel, q_ref, fs_ref, cb_ref, kc_c, kc_s1, kc_s2, ov_ref, oc_ref, idx_ref):
    n_seq, n_cmp = fs_ref.shape[1], fs_ref.shape[2]
    tabs = (kc_c[...], kc_s1[...], kc_s2[...])
    cmp_end = lax.broadcasted_iota(jnp.int32, (A_HEADS, n_cmp), 1) * D_CMP + (L_CMP - 1)
    head = lax.broadcasted_iota(jnp.int32, (A_HEADS, n_cmp), 0)
    imps = []
    for b in range(n_seq):
        kc, vc = _compressed_kv(fs_ref, cb_ref, tabs, b)
        p_c = _masked_softmax(_dot_nt(q_ref[b], kc.astype(BF16)), cmp_end <= q_pos, 1)
        oc_ref[b] = _dot(p_c.astype(BF16), vc.astype(BF16))
        p_sum = jnp.zeros((A_HEADS, n_cmp), F32)
        for g in range(A_KV):
            pg = jnp.sum(jnp.where(head // A_REP == g, p_c, 0.0), axis=0, keepdims=True)
            p_sum = jnp.where(head == g, pg, p_sum)
        imps.append(_dot_split3_rhs(p_sum, ov_ref[...]))
    imp = jnp.concatenate(imps, axis=0)
    n_pad = ov_ref.shape[1]
    blk = lax.broadcasted_iota(jnp.int32, (A_HEADS * n_seq, n_pad), 1)
    cur = q_pos // L_SEL
    forced = (blk == 0) | (blk == cur) | (blk == cur - 1)
    score = jnp.where(forced, FORCE_SCORE, jnp.where(blk * L_SEL <= q_pos, imp, -1.0))
    score = jnp.where(blk < n_sel, score, 2 * NEG)
    picks, _ = _top_blocks(score, min(N_TOP, n_sel), 1)
    lane = lax.broadcasted_iota(jnp.int32, (A_HEADS * n_seq, LANES), 1)
    out = jnp.zeros((A_HEADS * n_seq, LANES), jnp.int32)
    for j, idx in enumerate(picks):
        out = jnp.where(lane == j, idx.astype(jnp.int32), out)
    for b in range(n_seq):
        idx_ref[b] = out[b * A_HEADS:(b + 1) * A_HEADS]


def _nsa_sample_select(q8, fs, cbias, kc_tabs, ov, q_pos, n_sel, n_seq):
    B = q8.shape[0]
    n_cmp = fs.shape[2]
    const = lambda a: pl.BlockSpec(a.shape, lambda b: (0,) * a.ndim)
    per_b = pl.BlockSpec((n_seq, A_HEADS, LANES), lambda b: (b, 0, 0))
    return pl.pallas_call(
        functools.partial(_nsa_sample_select_kernel, q_pos, n_sel),
        grid=(B // n_seq,),
        in_specs=[per_b, pl.BlockSpec((2, n_seq, n_cmp, 2 * LANES), lambda b: (0, b, 0, 0)),
                  const(cbias), const(kc_tabs[0]), const(kc_tabs[1]), const(kc_tabs[2]), const(ov)],
        out_specs=(per_b, per_b),
        out_shape=(jax.ShapeDtypeStruct((B, A_HEADS, LANES), F32),
                   jax.ShapeDtypeStruct((B, A_HEADS, LANES), jnp.int32)),
        compiler_params=_params("parallel"),
        name="nsa_sample_select",
    )(q8, fs, cbias, *kc_tabs, ov)


def _r16(a):
    return a.astype(BF16).astype(F32)


def _nsa_sample_attend_kernel(q_pos, n_top, pt_ref, ix_ref, q_ref, new_ref, newc_ref, oc_ref, misc_ref, win_ref,
                              *refs):
    page_refs, (oa_ref, nwin_ref) = refs[:A_KV * n_top], refs[A_KV * n_top:]
    b = pl.program_id(0)
    q = q_ref[0]
    qf = q.astype(F32)
    head = lax.broadcasted_iota(jnp.int32, (A_HEADS, 1), 0)
    new = new_ref[0]
    page = page_refs[0].shape[2]
    per_page = page // L_SEL
    lane = lax.broadcasted_iota(jnp.int32, (A_HEADS, page), 1)
    s_tok = jnp.sum(qf * _r16(new[0:1, :]), axis=-1, keepdims=True)

    o_s = jnp.zeros((A_HEADS, LANES), F32)
    for g in range(A_KV):
        parts, n_fresh = [], 0
        for j in range(n_top):
            blk = ix_ref[b, g, j]
            fresh = blk * L_SEL >= q_pos
            limit = jnp.where(fresh, -1, q_pos)
            s = _dot(q, page_refs[g * n_top + j][0].astype(BF16))
            s_pos = (blk // per_page) * page + lane
            ok = jnp.where(lane // L_SEL == blk % per_page, s_pos, limit + 1) <= limit
            parts.append(jnp.where(ok, s, NEG))
            n_fresh = n_fresh + jnp.where(fresh, 1, 0)
        s_all = jnp.concatenate(parts, axis=1)
        has_tok = jnp.where(head >= 0, n_fresh, 0) > 0
        m = jnp.maximum(jnp.max(s_all, axis=-1, keepdims=True), jnp.where(has_tok, s_tok, NEG))
        e = jnp.exp(s_all - m)
        e_tok = jnp.where(has_tok, jnp.exp(s_tok - m), 0.0)
        den = jnp.sum(e, axis=-1, keepdims=True) + e_tok
        e16 = e.astype(BF16)
        acc = _r16(e_tok) * _r16(new[1:2, :])
        for j in range(n_top):
            acc = acc + _dot_nt(e16[:, j * page:(j + 1) * page], page_refs[g * n_top + j][1].astype(BF16))
        o_s = jnp.where(head // A_REP == g, acc / den, o_s)

    wb = win_ref.shape[2]
    kwt, vwt = win_ref[0], win_ref[1]
    s_old = _dot(q, kwt.astype(BF16))
    s_new = jnp.sum(qf * _r16(new[2:3, :]), axis=-1, keepdims=True)
    rel = wb - lax.broadcasted_iota(jnp.int32, (A_HEADS, wb), 1)
    ok = (rel >= 0) & (rel < WINDOW) & (q_pos - rel >= 0)
    m = jnp.maximum(jnp.max(jnp.where(ok, s_old, NEG), axis=-1, keepdims=True), s_new)
    e_old = jnp.where(ok, jnp.exp(s_old - m), 0.0)
    e_new = jnp.exp(s_new - m)
    den = jnp.sum(e_old, axis=-1, keepdims=True) + e_new
    o_w = _dot_nt((e_old / den).astype(BF16), vwt.astype(BF16)) + _r16(e_new / den) * _r16(new[3:4, :])
    gates = misc_ref[0]
    oa_ref[0] = gates[:, 0:1] * oc_ref[0] + gates[:, 1:2] * o_s + gates[:, 2:3] * o_w
    newc = newc_ref[0]
    last = lax.broadcasted_iota(jnp.int32, (LANES, wb), 1) == wb - 1
    nwin_ref[0] = jnp.where(last, newc[:, 2:3], pltpu.roll(kwt, wb - 1, 1))
    nwin_ref[1] = jnp.where(last, newc[:, 3:4], pltpu.roll(vwt, wb - 1, 1))


def _nsa_sample_attend(page_table, idx, q8, new_rows, new_cols, o_c, gate_rows, win_t, cache_t, layer, q_pos):
    B, n_pages = page_table.shape
    n_top = idx.shape[2]
    page = cache_t.shape[4]
    per_page = page // L_SEL
    wb = win_t.shape[4]

    def page_spec(g, j):
        def index(b, pt, ix):
            pg = jnp.clip(ix[b, g, j] // per_page, 0, n_pages - 1)
            return (pt[b, pg], layer, 1, 0, 0)
        return pl.BlockSpec((None, None, 2, KV_WIDTH, page), index)

    per_b = pl.BlockSpec((1, A_HEADS, LANES), lambda b, pt, ix: (b, 0, 0))
    pages = [page_spec(g, j) for g in range(A_KV) for j in range(n_top)]
    grid_spec = pltpu.PrefetchScalarGridSpec(
        num_scalar_prefetch=2,
        grid=(B,),
        in_specs=[per_b, per_b, pl.BlockSpec((1, KV_WIDTH, A_HEADS), lambda b, pt, ix: (b, 0, 0)), per_b, per_b,
                  pl.BlockSpec((None, None, 2, KV_WIDTH, wb), lambda b, pt, ix: (b, layer, 0, 0, 0))] + pages,
        out_specs=(per_b, pl.BlockSpec((None, 2, KV_WIDTH, wb), lambda b, pt, ix: (b, 0, 0, 0))),
    )
    return pl.pallas_call(
        functools.partial(_nsa_sample_attend_kernel, q_pos, n_top),
        grid_spec=grid_spec,
        out_shape=(jax.ShapeDtypeStruct((B, A_HEADS, LANES), F32),
                   jax.ShapeDtypeStruct((B, 2, KV_WIDTH, wb), F32)),
        compiler_params=_params("parallel"),
        name="nsa_sample_attend",
    )(page_table, idx, q8, new_rows, new_cols, o_c, gate_rows, win_t, *([cache_t] * len(pages)))


def _ffn_kernel(final, tf, x_ref, hm_ref, oa_ref, wo_ref, nf_ref, wg_ref, wu_ref, wd_ref, nfin_ref, out_ref, act_s):
    half = hm_ref.shape[1]
    x1 = x_ref[...] + _dot(hm_ref[...], wo_ref[0:half, :]) + _dot(oa_ref[...], wo_ref[half:, :])
    ms = jnp.mean(x1 * x1, axis=-1, keepdims=True)
    xn = ((x1 * lax.rsqrt(ms + EPS)) * nf_ref[...]).astype(BF16)
    for c in range(wg_ref.shape[1] // tf):
        sl = slice(c * tf, (c + 1) * tf)
        gate = _dot(xn, wg_ref[:, sl])
        act_s[:, sl] = ((gate * _sigmoid(gate)) * _dot(xn, wu_ref[:, sl])).astype(BF16)
    y = x1 + _dot(act_s[...], wd_ref[...])
    if final:
        ms = jnp.mean(y * y, axis=-1, keepdims=True)
        y = (y * lax.rsqrt(ms + EPS)) * nfin_ref[...]
    out_ref[...] = y


def _ffn(x, hm, oa, wo, nf, wg, wu, wd, nfin, final, tm, tf):
    M, D = x.shape
    F = wg.shape[1]
    row = lambda w: pl.BlockSpec((tm, w), lambda i: (i, 0))
    const = lambda a: pl.BlockSpec(a.shape, lambda i: (0,) * a.ndim, pipeline_mode=pl.Buffered(1))
    return pl.pallas_call(
        functools.partial(_ffn_kernel, final, tf),
        grid=(M // tm,),
        in_specs=[row(D), row(hm.shape[1]), row(oa.shape[1]), const(wo), const(nf),
                  const(wg), const(wu), const(wd), const(nfin)],
        out_specs=row(D),
        out_shape=jax.ShapeDtypeStruct((M, D), F32),
        scratch_shapes=[pltpu.VMEM((tm, F), BF16)],
        compiler_params=_params("parallel"),
        name="ffn",
    )(x, hm, oa, wo, nf, wg, wu, wd, nfin)


def _permute_w_in(w):
    a = 4 * M_WIDTH
    b = a + 2 * M_HEADS
    c = b + A_WIDTH + 6 * KV_WIDTH
    pad = jnp.zeros((w.shape[0], LANES - N_MISC), w.dtype)
    return jnp.concatenate([w[:, :a], w[:, b:c], w[:, a:b], w[:, c:], pad], axis=1).astype(BF16)


def _cmp_weights(wk, wv):
    def one(w):
        z = jnp.zeros((D_CMP, A_DIM, A_DIM), w.dtype)
        halves = []
        for part in (w[:D_CMP], w[D_CMP:]):
            top = jnp.concatenate([part, z], axis=2)
            bot = jnp.concatenate([z, part], axis=2)
            halves.append(jnp.concatenate([top, bot], axis=1))
        return jnp.concatenate(halves, axis=2)
    return jnp.stack([one(wk), one(wv)]).astype(BF16).reshape(2, D_CMP // 2, 2 * KV_WIDTH, 2 * LANES)


def _overlap(n_cmp_pad, n_sel_pad, n_cmp, n_sel):
    lo_c = np.arange(n_cmp_pad)[:, None] * D_CMP
    lo_s = np.arange(n_sel_pad)[None, :] * L_SEL
    ov = np.clip(np.minimum(lo_c + L_CMP, lo_s + L_SEL) - np.maximum(lo_c, lo_s), 0, None).astype(np.float32) / L_CMP
    ov[n_cmp:, :] = 0.0
    ov[:, n_sel:] = 0.0
    return jnp.asarray(ov)


def _pick(n, prefs):
    for p in prefs:
        if n % p == 0:
            return p
    return n


def kernel(x_prompt, x_sample, cache_nsa_kv, page_table, state_win_kv, state_mlstm_c, state_mlstm_n, state_mlstm_m, norm_mix, w_in, b_if, m_norm, w_cmp_k, b_cmp_k, w_cmp_v, b_cmp_v, w_out, norm_ffn, w_gate, w_up, w_down, norm_final):
    B, T, D = x_prompt.shape
    Bs, Ts, _ = x_sample.shape
    depth = w_in.shape[0]
    n_pool, _, n_slots, page = cache_nsa_kv.shape[:4]
    n_pages = page_table.shape[1]
    past = n_pages * page
    wb = state_win_kv.shape[3]
    assert Ts == 1 and T % Q_BLOCK == 0 and T >= WINDOW + Q_BLOCK and wb == WINDOW
    assert past % page == 0 and page % L_SEL == 0 and page % D_CMP == 0
    assert ((past + Ts) // D_CMP) * D_CMP <= past

    Mp = B * T
    tm_p = _pick(Mp, (512, 256, 128))
    tm_f = _pick(Mp, (512, 256, 128))
    ff = w_gate.shape[2]
    tf = _pick(ff, (256, 128))
    l_chunk = _pick(T, (256, 128, 64))
    ls_pad = 16
    tk = _pick(T, (512, 256, 128))
    assert tk % tm_p == 0 and tm_p % LANES == 0
    n_group = _pick(n_pages, (16, 8, 4, 2, 1))
    tr = _pick(T, (2048, 1024, 512, 256))

    n_chunk_p = T // D_CMP
    n_cmp_p, n_sel_p = n_chunk_p - 1, -(-T // L_SEL)
    tabs_p = _rope_tables(jnp.arange(T))
    kc_tabs_p = _rope_tables(jnp.arange(n_chunk_p) * D_CMP)
    assert L_CMP <= 256
    ovt_p = _overlap(n_chunk_p, n_sel_p, n_cmp_p, n_sel_p).T.astype(BF16)
    n_chunk_s = (past + Ts) // D_CMP
    n_cmp_s, n_sel_s = n_chunk_s - 1, -(-(past + Ts) // L_SEL)
    n_sel_pad = -(-n_sel_s // LANES) * LANES
    tabs_s = _rope_tables(jnp.full((Bs,), past))
    kc_tabs_s = _rope_tables(jnp.arange(n_chunk_s) * D_CMP)
    ov_s = _overlap(n_chunk_s, n_sel_pad, n_cmp_s, n_sel_s).astype(BF16)
    cache_t = cache_nsa_kv.transpose(0, 1, 2, 4, 5, 3).reshape(n_pool, depth, n_slots, KV_WIDTH, page)
    win_t = state_win_kv.transpose(0, 1, 2, 4, 5, 3).reshape(Bs, depth, 2, KV_WIDTH, wb)

    xp = x_prompt.reshape(Mp, D)
    xs = x_sample.reshape(Bs, D)
    rows_s, win_p, win_s, c_p, c_s, n_p, n_s, m_p, m_s = ([] for _ in range(9))
    y_p = y_s = None
    kvt = jnp.zeros((B, depth, n_slots * KV_WIDTH, T), F32)
    nfin = norm_final.reshape(1, D)
    for l in range(depth):
        w_perm = _permute_w_in(w_in[l])
        gain = norm_mix[l].reshape(1, D)
        bias = jnp.zeros((1, LANES), F32).at[0, :2 * M_HEADS].set(b_if[l].reshape(-1))
        mg = m_norm[l].reshape(1, M_WIDTH)
        w_cmp = _cmp_weights(w_cmp_k[l], w_cmp_v[l])
        cbias = jnp.stack([jnp.tile(b_cmp_k[l], A_KV), jnp.tile(b_cmp_v[l], A_KV)])
        wo, nf = w_out[l].astype(BF16), norm_ffn[l].reshape(1, D)
        wg, wu, wd = w_gate[l].astype(BF16), w_up[l].astype(BF16), w_down[l].astype(BF16)
        final = l == depth - 1

        qkv, og, q8, rows, win, misc, kk, vst, vwt, kvt = _inproj(xp, gain, w_perm, bias, tabs_p, tm_p,
                                                                    (tk, depth, l, kvt))
        misc3 = misc.reshape(B, T, LANES)
        gt = misc3[:, :, :N_MISC].transpose(0, 2, 1)
        hm, c_new, n_new, m_new = _mlstm(
            qkv.reshape(B, T, -1), og.reshape(B, T, -1), misc3, gt, mg,
            jnp.zeros((B, M_HEADS, M_DIM, M_DIM), F32), jnp.zeros((B, M_HEADS, M_DIM), F32),
            jnp.full((B, M_HEADS, M_DIM), NEG, F32), l_chunk, True)
        fs = _cmp_prompt(rows, w_cmp, tr).reshape(2, B, n_chunk_p, 2 * LANES)
        oa = _nsa_prompt(q8, gt, fs, cbias, kc_tabs_p, ovt_p, kk.reshape(B, T, -1),
                         vst.reshape(B, T // tk, KV_WIDTH, tk), vwt.reshape(B, T // LANES, KV_WIDTH, LANES), B, T, tk)
        xp_new = _ffn(xp, hm.reshape(Mp, -1), oa.reshape(Mp, -1), wo, nf, wg, wu, wd, nfin, final, tm_f, tf)
        if final:
            y_p = xp_new
        xp = xp_new
        win_last = win.reshape(B, T, 2 * KV_WIDTH)[:, T - wb:]
        win_p.append(win_last.reshape(B, wb, 2, A_KV, A_DIM).transpose(0, 2, 1, 3, 4))
        c_p.append(c_new); n_p.append(n_new); m_p.append(m_new[:, :, 0])

        qkv, og, q8, rows, win, misc = _inproj(xs, gain, w_perm, bias, tabs_s, Bs)
        pad_t = lambda a: jnp.pad(a[:, None, :], ((0, 0), (0, ls_pad - 1), (0, 0)))
        inert = jnp.zeros((ls_pad, LANES), F32).at[1:, :M_HEADS].set(NEG)
        misc_pad = pad_t(misc) + inert[None]
        gt = misc_pad[:, :, :2 * M_HEADS].transpose(0, 2, 1)
        m0 = jnp.broadcast_to(state_mlstm_m[:, l, :, None].astype(F32), (Bs, M_HEADS, M_DIM))
        hm, c_new, n_new, m_new = _mlstm(
            pad_t(qkv), pad_t(og), misc_pad, gt, mg,
            state_mlstm_c[:, l].astype(F32), state_mlstm_n[:, l].astype(F32), m0, ls_pad, False)
        hm = hm[:, 0]
        fs = _cmp_sample(cache_t, page_table, l, w_cmp, n_group)
        q8s = q8.transpose(1, 0, 2)
        o_c, idx = _nsa_sample_select(q8s, fs, cbias, kc_tabs_s, ov_s, past, n_sel_s, _pick(Bs, (4, 2, 1)))
        idx = idx[:, :A_KV, :min(N_TOP, n_sel_s)]
        new_rows = jnp.pad(jnp.stack([rows[:, 2 * LANES:3 * LANES], rows[:, 3 * LANES:], win[:, :LANES],
                                      win[:, LANES:]], axis=1), ((0, 0), (0, A_HEADS - 4), (0, 0)))
        gate_rows = jnp.pad(misc[:, 2 * M_HEADS:N_MISC].reshape(Bs, A_HEADS, N_BRANCH),
                            ((0, 0), (0, 0), (0, LANES - N_BRANCH)))
        oa8, nwin = _nsa_sample_attend(page_table, idx, q8s, new_rows, new_rows.transpose(0, 2, 1), o_c, gate_rows,
                                       win_t, cache_t, l, past)
        grp = (jnp.arange(A_HEADS) // A_REP)[None, :, None, None]
        oa = jnp.take_along_axis(oa8.reshape(Bs, A_HEADS, A_KV, A_DIM), jnp.broadcast_to(grp, (Bs, A_HEADS, 1, A_DIM)),
                                 axis=2).reshape(Bs, A_WIDTH).astype(BF16)
        xs_new = _ffn(xs, hm, oa, wo, nf, wg, wu, wd, nfin, final, Bs, tf)
        if final:
            y_s = xs_new
        xs = xs_new
        rows_s.append(rows.reshape(Bs, Ts, n_slots, A_KV, A_DIM))
        win_s.append(nwin.reshape(Bs, 2, A_KV, A_DIM, wb).transpose(0, 1, 4, 2, 3))
        c_s.append(c_new); n_s.append(n_new); m_s.append(m_new[:, :, 0])

    return (y_p.reshape(B, T, D), y_s.reshape(Bs, Ts, D),
            kvt.reshape(B, depth, n_slots, A_KV, A_DIM, T).transpose(0, 5, 1, 2, 3, 4), jnp.stack(rows_s, axis=2),
            jnp.stack(win_p, axis=1), jnp.stack(win_s, axis=1),
            jnp.stack(c_p, axis=1), jnp.stack(c_s, axis=1),
            jnp.stack(n_p, axis=1), jnp.stack(n_s, axis=1),
            jnp.stack(m_p, axis=1), jnp.stack(m_s, axis=1))
```

```python
import functools

import jax
import jax.numpy as jnp
import numpy as np
from jax import lax
from jax.experimental import pallas as pl
from jax.experimental.pallas import tpu as pltpu

F32 = jnp.float32
BF16 = jnp.bfloat16

M_HEADS = 4
M_DIM = 128
M_WIDTH = M_HEADS * M_DIM
A_HEADS = 8
A_DIM = 64
A_KV = 2
A_REP = A_HEADS // A_KV
A_WIDTH = A_HEADS * A_DIM
KV_WIDTH = A_KV * A_DIM
L_CMP = 32
D_CMP = 16
L_SEL = 64
N_TOP = 16
WINDOW = 512
Q_BLOCK = 128
N_BRANCH = 3
ROPE_THETA = 500000.0
ROPE_DIM = A_DIM // 4
EPS = 1e-6
NEG = -1e30
LOG2_E = 1.4426950408889634
FORCE_SCORE = 1e4
LANES = 128
N_MISC = 2 * M_HEADS + N_BRANCH * A_HEADS
D_PERM = 4 * M_WIDTH + A_WIDTH + 6 * KV_WIDTH + LANES
VMEM_LIMIT = 56 * 1024 * 1024

_NT = (((1,), (1,)), ((), ()))
_TN = (((0,), (0,)), ((), ()))


def _params(*sem):
    return pltpu.CompilerParams(dimension_semantics=sem, vmem_limit_bytes=VMEM_LIMIT)


def _sigmoid(x):
    return 1.0 / (1.0 + jnp.exp(-x))


def _dot(a, b):
    return jnp.dot(a, b, preferred_element_type=F32)


def _dot_exact(a, b):
    return jnp.dot(a, b, preferred_element_type=F32, precision=lax.Precision.HIGHEST)


def _dot_nt(a, b):
    return lax.dot_general(a, b, _NT, preferred_element_type=F32)


def _rope128(v, c, s1, s2):
    half = ROPE_DIM // 2
    return v * c + pltpu.roll(v, LANES - half, 1) * s1 + pltpu.roll(v, half, 1) * s2


def _rope_tables(pos):
    half = ROPE_DIM // 2
    inv = ROPE_THETA ** (-jnp.arange(half, dtype=F32) / half)
    ang = pos.astype(F32)[:, None] * inv[None, :]
    cos, sin = jnp.cos(ang), jnp.sin(ang)
    n = pos.shape[0]
    one = jnp.ones((n, A_DIM - ROPE_DIM), F32)
    zero = jnp.zeros((n, A_DIM - ROPE_DIM), F32)
    zh = jnp.zeros((n, half), F32)
    c = jnp.concatenate([cos, cos, one], axis=1)
    s1 = jnp.concatenate([-sin, zh, zero], axis=1)
    s2 = jnp.concatenate([zh, sin, zero], axis=1)
    tile = lambda a: jnp.concatenate([a, a], axis=1)
    return tile(c), tile(s1), tile(s2)


def _inproj_kernel(q_scale, n_in, *refs):
    x_ref, g_ref, w_ref, bias_ref, rc_ref, rs1_ref, rs2_ref = refs[:7]
    om_ref, og_ref, oq_ref, orows_ref, owin_ref, omisc_ref = refs[n_in:n_in + 6]
    attn_refs = refs[n_in + 6:]
    x = x_ref[...]
    ms = jnp.mean(x * x, axis=-1, keepdims=True)
    xn = ((x * lax.rsqrt(ms + EPS)) * g_ref[...]).astype(BF16)
    c, s1, s2 = rc_ref[...], rs1_ref[...], rs2_ref[...]
    lane = lax.broadcasted_iota(jnp.int32, (x.shape[0], LANES), 1)

    def proj(a, b):
        return _dot(xn, w_ref[:, a:b])

    om_ref[:, 0:M_WIDTH] = (proj(0, M_WIDTH) * (M_DIM ** -0.5)).astype(BF16)
    om_ref[:, M_WIDTH:3 * M_WIDTH] = proj(M_WIDTH, 3 * M_WIDTH).astype(BF16)
    og_ref[...] = _sigmoid(proj(3 * M_WIDTH, 4 * M_WIDTH))

    off = 4 * M_WIDTH
    pq = proj(off, off + A_WIDTH)
    for j in range(A_WIDTH // LANES):
        blk = _rope128(pq[:, j * LANES:(j + 1) * LANES], c, s1, s2) * q_scale
        swapped = pltpu.roll(blk, A_DIM, 1)
        for e in range(2):
            hd = 2 * j + e
            grp = hd // A_REP
            src = blk if e == grp else swapped
            keep = (lane // A_DIM) == grp
            oq_ref[hd] = jnp.where(keep, src, 0.0).astype(BF16)

    off += A_WIDTH
    pk = proj(off, off + 6 * KV_WIDTH)
    kcr, vcr, ksl, vsl, kw, vw = [pk[:, i * LANES:(i + 1) * LANES] for i in range(6)]
    ksl = _rope128(ksl, c, s1, s2)
    kw = _rope128(kw, c, s1, s2)
    owin_ref[:, 0:LANES] = kw
    owin_ref[:, LANES:2 * LANES] = vw
    if attn_refs:
        okk_ref, ovst_ref, ovwt_ref, okvt_ref = attn_refs
        orows_ref[:, 0:LANES] = kcr
        orows_ref[:, LANES:2 * LANES] = vcr
        vsl_t = vsl.T
        for i, a_t in enumerate((kcr.T, vcr.T, ksl.T, vsl_t)):
            okvt_ref[i * LANES:(i + 1) * LANES, :] = a_t
        okk_ref[:, 0:LANES] = ksl.astype(BF16)
        okk_ref[:, LANES:2 * LANES] = kw.astype(BF16)
        ovst_ref[0] = vsl_t.astype(BF16)
        for i in range(ovwt_ref.shape[0]):
            ovwt_ref[i] = vw[i * LANES:(i + 1) * LANES].T.astype(BF16)
    else:
        for i, a in enumerate((kcr, vcr, ksl, vsl)):
            orows_ref[:, i * LANES:(i + 1) * LANES] = a

    off += 6 * KV_WIDTH
    pm = proj(off, off + LANES) + bias_ref[...]
    log_sig = -(jnp.maximum(-pm, 0.0) + jnp.log(1.0 + jnp.exp(-jnp.abs(pm))))
    omisc_ref[...] = jnp.where(lane < M_HEADS, pm, jnp.where(lane < 2 * M_HEADS, log_sig, _sigmoid(pm)))


def _inproj(x, gain, w_perm, bias, tabs, tm, attn=None):
    M, D = x.shape
    nt = tabs[0].shape[0] // tm
    row = lambda i: (i, 0)
    tab = pl.BlockSpec((tm, LANES), lambda i: (i % nt, 0))
    full = lambda a: pl.BlockSpec(a.shape, lambda i: (0,) * a.ndim)
    out_shape = (
        jax.ShapeDtypeStruct((M, 3 * M_WIDTH), BF16),
        jax.ShapeDtypeStruct((M, M_WIDTH), F32),
        jax.ShapeDtypeStruct((A_HEADS, M, LANES), BF16),
        jax.ShapeDtypeStruct((M, (2 if attn else 4) * KV_WIDTH), F32),
        jax.ShapeDtypeStruct((M, 2 * KV_WIDTH), F32),
        jax.ShapeDtypeStruct((M, LANES), F32),
    )
    out_specs = (
        pl.BlockSpec((tm, 3 * M_WIDTH), row),
        pl.BlockSpec((tm, M_WIDTH), row),
        pl.BlockSpec((A_HEADS, tm, LANES), lambda i: (0, i, 0)),
        pl.BlockSpec((tm, (2 if attn else 4) * KV_WIDTH), row),
        pl.BlockSpec((tm, 2 * KV_WIDTH), row),
        pl.BlockSpec((tm, LANES), row),
    )
    in_specs = [pl.BlockSpec((tm, D), row), full(gain), full(w_perm), full(bias), tab, tab, tab]
    operands = [x, gain, w_perm, bias, *tabs]
    aliases = {}
    if attn:
        tk, depth, layer, kvt = attn
        per = tk // tm
        seq = tabs[0].shape[0]
        out_shape += (jax.ShapeDtypeStruct((M, 2 * KV_WIDTH), BF16),
                      jax.ShapeDtypeStruct((M // tk, KV_WIDTH, tk), BF16),
                      jax.ShapeDtypeStruct((M // LANES, KV_WIDTH, LANES), BF16),
                      jax.ShapeDtypeStruct((M // seq, depth, 4 * KV_WIDTH, seq), F32))
        out_specs += (pl.BlockSpec((tm, 2 * KV_WIDTH), row),
                      pl.BlockSpec((1, KV_WIDTH, tm), lambda i: (i // per, 0, i % per)),
                      pl.BlockSpec((tm // LANES, KV_WIDTH, LANES), lambda i: (i, 0, 0)),
                      pl.BlockSpec((None, None, 4 * KV_WIDTH, tm), lambda i: (i // nt, layer, 0, i % nt)))
        in_specs.append(pl.BlockSpec(memory_space=pl.ANY))
        operands.append(kvt)
        aliases = {len(operands) - 1: len(out_shape) - 1}
    return pl.pallas_call(
        functools.partial(_inproj_kernel, A_DIM ** -0.5 * (LOG2_E if attn else 1.0), len(operands)),
        grid=(M // tm,),
        in_specs=in_specs,
        out_specs=out_specs,
        out_shape=out_shape,
        input_output_aliases=aliases,
        compiler_params=_params("parallel"),
        name="inproj",
    )(*operands)


def _mlstm_kernel(q_ref, k_ref, v_ref, og_ref, misc_ref, gt_ref, mg_ref, c0_ref, n0_ref, m0_ref,
                  hm_ref, c_ref, n_ref, m_ref):
    L = q_ref.shape[1]

    @pl.when(pl.program_id(1) == 0)
    def _():
        c_ref[...] = c0_ref[...]
        n_ref[...] = n0_ref[...]
        m_ref[...] = m0_ref[...]

    gc = misc_ref[0]
    gr = gt_ref[0]
    row = lax.broadcasted_iota(jnp.int32, (L, L), 0)
    col = lax.broadcasted_iota(jnp.int32, (L, L), 1)
    causal = row >= col
    b_col = _dot_exact(causal.astype(F32), gc)
    b_row = _dot_exact(gr, (row <= col).astype(F32))
    for h in range(M_HEADS):
        sl = slice(h * M_DIM, (h + 1) * M_DIM)
        q, k, v = q_ref[0, :, sl], k_ref[0, :, sl], v_ref[0, :, sl]
        c_prev, n_prev, m_prev = c_ref[0, h], n_ref[0, h:h + 1, :], m_ref[0, h:h + 1, 0:1]
        bc = b_col[:, M_HEADS + h:M_HEADS + h + 1]
        ic = gc[:, h:h + 1]
        br = b_row[M_HEADS + h:M_HEADS + h + 1, :]
        ir = gr[h:h + 1, :]
        d = jnp.where(causal, bc - br + ir, NEG)
        m_inter = bc + m_prev
        m_t = jnp.maximum(m_inter, jnp.max(d, axis=-1, keepdims=True))
        w_inter = jnp.exp(m_inter - m_t)
        p = _dot_nt(q, k) * jnp.exp(d - m_t)
        num = w_inter * _dot(q, c_prev.astype(BF16)) + _dot(p.astype(BF16), v)
        den = (w_inter * jnp.sum(q.astype(F32) * n_prev, axis=-1, keepdims=True)
               + jnp.sum(p, axis=-1, keepdims=True))
        hh = num / jnp.maximum(jnp.abs(den), jnp.exp(-m_t))
        m_new = m_t[L - 1:L, :]
        w_new = jnp.exp(bc[L - 1:L, :] - bc + ic - m_new)
        decay = jnp.exp(m_inter[L - 1:L, :] - m_new)
        kf, vf = k.astype(F32), v.astype(F32)
        c_ref[0, h] = decay * c_prev + lax.dot_general(k, (w_new * vf).astype(BF16), _TN,
                                                       preferred_element_type=F32)
        n_ref[0, h:h + 1, :] = decay * n_prev + jnp.sum(w_new * kf, axis=0, keepdims=True)
        m_ref[0, h:h + 1, :] = jnp.broadcast_to(m_new, (1, M_DIM))
        hn = hh * lax.rsqrt(jnp.mean(hh * hh, axis=-1, keepdims=True) + EPS)
        hm_ref[0, :, sl] = ((hn * mg_ref[:, sl]) * og_ref[0, :, sl]).astype(BF16)


def _mlstm_t_kernel(q_ref, k_ref, v_ref, og_ref, misc_ref, gt_ref, mg_ref, c0_ref, n0_ref, m0_ref,
                    hm_ref, c_ref, n_ref, m_ref):
    L = q_ref.shape[1]

    @pl.when(pl.program_id(1) == 0)
    def _():
        c_ref[...] = c0_ref[...]
        n_ref[...] = n0_ref[...]
        m_ref[...] = m0_ref[...]

    gc = misc_ref[0]
    gr = gt_ref[0]
    src = lax.broadcasted_iota(jnp.int32, (L, L), 0)
    tgt = lax.broadcasted_iota(jnp.int32, (L, L), 1)
    causal = src <= tgt
    b_col = _dot_exact((src >= tgt).astype(F32), gc)
    b_row = _dot_exact(gr, causal.astype(F32))
    contract_00 = (((0,), (1,)), ((), ()))
    for h in range(M_HEADS):
        sl = slice(h * M_DIM, (h + 1) * M_DIM)
        q, k, v = q_ref[0, :, sl], k_ref[0, :, sl], v_ref[0, :, sl]
        c_prev, n_prev, m_prev = c_ref[0, h], n_ref[0, h:h + 1, :], m_ref[0, h:h + 1, 0:1]
        br = b_row[M_HEADS + h:M_HEADS + h + 1, :]
        ir = gr[h:h + 1, :]
        src_term = b_col[:, M_HEADS + h:M_HEADS + h + 1] - gc[:, h:h + 1]
        d = jnp.where(causal, br - src_term, NEG)
        m_inter = br + m_prev
        m_t = jnp.maximum(m_inter, jnp.max(d, axis=0, keepdims=True))
        w_inter = jnp.exp(m_inter - m_t)
        p = _dot_nt(k, q) * jnp.exp(d - m_t)
        k_t = k.astype(F32).T
        v_t = v.astype(F32).T.astype(BF16)
        num = (w_inter * lax.dot_general(c_prev.astype(BF16), q, contract_00, preferred_element_type=F32)
               + _dot(v_t, p.astype(BF16)))
        q_n = _dot_nt(jnp.broadcast_to(n_prev, (8, M_DIM)).astype(BF16), q)[0:1, :]
        den = w_inter * q_n + jnp.sum(p, axis=0, keepdims=True)
        hh = num / jnp.maximum(jnp.abs(den), jnp.exp(-m_t))
        m_new = m_t[:, L - 1:L]
        w_new = jnp.exp(br[:, L - 1:L] - br + ir - m_new)
        decay = jnp.exp(m_inter[:, L - 1:L] - m_new)
        c_ref[0, h] = decay * c_prev + _dot((k_t * w_new).astype(BF16), v)
        n_ref[0, h:h + 1, :] = decay * n_prev + _dot(jnp.broadcast_to(w_new, (8, L)).astype(BF16), k)[0:1, :]
        m_ref[0, h:h + 1, :] = jnp.broadcast_to(m_new, (1, M_DIM))
        hn = hh * lax.rsqrt(jnp.mean(hh * hh, axis=0, keepdims=True) + EPS)
        hm_ref[0, :, sl] = ((hn.T * mg_ref[:, sl]) * og_ref[0, :, sl]).astype(BF16)


def _mlstm(qkv, og, misc, gt, mg, c0, n0, m0, L, transposed):
    B, T, _ = qkv.shape
    seq = lambda j: pl.BlockSpec((1, L, M_WIDTH), lambda b, c: (b, c, j))
    st4 = pl.BlockSpec((1, M_HEADS, M_DIM, M_DIM), lambda b, c: (b, 0, 0, 0))
    st3 = pl.BlockSpec((1, M_HEADS, M_DIM), lambda b, c: (b, 0, 0))
    return pl.pallas_call(
        _mlstm_t_kernel if transposed else _mlstm_kernel,
        grid=(B, T // L),
        in_specs=[seq(0), seq(1), seq(2), seq(0),
                  pl.BlockSpec((1, L, LANES), lambda b, c: (b, c, 0)),
                  pl.BlockSpec((1, 2 * M_HEADS, L), lambda b, c: (b, 0, c)),
                  pl.BlockSpec((1, M_WIDTH), lambda b, c: (0, 0)),
                  st4, st3, st3],
        out_specs=(seq(0), st4, st3, st3),
        out_shape=(jax.ShapeDtypeStruct((B, T, M_WIDTH), BF16),
                   jax.ShapeDtypeStruct(c0.shape, F32),
                   jax.ShapeDtypeStruct(n0.shape, F32),
                   jax.ShapeDtypeStruct(m0.shape, F32)),
        compiler_params=_params("parallel", "arbitrary"),
        name="mlstm",
    )(qkv, qkv, qkv, og, misc, gt, mg, c0, n0, m0)


def _cmp_accumulate(load, w_ref, s):
    acc = None
    for r in range(0, D_CMP, 2):
        lhs = jnp.concatenate([load(r), load(r + 1)], axis=1).astype(BF16)
        term = _dot(lhs, w_ref[s, r // 2])
        acc = term if acc is None else acc + term
    return acc


def _cmp_prompt_kernel(k_ref, v_ref, w_ref, fs_ref):
    n = k_ref.shape[0] // D_CMP
    for s, ref in enumerate((k_ref, v_ref)):
        fs_ref[s] = _cmp_accumulate(lambda r: ref[pl.ds(r, n, stride=D_CMP), :], w_ref, s)


def _cmp_prompt(rows, w_cmp, tr):
    M = rows.shape[0]
    return pl.pallas_call(
        _cmp_prompt_kernel,
        grid=(M // tr,),
        in_specs=[pl.BlockSpec((tr, KV_WIDTH), lambda i: (i, 0)), pl.BlockSpec((tr, KV_WIDTH), lambda i: (i, 1)),
                  pl.BlockSpec(w_cmp.shape, lambda i: (0, 0, 0, 0))],
        out_specs=pl.BlockSpec((2, tr // D_CMP, 2 * LANES), lambda i: (0, i, 0)),
        out_shape=jax.ShapeDtypeStruct((2, M // D_CMP, 2 * LANES), F32),
        compiler_params=_params("parallel"),
        name="cmp_prompt",
    )(rows, rows, w_cmp)


def _cmp_sample_kernel(n_group, pt_ref, *refs):
    page_refs, w_ref, fs_ref, rows_s = refs[:n_group], refs[n_group], refs[n_group + 1], refs[n_group + 2:]
    page = page_refs[0].shape[2]
    n = n_group * page // D_CMP
    for s in range(2):
        for i, p in enumerate(page_refs):
            rows_s[s][i * page:(i + 1) * page, :] = p[s].T
    for s in range(2):
        fs_ref[s, 0] = _cmp_accumulate(lambda r: rows_s[s][pl.ds(r, n, stride=D_CMP), :], w_ref, s)


def _cmp_sample(cache_t, page_table, layer, w_cmp, n_group):
    B, n_pages = page_table.shape
    page = cache_t.shape[4]
    per_page = page // D_CMP

    def page_spec(i):
        return pl.BlockSpec((None, None, 2, KV_WIDTH, page),
                            lambda b, p, pt: (pt[b, p * n_group + i], layer, 0, 0, 0))

    grid_spec = pltpu.PrefetchScalarGridSpec(
        num_scalar_prefetch=1,
        grid=(B, n_pages // n_group),
        in_specs=[page_spec(i) for i in range(n_group)]
        + [pl.BlockSpec(w_cmp.shape, lambda b, p, pt: (0, 0, 0, 0))],
        out_specs=pl.BlockSpec((2, 1, n_group * per_page, 2 * LANES), lambda b, p, pt: (0, b, p, 0)),
        scratch_shapes=[pltpu.VMEM((n_group * page, KV_WIDTH), F32), pltpu.VMEM((n_group * page, KV_WIDTH), F32)],
    )
    return pl.pallas_call(
        functools.partial(_cmp_sample_kernel, n_group),
        grid_spec=grid_spec,
        out_shape=jax.ShapeDtypeStruct((2, B, n_pages * per_page, 2 * LANES), F32),
        compiler_params=_params("parallel", "arbitrary"),
        name="cmp_sample",
    )(page_table, *([cache_t] * n_group), w_cmp)


def _compressed_kv(fs_ref, bias_ref, tabs, b=0):
    n = fs_ref.shape[2]
    out = []
    for s in range(2):
        fs = fs_ref[s, b]
        out.append(fs[:, 0:LANES] + pltpu.roll(fs[:, LANES:2 * LANES], n - 1, 0) + bias_ref[s:s + 1, :])
    return _rope128(out[0], *tabs), out[1]


def _masked_softmax(s, valid, axis):
    m = jnp.max(jnp.where(valid, s, NEG), axis=axis, keepdims=True)
    e = jnp.where(valid, jnp.exp(s - m), 0.0)
    l = jnp.sum(e, axis=axis, keepdims=True)
    return e * (1.0 / jnp.where(l > 0.0, l, 1.0))


def _bias_softmax(s, axis, keep=None):
    m = jnp.max(s, axis=axis, keepdims=True)
    e = jnp.exp2(s - m)
    scale = 1.0 / jnp.sum(e, axis=axis, keepdims=True)
    return e * (scale if keep is None else scale * keep)


def _split3(x):
    h1 = x.astype(BF16)
    r1 = x - h1.astype(F32)
    h2 = r1.astype(BF16)
    return h1, h2, (r1 - h2.astype(F32)).astype(BF16)


def _dot_split3(a, x):
    h1, h2, h3 = _split3(x)
    return _dot(a, h1) + _dot(a, h2) + _dot(a, h3)


def _dot_split3_rhs(x, a):
    h1, h2, h3 = _split3(x)
    return _dot(h1, a) + _dot(h2, a) + _dot(h3, a)


def _top_blocks(score, n_top, axis):
    pos = lax.broadcasted_iota(jnp.int32, score.shape, axis).astype(F32)
    picks = []
    for _ in range(n_top):
        mx = jnp.max(score, axis=axis, keepdims=True)
        idx = jnp.min(jnp.where(score == mx, pos, float(score.shape[axis])), axis=axis, keepdims=True)
        picks.append(idx)
        score = jnp.where(pos == idx, NEG, score)
    return picks, score


def _nsa_prompt_kernel(tk, q_ref, gt_ref, fs_ref, cb_ref, kc_c, kc_s1, kc_s2, ovt_ref,
                       ks_ref, kw_ref, vst_ref, vwt_ref, oa_ref,
                       kc_s, vct_s, sel_s, m_s, acc_s, sc_s, part_s, oc_s, imp_s):
    n_cmp = fs_ref.shape[2]
    n_sel = ovt_ref.shape[0]
    nq = Q_BLOCK
    cols = A_REP * nq
    per_tile = tk // L_SEL
    qi = pl.program_id(1)
    q0 = qi * nq

    @pl.when(qi == 0)
    def _():
        kc, vc = _compressed_kv(fs_ref, cb_ref, (kc_c[...], kc_s1[...], kc_s2[...]))
        kc_s[...] = kc.astype(BF16)
        vct_s[...] = vc.T.astype(BF16)

    t_pos = q0 + lax.broadcasted_iota(jnp.int32, (1, nq), 1)
    gates = gt_ref[0]

    def heads(a):
        return jnp.concatenate([a] * A_REP, axis=1)

    def all_heads(a):
        return jnp.concatenate([a] * A_HEADS, axis=1)

    cmp_end = lax.broadcasted_iota(jnp.int32, (n_cmp, 1), 0) * D_CMP + (L_CMP - 1)
    c_bias = jnp.where(cmp_end <= t_pos, 0.0, NEG)
    c_keep = jnp.where(t_pos >= L_CMP - 1, 1.0, 0.0)
    blk = lax.broadcasted_iota(jnp.int32, (n_sel, 1), 0)
    t_both = jnp.concatenate([t_pos] * A_KV, axis=1)
    cur = t_both // L_SEL
    forced = (blk == 0) | (blk == cur) | (blk == cur - 1)
    started = blk * L_SEL <= t_both
    w_start = jnp.maximum(q0 - WINDOW, 0)
    w_len = WINDOW + nq
    rel = t_pos - (w_start + lax.broadcasted_iota(jnp.int32, (w_len, 1), 0))
    w_bias = jnp.where((rel >= 0) & (rel < WINDOW), 0.0, NEG)
    n_tiles = (q0 + nq + tk - 1) // tk
    key_iota = lax.broadcasted_iota(jnp.int32, (tk, 1), 0)

    groups = range(A_KV)
    vs = [slice(g * A_DIM, (g + 1) * A_DIM) for g in groups]
    gs = [slice(g * cols, (g + 1) * cols) for g in groups]
    q_all = q_ref[...].reshape(A_HEADS * nq, LANES)

    n_cls = n_cmp // LANES if n_cmp % LANES == 0 else 1
    unit = n_cmp // n_cls
    cls = jnp.minimum(((q0 + nq) // D_CMP + unit - 1) // unit, n_cls)

    def compressed(rows):
        p_c = _bias_softmax(_dot_nt(kc_s[0:rows, :], q_all) + all_heads(c_bias[0:rows]), 0, all_heads(c_keep))
        p_c16 = p_c.astype(BF16)
        p_sums = []
        for g in groups:
            oc_s[g] = _dot(vct_s[vs[g], 0:rows], p_c16[:, gs[g]])
            p_sum = p_c[:, g * cols:g * cols + nq]
            for r in range(1, A_REP):
                p_sum = p_sum + p_c[:, g * cols + r * nq:g * cols + (r + 1) * nq]
            p_sums.append(p_sum)
        imp_s[...] = _dot_split3(ovt_ref[:, 0:rows], jnp.concatenate(p_sums, axis=1))

    for k in range(1, n_cls + 1):
        pl.when(cls == k)(functools.partial(compressed, k * unit))
    o_c = [oc_s[g] for g in groups]

    score = jnp.where(forced, FORCE_SCORE, jnp.where(started, imp_s[...], -1.0))
    _, left_all = _top_blocks(score, min(N_TOP, n_sel), 0)
    for g in groups:
        left = left_all[:, g * nq:(g + 1) * nq]
        sel_s[g] = jnp.where(left < -2.0, 0.0, NEG)

    ws = pl.multiple_of(w_start, nq)
    s_w = _dot_nt(kw_ref[0, pl.ds(ws, w_len), :], q_all) + all_heads(w_bias)
    e_w = jnp.exp2(s_w - jnp.max(s_w, axis=0, keepdims=True)).astype(BF16)
    wc = w_start // LANES
    vw_t = jnp.concatenate([vwt_ref[0, wc + c] for c in range(w_len // LANES)], axis=1)
    ones_w = jnp.ones((16, w_len), BF16)
    for g in groups:
        acc_w = _dot(jnp.concatenate([vw_t[vs[g], :], ones_w], axis=0), e_w[:, gs[g]])
        o_w = acc_w[0:A_DIM] * (1.0 / acc_w[A_DIM:A_DIM + 1])
        for r in range(A_REP):
            hd = g * A_REP + r
            base = 2 * M_HEADS + hd * N_BRANCH
            cs = slice(r * nq, (r + 1) * nq)
            part_s[hd * A_DIM:(hd + 1) * A_DIM, :] = (gates[base:base + 1] * o_c[g][:, cs]
                                                      + gates[base + 2:base + 3] * o_w[:, cs])

    m_s[...] = jnp.full(m_s.shape, NEG, F32)
    acc_s[...] = jnp.zeros(acc_s.shape, F32)
    ones_rows = jnp.ones((acc_s.shape[1] - A_DIM, tk), BF16)

    last_tile = ks_ref.shape[1] // tk - 1

    def scores(kt):
        k0 = pl.multiple_of(jnp.minimum(kt, last_tile) * tk, tk)
        return _dot_nt(ks_ref[0, pl.ds(k0, tk), :], q_all)

    def attend(kt, slot):
        kc = jnp.minimum(kt, last_tile)
        causal = kt * tk + key_iota <= t_pos
        bias = []
        for g in groups:
            chosen = jnp.concatenate(
                [jnp.broadcast_to(sel_s[g, pl.ds(kc * per_tile + i, 1), :], (L_SEL, nq)) for i in range(per_tile)],
                axis=0)
            bias.append(heads(jnp.where(causal, chosen, NEG)))
        s = sc_s[slot] + jnp.concatenate(bias, axis=1)
        m_old = m_s[...]
        m_new = jnp.maximum(m_old, jnp.max(s, axis=0, keepdims=True))
        e = jnp.exp2(s - m_new)
        a = jnp.exp2(m_old - m_new)
        e16 = e.astype(BF16)
        for g in groups:
            vals = jnp.concatenate([vst_ref[0, kc, vs[g], :], ones_rows], axis=0)
            acc_s[g] = a[:, gs[g]] * acc_s[g] + _dot(vals, e16[:, gs[g]])
        m_s[...] = m_new

    sc_s[0] = scores(0)

    def pair(j, carry):
        sc_s[1] = scores(2 * j + 1)
        attend(2 * j, 0)
        sc_s[0] = scores(2 * j + 2)
        attend(2 * j + 1, 1)
        return carry

    lax.fori_loop(0, n_tiles // 2, pair, 0)

    @pl.when(n_tiles % 2 == 1)
    def _():
        attend(n_tiles - 1, 0)

    heads_out = []
    for g in groups:
        l = acc_s[g, A_DIM:A_DIM + 1, :]
        o_s = acc_s[g, 0:A_DIM, :] * (1.0 / jnp.where(l > 0.0, l, 1.0))
        for r in range(A_REP):
            hd = g * A_REP + r
            gate = gates[2 * M_HEADS + hd * N_BRANCH + 1:2 * M_HEADS + hd * N_BRANCH + 2]
            heads_out.append(part_s[hd * A_DIM:(hd + 1) * A_DIM, :] + gate * o_s[:, r * nq:(r + 1) * nq])
    oa_ref[0] = jnp.concatenate(heads_out, axis=0).T.astype(BF16)


def _nsa_prompt(q8, gt, fs, cbias, kc_tabs, ovt, kk, vst, vwt, B, T, tk):
    nqb = T // Q_BLOCK
    n_cmp = fs.shape[2]
    n_sel = ovt.shape[0]
    cols = A_REP * Q_BLOCK
    const = lambda a: pl.BlockSpec(a.shape, lambda b, i: (0,) * a.ndim)
    keys = lambda j: pl.BlockSpec((1, T, LANES), lambda b, i: (b, 0, j))
    per_b = lambda a: pl.BlockSpec((1,) + a.shape[1:], lambda b, i: (b, 0, 0, 0))
    return pl.pallas_call(
        functools.partial(_nsa_prompt_kernel, tk),
        grid=(B, nqb),
        in_specs=[pl.BlockSpec((A_HEADS, Q_BLOCK, LANES), lambda b, i: (0, b * nqb + i, 0)),
                  pl.BlockSpec((1, gt.shape[1], Q_BLOCK), lambda b, i: (b, 0, i)),
                  pl.BlockSpec((2, 1, n_cmp, 2 * LANES), lambda b, i: (0, b, 0, 0)),
                  const(cbias), const(kc_tabs[0]), const(kc_tabs[1]), const(kc_tabs[2]),
                  const(ovt), keys(0), keys(1), per_b(vst), per_b(vwt)],
        out_specs=pl.BlockSpec((1, Q_BLOCK, A_WIDTH), lambda b, i: (b, i, 0)),
        out_shape=jax.ShapeDtypeStruct((B, T, A_WIDTH), BF16),
        scratch_shapes=[pltpu.VMEM((n_cmp, LANES), BF16), pltpu.VMEM((LANES, n_cmp), BF16),
                        pltpu.VMEM((A_KV, n_sel, Q_BLOCK), F32),
                        pltpu.VMEM((1, A_KV * cols), F32),
                        pltpu.VMEM((A_KV, A_DIM + 16, cols), F32),
                        pltpu.VMEM((2, tk, A_KV * cols), F32),
                        pltpu.VMEM((A_WIDTH, Q_BLOCK), F32),
                        pltpu.VMEM((A_KV, A_DIM, cols), F32),
                        pltpu.VMEM((n_sel, A_KV * Q_BLOCK), F32)],
        compiler_params=_params("parallel", "arbitrary"),
        name="nsa_prompt",
    )(q8, gt, fs, cbias, *kc_tabs, ovt, kk, kk, vst, vwt)


def _nsa_sample_select_kernel(q_pos, n_sel, q_ref, fs_ref, cb_ref, kc_c, kc_s1, kc_s2, ov_ref, oc_ref, idx_ref):
    n_seq, n_cmp = fs_ref.shape[1], fs_ref.shape[2]
    tabs = (kc_c[...], kc_s1[...], kc_s2[...])
    cmp_end = lax.broadcasted_iota(jnp.int32, (A_HEADS, n_cmp), 1) * D_CMP + (L_CMP - 1)
    head = lax.broadcasted_iota(jnp.int32, (A_HEADS, n_cmp), 0)
    imps = []
    for b in range(n_seq):
        kc, vc = _compressed_kv(fs_ref, cb_ref, tabs, b)
        p_c = _masked_softmax(_dot_nt(q_ref[b], kc.astype(BF16)), cmp_end <= q_pos, 1)
        oc_ref[b] = _dot(p_c.astype(BF16), vc.astype(BF16))
        p_sum = jnp.zeros((A_HEADS, n_cmp), F32)
        for g in range(A_KV):
            pg = jnp.sum(jnp.where(head // A_REP == g, p_c, 0.0), axis=0, keepdims=True)
            p_sum = jnp.where(head == g, pg, p_sum)
        imps.append(_dot_split3_rhs(p_sum, ov_ref[...]))
    imp = jnp.concatenate(imps, axis=0)
    n_pad = ov_ref.shape[1]
    blk = lax.broadcasted_iota(jnp.int32, (A_HEADS * n_seq, n_pad), 1)
    cur = q_pos // L_SEL
    forced = (blk == 0) | (blk == cur) | (blk == cur - 1)
    score = jnp.where(forced, FORCE_SCORE, jnp.where(blk * L_SEL <= q_pos, imp, -1.0))
    score = jnp.where(blk < n_sel, score, 2 * NEG)
    picks, _ = _top_blocks(score, min(N_TOP, n_sel), 1)
    lane = lax.broadcasted_iota(jnp.int32, (A_HEADS * n_seq, LANES), 1)
    out = jnp.zeros((A_HEADS * n_seq, LANES), jnp.int32)
    for j, idx in enumerate(picks):
        out = jnp.where(lane == j, idx.astype(jnp.int32), out)
    for b in range(n_seq):
        idx_ref[b] = out[b * A_HEADS:(b + 1) * A_HEADS]


def _nsa_sample_select(q8, fs, cbias, kc_tabs, ov, q_pos, n_sel, n_seq):
    B = q8.shape[0]
    n_cmp = fs.shape[2]
    const = lambda a: pl.BlockSpec(a.shape, lambda b: (0,) * a.ndim)
    per_b = pl.BlockSpec((n_seq, A_HEADS, LANES), lambda b: (b, 0, 0))
    return pl.pallas_call(
        functools.partial(_nsa_sample_select_kernel, q_pos, n_sel),
        grid=(B // n_seq,),
        in_specs=[per_b, pl.BlockSpec((2, n_seq, n_cmp, 2 * LANES), lambda b: (0, b, 0, 0)),
                  const(cbias), const(kc_tabs[0]), const(kc_tabs[1]), const(kc_tabs[2]), const(ov)],
        out_specs=(per_b, per_b),
        out_shape=(jax.ShapeDtypeStruct((B, A_HEADS, LANES), F32),
                   jax.ShapeDtypeStruct((B, A_HEADS, LANES), jnp.int32)),
        compiler_params=_params("parallel"),
        name="nsa_sample_select",
    )(q8, fs, cbias, *kc_tabs, ov)


def _r16(a):
    return a.astype(BF16).astype(F32)


def _nsa_sample_attend_kernel(q_pos, n_top, pt_ref, ix_ref, q_ref, new_ref, newc_ref, oc_ref, misc_ref, win_ref,
                              *refs):
    page_refs, (oa_ref, nwin_ref) = refs[:A_KV * n_top], refs[A_KV * n_top:]
    b = pl.program_id(0)
    q = q_ref[0]
    qf = q.astype(F32)
    head = lax.broadcasted_iota(jnp.int32, (A_HEADS, 1), 0)
    new = new_ref[0]
    page = page_refs[0].shape[2]
    per_page = page // L_SEL
    lane = lax.broadcasted_iota(jnp.int32, (A_HEADS, page), 1)
    s_tok = jnp.sum(qf * _r16(new[0:1, :]), axis=-1, keepdims=True)

    o_s = jnp.zeros((A_HEADS, LANES), F32)
    for g in range(A_KV):
        parts, n_fresh = [], 0
        for j in range(n_top):
            blk = ix_ref[b, g, j]
            fresh = blk * L_SEL >= q_pos
            limit = jnp.where(fresh, -1, q_pos)
            s = _dot(q, page_refs[g * n_top + j][0].astype(BF16))
            s_pos = (blk // per_page) * page + lane
            ok = jnp.where(lane // L_SEL == blk % per_page, s_pos, limit + 1) <= limit
            parts.append(jnp.where(ok, s, NEG))
            n_fresh = n_fresh + jnp.where(fresh, 1, 0)
        s_all = jnp.concatenate(parts, axis=1)
        has_tok = jnp.where(head >= 0, n_fresh, 0) > 0
        m = jnp.maximum(jnp.max(s_all, axis=-1, keepdims=True), jnp.where(has_tok, s_tok, NEG))
        e = jnp.exp(s_all - m)
        e_tok = jnp.where(has_tok, jnp.exp(s_tok - m), 0.0)
        den = jnp.sum(e, axis=-1, keepdims=True) + e_tok
        e16 = e.astype(BF16)
        acc = _r16(e_tok) * _r16(new[1:2, :])
        for j in range(n_top):
            acc = acc + _dot_nt(e16[:, j * page:(j + 1) * page], page_refs[g * n_top + j][1].astype(BF16))
        o_s = jnp.where(head // A_REP == g, acc / den, o_s)

    wb = win_ref.shape[2]
    kwt, vwt = win_ref[0], win_ref[1]
    s_old = _dot(q, kwt.astype(BF16))
    s_new = jnp.sum(qf * _r16(new[2:3, :]), axis=-1, keepdims=True)
    rel = wb - lax.broadcasted_iota(jnp.int32, (A_HEADS, wb), 1)
    ok = (rel >= 0) & (rel < WINDOW) & (q_pos - rel >= 0)
    m = jnp.maximum(jnp.max(jnp.where(ok, s_old, NEG), axis=-1, keepdims=True), s_new)
    e_old = jnp.where(ok, jnp.exp(s_old - m), 0.0)
    e_new = jnp.exp(s_new - m)
    den = jnp.sum(e_old, axis=-1, keepdims=True) + e_new
    o_w = _dot_nt((e_old / den).astype(BF16), vwt.astype(BF16)) + _r16(e_new / den) * _r16(new[3:4, :])
    gates = misc_ref[0]
    oa_ref[0] = gates[:, 0:1] * oc_ref[0] + gates[:, 1:2] * o_s + gates[:, 2:3] * o_w
    newc = newc_ref[0]
    last = lax.broadcasted_iota(jnp.int32, (LANES, wb), 1) == wb - 1
    nwin_ref[0] = jnp.where(last, newc[:, 2:3], pltpu.roll(kwt, wb - 1, 1))
    nwin_ref[1] = jnp.where(last, newc[:, 3:4], pltpu.roll(vwt, wb - 1, 1))


def _nsa_sample_attend(page_table, idx, q8, new_rows, new_cols, o_c, gate_rows, win_t, cache_t, layer, q_pos):
    B, n_pages = page_table.shape
    n_top = idx.shape[2]
    page = cache_t.shape[4]
    per_page = page // L_SEL
    wb = win_t.shape[4]

    def page_spec(g, j):
        def index(b, pt, ix):
            pg = jnp.clip(ix[b, g, j] // per_page, 0, n_pages - 1)
            return (pt[b, pg], layer, 1, 0, 0)
        return pl.BlockSpec((None, None, 2, KV_WIDTH, page), index)

    per_b = pl.BlockSpec((1, A_HEADS, LANES), lambda b, pt, ix: (b, 0, 0))
    pages = [page_spec(g, j) for g in range(A_KV) for j in range(n_top)]
    grid_spec = pltpu.PrefetchScalarGridSpec(
        num_scalar_prefetch=2,
        grid=(B,),
        in_specs=[per_b, per_b, pl.BlockSpec((1, KV_WIDTH, A_HEADS), lambda b, pt, ix: (b, 0, 0)), per_b, per_b,
                  pl.BlockSpec((None, None, 2, KV_WIDTH, wb), lambda b, pt, ix: (b, layer, 0, 0, 0))] + pages,
        out_specs=(per_b, pl.BlockSpec((None, 2, KV_WIDTH, wb), lambda b, pt, ix: (b, 0, 0, 0))),
    )
    return pl.pallas_call(
        functools.partial(_nsa_sample_attend_kernel, q_pos, n_top),
        grid_spec=grid_spec,
        out_shape=(jax.ShapeDtypeStruct((B, A_HEADS, LANES), F32),
                   jax.ShapeDtypeStruct((B, 2, KV_WIDTH, wb), F32)),
        compiler_params=_params("parallel"),
        name="nsa_sample_attend",
    )(page_table, idx, q8, new_rows, new_cols, o_c, gate_rows, win_t, *([cache_t] * len(pages)))


def _ffn_kernel(final, tf, x_ref, hm_ref, oa_ref, wo_ref, nf_ref, wg_ref, wu_ref, wd_ref, nfin_ref, out_ref, act_s):
    half = hm_ref.shape[1]
    x1 = x_ref[...] + _dot(hm_ref[...], wo_ref[0:half, :]) + _dot(oa_ref[...], wo_ref[half:, :])
    ms = jnp.mean(x1 * x1, axis=-1, keepdims=True)
    xn = ((x1 * lax.rsqrt(ms + EPS)) * nf_ref[...]).astype(BF16)
    for c in range(wg_ref.shape[1] // tf):
        sl = slice(c * tf, (c + 1) * tf)
        gate = _dot(xn, wg_ref[:, sl])
        act_s[:, sl] = ((gate * _sigmoid(gate)) * _dot(xn, wu_ref[:, sl])).astype(BF16)
    y = x1 + _dot(act_s[...], wd_ref[...])
    if final:
        ms = jnp.mean(y * y, axis=-1, keepdims=True)
        y = (y * lax.rsqrt(ms + EPS)) * nfin_ref[...]
    out_ref[...] = y


def _ffn(x, hm, oa, wo, nf, wg, wu, wd, nfin, final, tm, tf):
    M, D = x.shape
    F = wg.shape[1]
    row = lambda w: pl.BlockSpec((tm, w), lambda i: (i, 0))
    const = lambda a: pl.BlockSpec(a.shape, lambda i: (0,) * a.ndim, pipeline_mode=pl.Buffered(1))
    return pl.pallas_call(
        functools.partial(_ffn_kernel, final, tf),
        grid=(M // tm,),
        in_specs=[row(D), row(hm.shape[1]), row(oa.shape[1]), const(wo), const(nf),
                  const(wg), const(wu), const(wd), const(nfin)],
        out_specs=row(D),
        out_shape=jax.ShapeDtypeStruct((M, D), F32),
        scratch_shapes=[pltpu.VMEM((tm, F), BF16)],
        compiler_params=_params("parallel"),
        name="ffn",
    )(x, hm, oa, wo, nf, wg, wu, wd, nfin)


def _permute_w_in(w):
    a = 4 * M_WIDTH
    b = a + 2 * M_HEADS
    c = b + A_WIDTH + 6 * KV_WIDTH
    pad = jnp.zeros((w.shape[0], LANES - N_MISC), w.dtype)
    return jnp.concatenate([w[:, :a], w[:, b:c], w[:, a:b], w[:, c:], pad], axis=1).astype(BF16)


def _cmp_weights(wk, wv):
    def one(w):
        z = jnp.zeros((D_CMP, A_DIM, A_DIM), w.dtype)
        halves = []
        for part in (w[:D_CMP], w[D_CMP:]):
            top = jnp.concatenate([part, z], axis=2)
            bot = jnp.concatenate([z, part], axis=2)
            halves.append(jnp.concatenate([top, bot], axis=1))
        return jnp.concatenate(halves, axis=2)
    return jnp.stack([one(wk), one(wv)]).astype(BF16).reshape(2, D_CMP // 2, 2 * KV_WIDTH, 2 * LANES)


def _overlap(n_cmp_pad, n_sel_pad, n_cmp, n_sel):
    lo_c = np.arange(n_cmp_pad)[:, None] * D_CMP
    lo_s = np.arange(n_sel_pad)[None, :] * L_SEL
    ov = np.clip(np.minimum(lo_c + L_CMP, lo_s + L_SEL) - np.maximum(lo_c, lo_s), 0, None).astype(np.float32) / L_CMP
    ov[n_cmp:, :] = 0.0
    ov[:, n_sel:] = 0.0
    return jnp.asarray(ov)


def _pick(n, prefs):
    for p in prefs:
        if n % p == 0:
            return p
    return n


def kernel(x_prompt, x_sample, cache_nsa_kv, page_table, state_win_kv, state_mlstm_c, state_mlstm_n, state_mlstm_m, norm_mix, w_in, b_if, m_norm, w_cmp_k, b_cmp_k, w_cmp_v, b_cmp_v, w_out, norm_ffn, w_gate, w_up, w_down, norm_final):
    B, T, D = x_prompt.shape
    Bs, Ts, _ = x_sample.shape
    depth = w_in.shape[0]
    n_pool, _, n_slots, page = cache_nsa_kv.shape[:4]
    n_pages = page_table.shape[1]
    past = n_pages * page
    wb = state_win_kv.shape[3]
    assert Ts == 1 and T % Q_BLOCK == 0 and T >= WINDOW + Q_BLOCK and wb == WINDOW
    assert past % page == 0 and page % L_SEL == 0 and page % D_CMP == 0
    assert ((past + Ts) // D_CMP) * D_CMP <= past

    Mp = B * T
    tm_p = _pick(Mp, (512, 256, 128))
    tm_f = _pick(Mp, (512, 256, 128))
    ff = w_gate.shape[2]
    tf = _pick(ff, (256, 128))
    l_chunk = _pick(T, (256, 128, 64))
    ls_pad = 16
    tk = _pick(T, (512, 256, 128))
    assert tk % tm_p == 0 and tm_p % LANES == 0
    n_group = _pick(n_pages, (16, 8, 4, 2, 1))
    tr = _pick(T, (2048, 1024, 512, 256))

    n_chunk_p = T // D_CMP
    n_cmp_p, n_sel_p = n_chunk_p - 1, -(-T // L_SEL)
    tabs_p = _rope_tables(jnp.arange(T))
    kc_tabs_p = _rope_tables(jnp.arange(n_chunk_p) * D_CMP)
    assert L_CMP <= 256
    ovt_p = _overlap(n_chunk_p, n_sel_p, n_cmp_p, n_sel_p).T.astype(BF16)
    n_chunk_s = (past + Ts) // D_CMP
    n_cmp_s, n_sel_s = n_chunk_s - 1, -(-(past + Ts) // L_SEL)
    n_sel_pad = -(-n_sel_s // LANES) * LANES
    tabs_s = _rope_tables(jnp.full((Bs,), past))
    kc_tabs_s = _rope_tables(jnp.arange(n_chunk_s) * D_CMP)
    ov_s = _overlap(n_chunk_s, n_sel_pad, n_cmp_s, n_sel_s).astype(BF16)
    cache_t = cache_nsa_kv.transpose(0, 1, 2, 4, 5, 3).reshape(n_pool, depth, n_slots, KV_WIDTH, page)
    win_t = state_win_kv.transpose(0, 1, 2, 4, 5, 3).reshape(Bs, depth, 2, KV_WIDTH, wb)

    xp = x_prompt.reshape(Mp, D)
    xs = x_sample.reshape(Bs, D)
    rows_s, win_p, win_s, c_p, c_s, n_p, n_s, m_p, m_s = ([] for _ in range(9))
    y_p = y_s = None
    kvt = jnp.zeros((B, depth, n_slots * KV_WIDTH, T), F32)
    nfin = norm_final.reshape(1, D)
    for l in range(depth):
        w_perm = _permute_w_in(w_in[l])
        gain = norm_mix[l].reshape(1, D)
        bias = jnp.zeros((1, LANES), F32).at[0, :2 * M_HEADS].set(b_if[l].reshape(-1))
        mg = m_norm[l].reshape(1, M_WIDTH)
        w_cmp = _cmp_weights(w_cmp_k[l], w_cmp_v[l])
        cbias = jnp.stack([jnp.tile(b_cmp_k[l], A_KV), jnp.tile(b_cmp_v[l], A_KV)])
        wo, nf = w_out[l].astype(BF16), norm_ffn[l].reshape(1, D)
        wg, wu, wd = w_gate[l].astype(BF16), w_up[l].astype(BF16), w_down[l].astype(BF16)
        final = l == depth - 1

        qkv, og, q8, rows, win, misc, kk, vst, vwt, kvt = _inproj(xp, gain, w_perm, bias, tabs_p, tm_p,
                                                                    (tk, depth, l, kvt))
        misc3 = misc.reshape(B, T, LANES)
        gt = misc3[:, :, :N_MISC].transpose(0, 2, 1)
        hm, c_new, n_new, m_new = _mlstm(
            qkv.reshape(B, T, -1), og.reshape(B, T, -1), misc3, gt, mg,
            jnp.zeros((B, M_HEADS, M_DIM, M_DIM), F32), jnp.zeros((B, M_HEADS, M_DIM), F32),
            jnp.full((B, M_HEADS, M_DIM), NEG, F32), l_chunk, True)
        fs = _cmp_prompt(rows, w_cmp, tr).reshape(2, B, n_chunk_p, 2 * LANES)
        oa = _nsa_prompt(q8, gt, fs, cbias, kc_tabs_p, ovt_p, kk.reshape(B, T, -1),
                         vst.reshape(B, T // tk, KV_WIDTH, tk), vwt.reshape(B, T // LANES, KV_WIDTH, LANES), B, T, tk)
        xp_new = _ffn(xp, hm.reshape(Mp, -1), oa.reshape(Mp, -1), wo, nf, wg, wu, wd, nfin, final, tm_f, tf)
        if final:
            y_p = xp_new
        xp = xp_new
        win_last = win.reshape(B, T, 2 * KV_WIDTH)[:, T - wb:]
        win_p.append(win_last.reshape(B, wb, 2, A_KV, A_DIM).transpose(0, 2, 1, 3, 4))
        c_p.append(c_new); n_p.append(n_new); m_p.append(m_new[:, :, 0])

        qkv, og, q8, rows, win, misc = _inproj(xs, gain, w_perm, bias, tabs_s, Bs)
        pad_t = lambda a: jnp.pad(a[:, None, :], ((0, 0), (0, ls_pad - 1), (0, 0)))
        inert = jnp.zeros((ls_pad, LANES), F32).at[1:, :M_HEADS].set(NEG)
        misc_pad = pad_t(misc) + inert[None]
        gt = misc_pad[:, :, :2 * M_HEADS].transpose(0, 2, 1)
        m0 = jnp.broadcast_to(state_mlstm_m[:, l, :, None].astype(F32), (Bs, M_HEADS, M_DIM))
        hm, c_new, n_new, m_new = _mlstm(
            pad_t(qkv), pad_t(og), misc_pad, gt, mg,
            state_mlstm_c[:, l].astype(F32), state_mlstm_n[:, l].astype(F32), m0, ls_pad, False)
        hm = hm[:, 0]
        fs = _cmp_sample(cache_t, page_table, l, w_cmp, n_group)
        q8s = q8.transpose(1, 0, 2)
        o_c, idx = _nsa_sample_select(q8s, fs, cbias, kc_tabs_s, ov_s, past, n_sel_s, _pick(Bs, (4, 2, 1)))
        idx = idx[:, :A_KV, :min(N_TOP, n_sel_s)]
        new_rows = jnp.pad(jnp.stack([rows[:, 2 * LANES:3 * LANES], rows[:, 3 * LANES:], win[:, :LANES],
                                      win[:, LANES:]], axis=1), ((0, 0), (0, A_HEADS - 4), (0, 0)))
        gate_rows = jnp.pad(misc[:, 2 * M_HEADS:N_MISC].reshape(Bs, A_HEADS, N_BRANCH),
                            ((0, 0), (0, 0), (0, LANES - N_BRANCH)))
        oa8, nwin = _nsa_sample_attend(page_table, idx, q8s, new_rows, new_rows.transpose(0, 2, 1), o_c, gate_rows,
                                       win_t, cache_t, l, past)
        grp = (jnp.arange(A_HEADS) // A_REP)[None, :, None, None]
        oa = jnp.take_along_axis(oa8.reshape(Bs, A_HEADS, A_KV, A_DIM), jnp.broadcast_to(grp, (Bs, A_HEADS, 1, A_DIM)),
                                 axis=2).reshape(Bs, A_WIDTH).astype(BF16)
        xs_new = _ffn(xs, hm, oa, wo, nf, wg, wu, wd, nfin, final, Bs, tf)
        if final:
            y_s = xs_new
        xs = xs_new
        rows_s.append(rows.reshape(Bs, Ts, n_slots, A_KV, A_DIM))
        win_s.append(nwin.reshape(Bs, 2, A_KV, A_DIM, wb).transpose(0, 1, 4, 2, 3))
        c_s.append(c_new); n_s.append(n_new); m_s.append(m_new[:, :, 0])

    return (y_p.reshape(B, T, D), y_s.reshape(Bs, Ts, D),
            kvt.reshape(B, depth, n_slots, A_KV, A_DIM, T).transpose(0, 5, 1, 2, 3, 4), jnp.stack(rows_s, axis=2),
            jnp.stack(win_p, axis=1), jnp.stack(win_s, axis=1),
            jnp.stack(c_p, axis=1), jnp.stack(c_s, axis=1),
            jnp.stack(n_p, axis=1), jnp.stack(n_s, axis=1),
            jnp.stack(m_p, axis=1), jnp.stack(m_s, axis=1))
```

```python
import functools

import jax
import jax.numpy as jnp
import numpy as np
from jax import lax
from jax.experimental import pallas as pl
from jax.experimental.pallas import tpu as pltpu

F32 = jnp.float32
BF16 = jnp.bfloat16

M_HEADS = 4
M_DIM = 128
M_WIDTH = M_HEADS * M_DIM
A_HEADS = 8
A_DIM = 64
A_KV = 2
A_REP = A_HEADS // A_KV
A_WIDTH = A_HEADS * A_DIM
KV_WIDTH = A_KV * A_DIM
L_CMP = 32
D_CMP = 16
L_SEL = 64
N_TOP = 16
WINDOW = 512
Q_BLOCK = 128
N_BRANCH = 3
ROPE_THETA = 500000.0
ROPE_DIM = A_DIM // 4
EPS = 1e-6
NEG = -1e30
LOG2_E = 1.4426950408889634
FORCE_SCORE = 1e4
LANES = 128
N_MISC = 2 * M_HEADS + N_BRANCH * A_HEADS
D_PERM = 4 * M_WIDTH + A_WIDTH + 6 * KV_WIDTH + LANES
VMEM_LIMIT = 56 * 1024 * 1024

_NT = (((1,), (1,)), ((), ()))
_TN = (((0,), (0,)), ((), ()))


def _params(*sem):
    return pltpu.CompilerParams(dimension_semantics=sem, vmem_limit_bytes=VMEM_LIMIT)


def _sigmoid(x):
    return 1.0 / (1.0 + jnp.exp(-x))


def _dot(a, b):
    return jnp.dot(a, b, preferred_element_type=F32)


def _dot_exact(a, b):
    return jnp.dot(a, b, preferred_element_type=F32, precision=lax.Precision.HIGHEST)


def _dot_nt(a, b):
    return lax.dot_general(a, b, _NT, preferred_element_type=F32)


def _rope128(v, c, s1, s2):
    half = ROPE_DIM // 2
    return v * c + pltpu.roll(v, LANES - half, 1) * s1 + pltpu.roll(v, half, 1) * s2


def _rope_tables(pos):
    half = ROPE_DIM // 2
    inv = ROPE_THETA ** (-jnp.arange(half, dtype=F32) / half)
    ang = pos.astype(F32)[:, None] * inv[None, :]
    cos, sin = jnp.cos(ang), jnp.sin(ang)
    n = pos.shape[0]
    one = jnp.ones((n, A_DIM - ROPE_DIM), F32)
    zero = jnp.zeros((n, A_DIM - ROPE_DIM), F32)
    zh = jnp.zeros((n, half), F32)
    c = jnp.concatenate([cos, cos, one], axis=1)
    s1 = jnp.concatenate([-sin, zh, zero], axis=1)
    s2 = jnp.concatenate([zh, sin, zero], axis=1)
    tile = lambda a: jnp.concatenate([a, a], axis=1)
    return tile(c), tile(s1), tile(s2)


def _inproj_kernel(q_scale, n_in, *refs):
    x_ref, g_ref, w_ref, bias_ref, rc_ref, rs1_ref, rs2_ref = refs[:7]
    om_ref, og_ref, oq_ref, orows_ref, owin_ref, omisc_ref = refs[n_in:n_in + 6]
    attn_refs = refs[n_in + 6:]
    x = x_ref[...]
    ms = jnp.mean(x * x, axis=-1, keepdims=True)
    xn = ((x * lax.rsqrt(ms + EPS)) * g_ref[...]).astype(BF16)
    c, s1, s2 = rc_ref[...], rs1_ref[...], rs2_ref[...]
    lane = lax.broadcasted_iota(jnp.int32, (x.shape[0], LANES), 1)

    def proj(a, b):
        return _dot(xn, w_ref[:, a:b])

    om_ref[:, 0:M_WIDTH] = (proj(0, M_WIDTH) * (M_DIM ** -0.5)).astype(BF16)
    om_ref[:, M_WIDTH:3 * M_WIDTH] = proj(M_WIDTH, 3 * M_WIDTH).astype(BF16)
    og_ref[...] = _sigmoid(proj(3 * M_WIDTH, 4 * M_WIDTH))

    off = 4 * M_WIDTH
    pq = proj(off, off + A_WIDTH)
    for j in range(A_WIDTH // LANES):
        blk = _rope128(pq[:, j * LANES:(j + 1) * LANES], c, s1, s2) * q_scale
        swapped = pltpu.roll(blk, A_DIM, 1)
        for e in range(2):
            hd = 2 * j + e
            grp = hd // A_REP
            src = blk if e == grp else swapped
            keep = (lane // A_DIM) == grp
            oq_ref[hd] = jnp.where(keep, src, 0.0).astype(BF16)

    off += A_WIDTH
    pk = proj(off, off + 6 * KV_WIDTH)
    kcr, vcr, ksl, vsl, kw, vw = [pk[:, i * LANES:(i + 1) * LANES] for i in range(6)]
    ksl = _rope128(ksl, c, s1, s2)
    kw = _rope128(kw, c, s1, s2)
    owin_ref[:, 0:LANES] = kw
    owin_ref[:, LANES:2 * LANES] = vw
    if attn_refs:
        okk_ref, ovst_ref, ovwt_ref, okvt_ref = attn_refs
        orows_ref[:, 0:LANES] = kcr
        orows_ref[:, LANES:2 * LANES] = vcr
        vsl_t = vsl.T
        for i, a_t in enumerate((kcr.T, vcr.T, ksl.T, vsl_t)):
            okvt_ref[i * LANES:(i + 1) * LANES, :] = a_t
        okk_ref[:, 0:LANES] = ksl.astype(BF16)
        okk_ref[:, LANES:2 * LANES] = kw.astype(BF16)
        ovst_ref[0] = vsl_t.astype(BF16)
        for i in range(ovwt_ref.shape[0]):
            ovwt_ref[i] = vw[i * LANES:(i + 1) * LANES].T.astype(BF16)
    else:
        for i, a in enumerate((kcr, vcr, ksl, vsl)):
            orows_ref[:, i * LANES:(i + 1) * LANES] = a

    off += 6 * KV_WIDTH
    pm = proj(off, off + LANES) + bias_ref[...]
    log_sig = -(jnp.maximum(-pm, 0.0) + jnp.log(1.0 + jnp.exp(-jnp.abs(pm))))
    omisc_ref[...] = jnp.where(lane < M_HEADS, pm, jnp.where(lane < 2 * M_HEADS, log_sig, _sigmoid(pm)))


def _inproj(x, gain, w_perm, bias, tabs, tm, attn=None):
    M, D = x.shape
    nt = tabs[0].shape[0] // tm
    row = lambda i: (i, 0)
    tab = pl.BlockSpec((tm, LANES), lambda i: (i % nt, 0))
    full = lambda a: pl.BlockSpec(a.shape, lambda i: (0,) * a.ndim)
    out_shape = (
        jax.ShapeDtypeStruct((M, 3 * M_WIDTH), BF16),
        jax.ShapeDtypeStruct((M, M_WIDTH), F32),
        jax.ShapeDtypeStruct((A_HEADS, M, LANES), BF16),
        jax.ShapeDtypeStruct((M, (2 if attn else 4) * KV_WIDTH), F32),
        jax.ShapeDtypeStruct((M, 2 * KV_WIDTH), F32),
        jax.ShapeDtypeStruct((M, LANES), F32),
    )
    out_specs = (
        pl.BlockSpec((tm, 3 * M_WIDTH), row),
        pl.BlockSpec((tm, M_WIDTH), row),
        pl.BlockSpec((A_HEADS, tm, LANES), lambda i: (0, i, 0)),
        pl.BlockSpec((tm, (2 if attn else 4) * KV_WIDTH), row),
        pl.BlockSpec((tm, 2 * KV_WIDTH), row),
        pl.BlockSpec((tm, LANES), row),
    )
    in_specs = [pl.BlockSpec((tm, D), row), full(gain), full(w_perm), full(bias), tab, tab, tab]
    operands = [x, gain, w_perm, bias, *tabs]
    aliases = {}
    if attn:
        tk, depth, layer, kvt = attn
        per = tk // tm
        seq = tabs[0].shape[0]
        out_shape += (jax.ShapeDtypeStruct((M, 2 * KV_WIDTH), BF16),
                      jax.ShapeDtypeStruct((M // tk, KV_WIDTH, tk), BF16),
                      jax.ShapeDtypeStruct((M // LANES, KV_WIDTH, LANES), BF16),
                      jax.ShapeDtypeStruct((M // seq, depth, 4 * KV_WIDTH, seq), F32))
        out_specs += (pl.BlockSpec((tm, 2 * KV_WIDTH), row),
                      pl.BlockSpec((1, KV_WIDTH, tm), lambda i: (i // per, 0, i % per)),
                      pl.BlockSpec((tm // LANES, KV_WIDTH, LANES), lambda i: (i, 0, 0)),
                      pl.BlockSpec((None, None, 4 * KV_WIDTH, tm), lambda i: (i // nt, layer, 0, i % nt)))
        in_specs.append(pl.BlockSpec(memory_space=pl.ANY))
        operands.append(kvt)
        aliases = {len(operands) - 1: len(out_shape) - 1}
    return pl.pallas_call(
        functools.partial(_inproj_kernel, A_DIM ** -0.5 * (LOG2_E if attn else 1.0), len(operands)),
        grid=(M // tm,),
        in_specs=in_specs,
        out_specs=out_specs,
        out_shape=out_shape,
        input_output_aliases=aliases,
        compiler_params=_params("parallel"),
        name="inproj",
    )(*operands)


def _mlstm_kernel(q_ref, k_ref, v_ref, og_ref, misc_ref, gt_ref, mg_ref, c0_ref, n0_ref, m0_ref,
                  hm_ref, c_ref, n_ref, m_ref):
    L = q_ref.shape[1]

    @pl.when(pl.program_id(1) == 0)
    def _():
        c_ref[...] = c0_ref[...]
        n_ref[...] = n0_ref[...]
        m_ref[...] = m0_ref[...]

    gc = misc_ref[0]
    gr = gt_ref[0]
    row = lax.broadcasted_iota(jnp.int32, (L, L), 0)
    col = lax.broadcasted_iota(jnp.int32, (L, L), 1)
    causal = row >= col
    b_col = _dot_exact(causal.astype(F32), gc)
    b_row = _dot_exact(gr, (row <= col).astype(F32))
    for h in range(M_HEADS):
        sl = slice(h * M_DIM, (h + 1) * M_DIM)
        q, k, v = q_ref[0, :, sl], k_ref[0, :, sl], v_ref[0, :, sl]
        c_prev, n_prev, m_prev = c_ref[0, h], n_ref[0, h:h + 1, :], m_ref[0, h:h + 1, 0:1]
        bc = b_col[:, M_HEADS + h:M_HEADS + h + 1]
        ic = gc[:, h:h + 1]
        br = b_row[M_HEADS + h:M_HEADS + h + 1, :]
        ir = gr[h:h + 1, :]
        d = jnp.where(causal, bc - br + ir, NEG)
        m_inter = bc + m_prev
        m_t = jnp.maximum(m_inter, jnp.max(d, axis=-1, keepdims=True))
        w_inter = jnp.exp(m_inter - m_t)
        p = _dot_nt(q, k) * jnp.exp(d - m_t)
        num = w_inter * _dot(q, c_prev.astype(BF16)) + _dot(p.astype(BF16), v)
        den = (w_inter * jnp.sum(q.astype(F32) * n_prev, axis=-1, keepdims=True)
               + jnp.sum(p, axis=-1, keepdims=True))
        hh = num / jnp.maximum(jnp.abs(den), jnp.exp(-m_t))
        m_new = m_t[L - 1:L, :]
        w_new = jnp.exp(bc[L - 1:L, :] - bc + ic - m_new)
        decay = jnp.exp(m_inter[L - 1:L, :] - m_new)
        kf, vf = k.astype(F32), v.astype(F32)
        c_ref[0, h] = decay * c_prev + lax.dot_general(k, (w_new * vf).astype(BF16), _TN,
                                                       preferred_element_type=F32)
        n_ref[0, h:h + 1, :] = decay * n_prev + jnp.sum(w_new * kf, axis=0, keepdims=True)
        m_ref[0, h:h + 1, :] = jnp.broadcast_to(m_new, (1, M_DIM))
        hn = hh * lax.rsqrt(jnp.mean(hh * hh, axis=-1, keepdims=True) + EPS)
        hm_ref[0, :, sl] = ((hn * mg_ref[:, sl]) * og_ref[0, :, sl]).astype(BF16)


def _mlstm_t_kernel(q_ref, k_ref, v_ref, og_ref, misc_ref, gt_ref, mg_ref, c0_ref, n0_ref, m0_ref,
                    hm_ref, c_ref, n_ref, m_ref):
    L = q_ref.shape[1]

    @pl.when(pl.program_id(1) == 0)
    def _():
        c_ref[...] = c0_ref[...]
        n_ref[...] = n0_ref[...]
        m_ref[...] = m0_ref[...]

    gc = misc_ref[0]
    gr = gt_ref[0]
    src = lax.broadcasted_iota(jnp.int32, (L, L), 0)
    tgt = lax.broadcasted_iota(jnp.int32, (L, L), 1)
    causal = src <= tgt
    b_col = _dot_exact((src >= tgt).astype(F32), gc)
    b_row = _dot_exact(gr, causal.astype(F32))
    contract_00 = (((0,), (1,)), ((), ()))
    for h in range(M_HEADS):
        sl = slice(h * M_DIM, (h + 1) * M_DIM)
        q, k, v = q_ref[0, :, sl], k_ref[0, :, sl], v_ref[0, :, sl]
        c_prev, n_prev, m_prev = c_ref[0, h], n_ref[0, h:h + 1, :], m_ref[0, h:h + 1, 0:1]
        br = b_row[M_HEADS + h:M_HEADS + h + 1, :]
        ir = gr[h:h + 1, :]
        src_term = b_col[:, M_HEADS + h:M_HEADS + h + 1] - gc[:, h:h + 1]
        d = jnp.where(causal, br - src_term, NEG)
        m_inter = br + m_prev
        m_t = jnp.maximum(m_inter, jnp.max(d, axis=0, keepdims=True))
        w_inter = jnp.exp(m_inter - m_t)
        p = _dot_nt(k, q) * jnp.exp(d - m_t)
        k_t = k.astype(F32).T
        v_t = v.astype(F32).T.astype(BF16)
        num = (w_inter * lax.dot_general(c_prev.astype(BF16), q, contract_00, preferred_element_type=F32)
               + _dot(v_t, p.astype(BF16)))
        q_n = _dot_nt(jnp.broadcast_to(n_prev, (8, M_DIM)).astype(BF16), q)[0:1, :]
        den = w_inter * q_n + jnp.sum(p, axis=0, keepdims=True)
        hh = num / jnp.maximum(jnp.abs(den), jnp.exp(-m_t))
        m_new = m_t[:, L - 1:L]
        w_new = jnp.exp(br[:, L - 1:L] - br + ir - m_new)
        decay = jnp.exp(m_inter[:, L - 1:L] - m_new)
        c_ref[0, h] = decay * c_prev + _dot((k_t * w_new).astype(BF16), v)
        n_ref[0, h:h + 1, :] = decay * n_prev + _dot(jnp.broadcast_to(w_new, (8, L)).astype(BF16), k)[0:1, :]
        m_ref[0, h:h + 1, :] = jnp.broadcast_to(m_new, (1, M_DIM))
        hn = hh * lax.rsqrt(jnp.mean(hh * hh, axis=0, keepdims=True) + EPS)
        hm_ref[0, :, sl] = ((hn.T * mg_ref[:, sl]) * og_ref[0, :, sl]).astype(BF16)


def _mlstm(qkv, og, misc, gt, mg, c0, n0, m0, L, transposed):
    B, T, _ = qkv.shape
    seq = lambda j: pl.BlockSpec((1, L, M_WIDTH), lambda b, c: (b, c, j))
    st4 = pl.BlockSpec((1, M_HEADS, M_DIM, M_DIM), lambda b, c: (b, 0, 0, 0))
    st3 = pl.BlockSpec((1, M_HEADS, M_DIM), lambda b, c: (b, 0, 0))
    return pl.pallas_call(
        _mlstm_t_kernel if transposed else _mlstm_kernel,
        grid=(B, T // L),
        in_specs=[seq(0), seq(1), seq(2), seq(0),
                  pl.BlockSpec((1, L, LANES), lambda b, c: (b, c, 0)),
                  pl.BlockSpec((1, 2 * M_HEADS, L), lambda b, c: (b, 0, c)),
                  pl.BlockSpec((1, M_WIDTH), lambda b, c: (0, 0)),
                  st4, st3, st3],
        out_specs=(seq(0), st4, st3, st3),
        out_shape=(jax.ShapeDtypeStruct((B, T, M_WIDTH), BF16),
                   jax.ShapeDtypeStruct(c0.shape, F32),
                   jax.ShapeDtypeStruct(n0.shape, F32),
                   jax.ShapeDtypeStruct(m0.shape, F32)),
        compiler_params=_params("parallel", "arbitrary"),
        name="mlstm",
    )(qkv, qkv, qkv, og, misc, gt, mg, c0, n0, m0)


def _cmp_accumulate(load, w_ref, s):
    acc = None
    for r in range(0, D_CMP, 2):
        lhs = jnp.concatenate([load(r), load(r + 1)], axis=1).astype(BF16)
        term = _dot(lhs, w_ref[s, r // 2])
        acc = term if acc is None else acc + term
    return acc


def _cmp_prompt_kernel(k_ref, v_ref, w_ref, fs_ref):
    n = k_ref.shape[0] // D_CMP
    for s, ref in enumerate((k_ref, v_ref)):
        fs_ref[s] = _cmp_accumulate(lambda r: ref[pl.ds(r, n, stride=D_CMP), :], w_ref, s)


def _cmp_prompt(rows, w_cmp, tr):
    M = rows.shape[0]
    return pl.pallas_call(
        _cmp_prompt_kernel,
        grid=(M // tr,),
        in_specs=[pl.BlockSpec((tr, KV_WIDTH), lambda i: (i, 0)), pl.BlockSpec((tr, KV_WIDTH), lambda i: (i, 1)),
                  pl.BlockSpec(w_cmp.shape, lambda i: (0, 0, 0, 0))],
        out_specs=pl.BlockSpec((2, tr // D_CMP, 2 * LANES), lambda i: (0, i, 0)),
        out_shape=jax.ShapeDtypeStruct((2, M // D_CMP, 2 * LANES), F32),
        compiler_params=_params("parallel"),
        name="cmp_prompt",
    )(rows, rows, w_cmp)


def _cmp_sample_kernel(n_group, pt_ref, *refs):
    page_refs, w_ref, fs_ref, rows_s = refs[:n_group], refs[n_group], refs[n_group + 1], refs[n_group + 2:]
    page = page_refs[0].shape[2]
    n = n_group * page // D_CMP
    for s in range(2):
        for i, p in enumerate(page_refs):
            rows_s[s][i * page:(i + 1) * page, :] = p[s].T
    for s in range(2):
        fs_ref[s, 0] = _cmp_accumulate(lambda r: rows_s[s][pl.ds(r, n, stride=D_CMP), :], w_ref, s)


def _cmp_sample(cache_t, page_table, layer, w_cmp, n_group):
    B, n_pages = page_table.shape
    page = cache_t.shape[4]
    per_page = page // D_CMP

    def page_spec(i):
        return pl.BlockSpec((None, None, 2, KV_WIDTH, page),
                            lambda b, p, pt: (pt[b, p * n_group + i], layer, 0, 0, 0))

    grid_spec = pltpu.PrefetchScalarGridSpec(
        num_scalar_prefetch=1,
        grid=(B, n_pages // n_group),
        in_specs=[page_spec(i) for i in range(n_group)]
        + [pl.BlockSpec(w_cmp.shape, lambda b, p, pt: (0, 0, 0, 0))],
        out_specs=pl.BlockSpec((2, 1, n_group * per_page, 2 * LANES), lambda b, p, pt: (0, b, p, 0)),
        scratch_shapes=[pltpu.VMEM((n_group * page, KV_WIDTH), F32), pltpu.VMEM((n_group * page, KV_WIDTH), F32)],
    )
    return pl.pallas_call(
        functools.partial(_cmp_sample_kernel, n_group),
        grid_spec=grid_spec,
        out_shape=jax.ShapeDtypeStruct((2, B, n_pages * per_page, 2 * LANES), F32),
        compiler_params=_params("parallel", "arbitrary"),
        name="cmp_sample",
    )(page_table, *([cache_t] * n_group), w_cmp)


def _compressed_kv(fs_ref, bias_ref, tabs, b=0):
    n = fs_ref.shape[2]
    out = []
    for s in range(2):
        fs = fs_ref[s, b]
        out.append(fs[:, 0:LANES] + pltpu.roll(fs[:, LANES:2 * LANES], n - 1, 0) + bias_ref[s:s + 1, :])
    return _rope128(out[0], *tabs), out[1]


def _masked_softmax(s, valid, axis):
    m = jnp.max(jnp.where(valid, s, NEG), axis=axis, keepdims=True)
    e = jnp.where(valid, jnp.exp(s - m), 0.0)
    l = jnp.sum(e, axis=axis, keepdims=True)
    return e * (1.0 / jnp.where(l > 0.0, l, 1.0))


def _bias_softmax(s, axis, keep=None):
    m = jnp.max(s, axis=axis, keepdims=True)
    e = jnp.exp2(s - m)
    scale = 1.0 / jnp.sum(e, axis=axis, keepdims=True)
    return e * (scale if keep is None else scale * keep)


def _split3(x):
    h1 = x.astype(BF16)
    r1 = x - h1.astype(F32)
    h2 = r1.astype(BF16)
    return h1, h2, (r1 - h2.astype(F32)).astype(BF16)


def _dot_split3(a, x):
    h1, h2, h3 = _split3(x)
    return _dot(a, h1) + _dot(a, h2) + _dot(a, h3)


def _dot_split3_rhs(x, a):
    h1, h2, h3 = _split3(x)
    return _dot(h1, a) + _dot(h2, a) + _dot(h3, a)


def _top_blocks(score, n_top, axis):
    pos = lax.broadcasted_iota(jnp.int32, score.shape, axis).astype(F32)
    picks = []
    for _ in range(n_top):
        mx = jnp.max(score, axis=axis, keepdims=True)
        idx = jnp.min(jnp.where(score == mx, pos, float(score.shape[axis])), axis=axis, keepdims=True)
        picks.append(idx)
        score = jnp.where(pos == idx, NEG, score)
    return picks, score


def _nsa_prompt_kernel(tk, q_ref, gt_ref, fs_ref, cb_ref, kc_c, kc_s1, kc_s2, ovt_ref,
                       ks_ref, kw_ref, vst_ref, vwt_ref, oa_ref,
                       kc_s, vct_s, sel_s, m_s, acc_s, sc_s, part_s, oc_s, imp_s):
    n_cmp = fs_ref.shape[2]
    n_sel = ovt_ref.shape[0]
    nq = Q_BLOCK
    cols = A_REP * nq
    per_tile = tk // L_SEL
    qi = pl.program_id(1)
    q0 = qi * nq

    @pl.when(qi == 0)
    def _():
        kc, vc = _compressed_kv(fs_ref, cb_ref, (kc_c[...], kc_s1[...], kc_s2[...]))
        kc_s[...] = kc.astype(BF16)
        vct_s[...] = vc.T.astype(BF16)

    t_pos = q0 + lax.broadcasted_iota(jnp.int32, (1, nq), 1)
    gates = gt_ref[0]

    def heads(a):
        return jnp.concatenate([a] * A_REP, axis=1)

    def all_heads(a):
        return jnp.concatenate([a] * A_HEADS, axis=1)

    cmp_end = lax.broadcasted_iota(jnp.int32, (n_cmp, 1), 0) * D_CMP + (L_CMP - 1)
    c_bias = jnp.where(cmp_end <= t_pos, 0.0, NEG)
    c_keep = jnp.where(t_pos >= L_CMP - 1, 1.0, 0.0)
    blk = lax.broadcasted_iota(jnp.int32, (n_sel, 1), 0)
    t_both = jnp.concatenate([t_pos] * A_KV, axis=1)
    cur = t_both // L_SEL
    forced = (blk == 0) | (blk == cur) | (blk == cur - 1)
    started = blk * L_SEL <= t_both
    w_start = jnp.maximum(q0 - WINDOW, 0)
    w_len = WINDOW + nq
    rel = t_pos - (w_start + lax.broadcasted_iota(jnp.int32, (w_len, 1), 0))
    w_bias = jnp.where((rel >= 0) & (rel < WINDOW), 0.0, NEG)
    n_tiles = (q0 + nq + tk - 1) // tk
    key_iota = lax.broadcasted_iota(jnp.int32, (tk, 1), 0)

    groups = range(A_KV)
    vs = [slice(g * A_DIM, (g + 1) * A_DIM) for g in groups]
    gs = [slice(g * cols, (g + 1) * cols) for g in groups]
    q_all = q_ref[...].reshape(A_HEADS * nq, LANES)

    n_cls = n_cmp // LANES if n_cmp % LANES == 0 else 1
    unit = n_cmp // n_cls
    cls = jnp.minimum(((q0 + nq) // D_CMP + unit - 1) // unit, n_cls)

    def compressed(rows):
        p_c = _bias_softmax(_dot_nt(kc_s[0:rows, :], q_all) + all_heads(c_bias[0:rows]), 0, all_heads(c_keep))
        p_c16 = p_c.astype(BF16)
        p_sums = []
        for g in groups:
            oc_s[g] = _dot(vct_s[vs[g], 0:rows], p_c16[:, gs[g]])
            p_sum = p_c[:, g * cols:g * cols + nq]
            for r in range(1, A_REP):
                p_sum = p_sum + p_c[:, g * cols + r * nq:g * cols + (r + 1) * nq]
            p_sums.append(p_sum)
        imp_s[...] = _dot_split3(ovt_ref[:, 0:rows], jnp.concatenate(p_sums, axis=1))

    for k in range(1, n_cls + 1):
        pl.when(cls == k)(functools.partial(compressed, k * unit))
    o_c = [oc_s[g] for g in groups]

    score = jnp.where(forced, FORCE_SCORE, jnp.where(started, imp_s[...], -1.0))
    _, left_all = _top_blocks(score, min(N_TOP, n_sel), 0)
    for g in groups:
        left = left_all[:, g * nq:(g + 1) * nq]
        sel_s[g] = jnp.where(left < -2.0, 0.0, NEG)

    ws = pl.multiple_of(w_start, nq)
    s_w = _dot_nt(kw_ref[0, pl.ds(ws, w_len), :], q_all) + all_heads(w_bias)
    e_w = jnp.exp2(s_w - jnp.max(s_w, axis=0, keepdims=True)).astype(BF16)
    wc = w_start // LANES
    vw_t = jnp.concatenate([vwt_ref[0, wc + c] for c in range(w_len // LANES)], axis=1)
    ones_w = jnp.ones((16, w_len), BF16)
    for g in groups:
        acc_w = _dot(jnp.concatenate([vw_t[vs[g], :], ones_w], axis=0), e_w[:, gs[g]])
        o_w = acc_w[0:A_DIM] * (1.0 / acc_w[A_DIM:A_DIM + 1])
        for r in range(A_REP):
            hd = g * A_REP + r
            base = 2 * M_HEADS + hd * N_BRANCH
            cs = slice(r * nq, (r + 1) * nq)
            part_s[hd * A_DIM:(hd + 1) * A_DIM, :] = (gates[base:base + 1] * o_c[g][:, cs]
                                                      + gates[base + 2:base + 3] * o_w[:, cs])

    m_s[...] = jnp.full(m_s.shape, NEG, F32)
    acc_s[...] = jnp.zeros(acc_s.shape, F32)
    ones_rows = jnp.ones((acc_s.shape[1] - A_DIM, tk), BF16)

    last_tile = ks_ref.shape[1] // tk - 1

    def scores(kt):
        k0 = pl.multiple_of(jnp.minimum(kt, last_tile) * tk, tk)
        return _dot_nt(ks_ref[0, pl.ds(k0, tk), :], q_all)

    def attend(kt, slot):
        kc = jnp.minimum(kt, last_tile)
        causal = kt * tk + key_iota <= t_pos
        for g in groups:
            chosen = jnp.concatenate(
                [jnp.broadcast_to(sel_s[g, pl.ds(kc * per_tile + i, 1), :], (L_SEL, nq)) for i in range(per_tile)],
                axis=0)
            s = sc_s[slot, :, gs[g]] + heads(jnp.where(causal, chosen, NEG))
            m_old = m_s[:, gs[g]]
            m_new = jnp.maximum(m_old, jnp.max(s, axis=0, keepdims=True))
            e16 = jnp.exp2(s - m_new).astype(BF16)
            vals = jnp.concatenate([vst_ref[0, kc, vs[g], :], ones_rows], axis=0)
            acc_s[g] = jnp.exp2(m_old - m_new) * acc_s[g] + _dot(vals, e16)
            m_s[:, gs[g]] = m_new

    sc_s[0] = scores(0)

    def pair(j, carry):
        sc_s[1] = scores(2 * j + 1)
        attend(2 * j, 0)
        sc_s[0] = scores(2 * j + 2)
        attend(2 * j + 1, 1)
        return carry

    lax.fori_loop(0, n_tiles // 2, pair, 0)

    @pl.when(n_tiles % 2 == 1)
    def _():
        attend(n_tiles - 1, 0)

    heads_out = []
    for g in groups:
        l = acc_s[g, A_DIM:A_DIM + 1, :]
        o_s = acc_s[g, 0:A_DIM, :] * (1.0 / jnp.where(l > 0.0, l, 1.0))
        for r in range(A_REP):
            hd = g * A_REP + r
            gate = gates[2 * M_HEADS + hd * N_BRANCH + 1:2 * M_HEADS + hd * N_BRANCH + 2]
            heads_out.append(part_s[hd * A_DIM:(hd + 1) * A_DIM, :] + gate * o_s[:, r * nq:(r + 1) * nq])
    oa_ref[0] = jnp.concatenate(heads_out, axis=0).T.astype(BF16)


def _nsa_prompt(q8, gt, fs, cbias, kc_tabs, ovt, kk, vst, vwt, B, T, tk):
    nqb = T // Q_BLOCK
    n_cmp = fs.shape[2]
    n_sel = ovt.shape[0]
    cols = A_REP * Q_BLOCK
    const = lambda a: pl.BlockSpec(a.shape, lambda b, i: (0,) * a.ndim)
    keys = lambda j: pl.BlockSpec((1, T, LANES), lambda b, i: (b, 0, j))
    per_b = lambda a: pl.BlockSpec((1,) + a.shape[1:], lambda b, i: (b, 0, 0, 0))
    return pl.pallas_call(
        functools.partial(_nsa_prompt_kernel, tk),
        grid=(B, nqb),
        in_specs=[pl.BlockSpec((A_HEADS, Q_BLOCK, LANES), lambda b, i: (0, b * nqb + i, 0)),
                  pl.BlockSpec((1, gt.shape[1], Q_BLOCK), lambda b, i: (b, 0, i)),
                  pl.BlockSpec((2, 1, n_cmp, 2 * LANES), lambda b, i: (0, b, 0, 0)),
                  const(cbias), const(kc_tabs[0]), const(kc_tabs[1]), const(kc_tabs[2]),
                  const(ovt), keys(0), keys(1), per_b(vst), per_b(vwt)],
        out_specs=pl.BlockSpec((1, Q_BLOCK, A_WIDTH), lambda b, i: (b, i, 0)),
        out_shape=jax.ShapeDtypeStruct((B, T, A_WIDTH), BF16),
        scratch_shapes=[pltpu.VMEM((n_cmp, LANES), BF16), pltpu.VMEM((LANES, n_cmp), BF16),
                        pltpu.VMEM((A_KV, n_sel, Q_BLOCK), F32),
                        pltpu.VMEM((1, A_KV * cols), F32),
                        pltpu.VMEM((A_KV, A_DIM + 16, cols), F32),
                        pltpu.VMEM((2, tk, A_KV * cols), F32),
                        pltpu.VMEM((A_WIDTH, Q_BLOCK), F32),
                        pltpu.VMEM((A_KV, A_DIM, cols), F32),
                        pltpu.VMEM((n_sel, A_KV * Q_BLOCK), F32)],
        compiler_params=_params("parallel", "arbitrary"),
        name="nsa_prompt",
    )(q8, gt, fs, cbias, *kc_tabs, ovt, kk, kk, vst, vwt)


def _nsa_sample_select_kernel(q_pos, n_sel, q_ref, fs_ref, cb_ref, kc_c, kc_s1, kc_s2, ov_ref, oc_ref, idx_ref):
    n_seq, n_cmp = fs_ref.shape[1], fs_ref.shape[2]
    tabs = (kc_c[...], kc_s1[...], kc_s2[...])
    cmp_end = lax.broadcasted_iota(jnp.int32, (A_HEADS, n_cmp), 1) * D_CMP + (L_CMP - 1)
    head = lax.broadcasted_iota(jnp.int32, (A_HEADS, n_cmp), 0)
    imps = []
    for b in range(n_seq):
        kc, vc = _compressed_kv(fs_ref, cb_ref, tabs, b)
        p_c = _masked_softmax(_dot_nt(q_ref[b], kc.astype(BF16)), cmp_end <= q_pos, 1)
        oc_ref[b] = _dot(p_c.astype(BF16), vc.astype(BF16))
        p_sum = jnp.zeros((A_HEADS, n_cmp), F32)
        for g in range(A_KV):
            pg = jnp.sum(jnp.where(head // A_REP == g, p_c, 0.0), axis=0, keepdims=True)
            p_sum = jnp.where(head == g, pg, p_sum)
        imps.append(_dot_split3_rhs(p_sum, ov_ref[...]))
    imp = jnp.concatenate(imps, axis=0)
    n_pad = ov_ref.shape[1]
    blk = lax.broadcasted_iota(jnp.int32, (A_HEADS * n_seq, n_pad), 1)
    cur = q_pos // L_SEL
    forced = (blk == 0) | (blk == cur) | (blk == cur - 1)
    score = jnp.where(forced, FORCE_SCORE, jnp.where(blk * L_SEL <= q_pos, imp, -1.0))
    score = jnp.where(blk < n_sel, score, 2 * NEG)
    picks, _ = _top_blocks(score, min(N_TOP, n_sel), 1)
    lane = lax.broadcasted_iota(jnp.int32, (A_HEADS * n_seq, LANES), 1)
    out = jnp.zeros((A_HEADS * n_seq, LANES), jnp.int32)
    for j, idx in enumerate(picks):
        out = jnp.where(lane == j, idx.astype(jnp.int32), out)
    for b in range(n_seq):
        idx_ref[b] = out[b * A_HEADS:(b + 1) * A_HEADS]


def _nsa_sample_select(q8, fs, cbias, kc_tabs, ov, q_pos, n_sel, n_seq):
    B = q8.shape[0]
    n_cmp = fs.shape[2]
    const = lambda a: pl.BlockSpec(a.shape, lambda b: (0,) * a.ndim)
    per_b = pl.BlockSpec((n_seq, A_HEADS, LANES), lambda b: (b, 0, 0))
    return pl.pallas_call(
        functools.partial(_nsa_sample_select_kernel, q_pos, n_sel),
        grid=(B // n_seq,),
        in_specs=[per_b, pl.BlockSpec((2, n_seq, n_cmp, 2 * LANES), lambda b: (0, b, 0, 0)),
                  const(cbias), const(kc_tabs[0]), const(kc_tabs[1]), const(kc_tabs[2]), const(ov)],
        out_specs=(per_b, per_b),
        out_shape=(jax.ShapeDtypeStruct((B, A_HEADS, LANES), F32),
                   jax.ShapeDtypeStruct((B, A_HEADS, LANES), jnp.int32)),
        compiler_params=_params("parallel"),
        name="nsa_sample_select",
    )(q8, fs, cbias, *kc_tabs, ov)


def _r16(a):
    return a.astype(BF16).astype(F32)


def _nsa_sample_attend_kernel(q_pos, n_top, pt_ref, ix_ref, q_ref, new_ref, newc_ref, oc_ref, misc_ref, win_ref,
                              *refs):
    page_refs, (oa_ref, nwin_ref) = refs[:A_KV * n_top], refs[A_KV * n_top:]
    b = pl.program_id(0)
    q = q_ref[0]
    qf = q.astype(F32)
    head = lax.broadcasted_iota(jnp.int32, (A_HEADS, 1), 0)
    new = new_ref[0]
    page = page_refs[0].shape[2]
    per_page = page // L_SEL
    lane = lax.broadcasted_iota(jnp.int32, (A_HEADS, page), 1)
    s_tok = jnp.sum(qf * _r16(new[0:1, :]), axis=-1, keepdims=True)

    o_s = jnp.zeros((A_HEADS, LANES), F32)
    for g in range(A_KV):
        parts, n_fresh = [], 0
        for j in range(n_top):
            blk = ix_ref[b, g, j]
            fresh = blk * L_SEL >= q_pos
            limit = jnp.where(fresh, -1, q_pos)
            s = _dot(q, page_refs[g * n_top + j][0].astype(BF16))
            s_pos = (blk // per_page) * page + lane
            ok = jnp.where(lane // L_SEL == blk % per_page, s_pos, limit + 1) <= limit
            parts.append(jnp.where(ok, s, NEG))
            n_fresh = n_fresh + jnp.where(fresh, 1, 0)
        s_all = jnp.concatenate(parts, axis=1)
        has_tok = jnp.where(head >= 0, n_fresh, 0) > 0
        m = jnp.maximum(jnp.max(s_all, axis=-1, keepdims=True), jnp.where(has_tok, s_tok, NEG))
        e = jnp.exp(s_all - m)
        e_tok = jnp.where(has_tok, jnp.exp(s_tok - m), 0.0)
        den = jnp.sum(e, axis=-1, keepdims=True) + e_tok
        e16 = e.astype(BF16)
        acc = _r16(e_tok) * _r16(new[1:2, :])
        for j in range(n_top):
            acc = acc + _dot_nt(e16[:, j * page:(j + 1) * page], page_refs[g * n_top + j][1].astype(BF16))
        o_s = jnp.where(head // A_REP == g, acc / den, o_s)

    wb = win_ref.shape[2]
    kwt, vwt = win_ref[0], win_ref[1]
    s_old = _dot(q, kwt.astype(BF16))
    s_new = jnp.sum(qf * _r16(new[2:3, :]), axis=-1, keepdims=True)
    rel = wb - lax.broadcasted_iota(jnp.int32, (A_HEADS, wb), 1)
    ok = (rel >= 0) & (rel < WINDOW) & (q_pos - rel >= 0)
    m = jnp.maximum(jnp.max(jnp.where(ok, s_old, NEG), axis=-1, keepdims=True), s_new)
    e_old = jnp.where(ok, jnp.exp(s_old - m), 0.0)
    e_new = jnp.exp(s_new - m)
    den = jnp.sum(e_old, axis=-1, keepdims=True) + e_new
    o_w = _dot_nt((e_old / den).astype(BF16), vwt.astype(BF16)) + _r16(e_new / den) * _r16(new[3:4, :])
    gates = misc_ref[0]
    oa_ref[0] = gates[:, 0:1] * oc_ref[0] + gates[:, 1:2] * o_s + gates[:, 2:3] * o_w
    newc = newc_ref[0]
    last = lax.broadcasted_iota(jnp.int32, (LANES, wb), 1) == wb - 1
    nwin_ref[0] = jnp.where(last, newc[:, 2:3], pltpu.roll(kwt, wb - 1, 1))
    nwin_ref[1] = jnp.where(last, newc[:, 3:4], pltpu.roll(vwt, wb - 1, 1))


def _nsa_sample_attend(page_table, idx, q8, new_rows, new_cols, o_c, gate_rows, win_t, cache_t, layer, q_pos):
    B, n_pages = page_table.shape
    n_top = idx.shape[2]
    page = cache_t.shape[4]
    per_page = page // L_SEL
    wb = win_t.shape[4]

    def page_spec(g, j):
        def index(b, pt, ix):
            pg = jnp.clip(ix[b, g, j] // per_page, 0, n_pages - 1)
            return (pt[b, pg], layer, 1, 0, 0)
        return pl.BlockSpec((None, None, 2, KV_WIDTH, page), index)

    per_b = pl.BlockSpec((1, A_HEADS, LANES), lambda b, pt, ix: (b, 0, 0))
    pages = [page_spec(g, j) for g in range(A_KV) for j in range(n_top)]
    grid_spec = pltpu.PrefetchScalarGridSpec(
        num_scalar_prefetch=2,
        grid=(B,),
        in_specs=[per_b, per_b, pl.BlockSpec((1, KV_WIDTH, A_HEADS), lambda b, pt, ix: (b, 0, 0)), per_b, per_b,
                  pl.BlockSpec((None, None, 2, KV_WIDTH, wb), lambda b, pt, ix: (b, layer, 0, 0, 0))] + pages,
        out_specs=(per_b, pl.BlockSpec((None, 2, KV_WIDTH, wb), lambda b, pt, ix: (b, 0, 0, 0))),
    )
    return pl.pallas_call(
        functools.partial(_nsa_sample_attend_kernel, q_pos, n_top),
        grid_spec=grid_spec,
        out_shape=(jax.ShapeDtypeStruct((B, A_HEADS, LANES), F32),
                   jax.ShapeDtypeStruct((B, 2, KV_WIDTH, wb), F32)),
        compiler_params=_params("parallel"),
        name="nsa_sample_attend",
    )(page_table, idx, q8, new_rows, new_cols, o_c, gate_rows, win_t, *([cache_t] * len(pages)))


def _ffn_kernel(final, tf, x_ref, hm_ref, oa_ref, wo_ref, nf_ref, wg_ref, wu_ref, wd_ref, nfin_ref, out_ref, act_s):
    half = hm_ref.shape[1]
    x1 = x_ref[...] + _dot(hm_ref[...], wo_ref[0:half, :]) + _dot(oa_ref[...], wo_ref[half:, :])
    ms = jnp.mean(x1 * x1, axis=-1, keepdims=True)
    xn = ((x1 * lax.rsqrt(ms + EPS)) * nf_ref[...]).astype(BF16)
    for c in range(wg_ref.shape[1] // tf):
        sl = slice(c * tf, (c + 1) * tf)
        gate = _dot(xn, wg_ref[:, sl])
        act_s[:, sl] = ((gate * _sigmoid(gate)) * _dot(xn, wu_ref[:, sl])).astype(BF16)
    y = x1 + _dot(act_s[...], wd_ref[...])
    if final:
        ms = jnp.mean(y * y, axis=-1, keepdims=True)
        y = (y * lax.rsqrt(ms + EPS)) * nfin_ref[...]
    out_ref[...] = y


def _ffn(x, hm, oa, wo, nf, wg, wu, wd, nfin, final, tm, tf):
    M, D = x.shape
    F = wg.shape[1]
    row = lambda w: pl.BlockSpec((tm, w), lambda i: (i, 0))
    const = lambda a: pl.BlockSpec(a.shape, lambda i: (0,) * a.ndim, pipeline_mode=pl.Buffered(1))
    return pl.pallas_call(
        functools.partial(_ffn_kernel, final, tf),
        grid=(M // tm,),
        in_specs=[row(D), row(hm.shape[1]), row(oa.shape[1]), const(wo), const(nf),
                  const(wg), const(wu), const(wd), const(nfin)],
        out_specs=row(D),
        out_shape=jax.ShapeDtypeStruct((M, D), F32),
        scratch_shapes=[pltpu.VMEM((tm, F), BF16)],
        compiler_params=_params("parallel"),
        name="ffn",
    )(x, hm, oa, wo, nf, wg, wu, wd, nfin)


def _permute_w_in(w):
    a = 4 * M_WIDTH
    b = a + 2 * M_HEADS
    c = b + A_WIDTH + 6 * KV_WIDTH
    pad = jnp.zeros((w.shape[0], LANES - N_MISC), w.dtype)
    return jnp.concatenate([w[:, :a], w[:, b:c], w[:, a:b], w[:, c:], pad], axis=1).astype(BF16)


def _cmp_weights(wk, wv):
    def one(w):
        z = jnp.zeros((D_CMP, A_DIM, A_DIM), w.dtype)
        halves = []
        for part in (w[:D_CMP], w[D_CMP:]):
            top = jnp.concatenate([part, z], axis=2)
            bot = jnp.concatenate([z, part], axis=2)
            halves.append(jnp.concatenate([top, bot], axis=1))
        return jnp.concatenate(halves, axis=2)
    return jnp.stack([one(wk), one(wv)]).astype(BF16).reshape(2, D_CMP // 2, 2 * KV_WIDTH, 2 * LANES)


def _overlap(n_cmp_pad, n_sel_pad, n_cmp, n_sel):
    lo_c = np.arange(n_cmp_pad)[:, None] * D_CMP
    lo_s = np.arange(n_sel_pad)[None, :] * L_SEL
    ov = np.clip(np.minimum(lo_c + L_CMP, lo_s + L_SEL) - np.maximum(lo_c, lo_s), 0, None).astype(np.float32) / L_CMP
    ov[n_cmp:, :] = 0.0
    ov[:, n_sel:] = 0.0
    return jnp.asarray(ov)


def _pick(n, prefs):
    for p in prefs:
        if n % p == 0:
            return p
    return n


def kernel(x_prompt, x_sample, cache_nsa_kv, page_table, state_win_kv, state_mlstm_c, state_mlstm_n, state_mlstm_m, norm_mix, w_in, b_if, m_norm, w_cmp_k, b_cmp_k, w_cmp_v, b_cmp_v, w_out, norm_ffn, w_gate, w_up, w_down, norm_final):
    B, T, D = x_prompt.shape
    Bs, Ts, _ = x_sample.shape
    depth = w_in.shape[0]
    n_pool, _, n_slots, page = cache_nsa_kv.shape[:4]
    n_pages = page_table.shape[1]
    past = n_pages * page
    wb = state_win_kv.shape[3]
    assert Ts == 1 and T % Q_BLOCK == 0 and T >= WINDOW + Q_BLOCK and wb == WINDOW
    assert past % page == 0 and page % L_SEL == 0 and page % D_CMP == 0
    assert ((past + Ts) // D_CMP) * D_CMP <= past

    Mp = B * T
    tm_p = _pick(Mp, (512, 256, 128))
    tm_f = _pick(Mp, (512, 256, 128))
    ff = w_gate.shape[2]
    tf = _pick(ff, (256, 128))
    l_chunk = _pick(T, (256, 128, 64))
    ls_pad = 16
    tk = _pick(T, (512, 256, 128))
    assert tk % tm_p == 0 and tm_p % LANES == 0
    n_group = _pick(n_pages, (16, 8, 4, 2, 1))
    tr = _pick(T, (2048, 1024, 512, 256))

    n_chunk_p = T // D_CMP
    n_cmp_p, n_sel_p = n_chunk_p - 1, -(-T // L_SEL)
    tabs_p = _rope_tables(jnp.arange(T))
    kc_tabs_p = _rope_tables(jnp.arange(n_chunk_p) * D_CMP)
    assert L_CMP <= 256
    ovt_p = _overlap(n_chunk_p, n_sel_p, n_cmp_p, n_sel_p).T.astype(BF16)
    n_chunk_s = (past + Ts) // D_CMP
    n_cmp_s, n_sel_s = n_chunk_s - 1, -(-(past + Ts) // L_SEL)
    n_sel_pad = -(-n_sel_s // LANES) * LANES
    tabs_s = _rope_tables(jnp.full((Bs,), past))
    kc_tabs_s = _rope_tables(jnp.arange(n_chunk_s) * D_CMP)
    ov_s = _overlap(n_chunk_s, n_sel_pad, n_cmp_s, n_sel_s).astype(BF16)
    cache_t = cache_nsa_kv.transpose(0, 1, 2, 4, 5, 3).reshape(n_pool, depth, n_slots, KV_WIDTH, page)
    win_t = state_win_kv.transpose(0, 1, 2, 4, 5, 3).reshape(Bs, depth, 2, KV_WIDTH, wb)

    xp = x_prompt.reshape(Mp, D)
    xs = x_sample.reshape(Bs, D)
    rows_s, win_p, win_s, c_p, c_s, n_p, n_s, m_p, m_s = ([] for _ in range(9))
    y_p = y_s = None
    kvt = jnp.zeros((B, depth, n_slots * KV_WIDTH, T), F32)
    nfin = norm_final.reshape(1, D)
    for l in range(depth):
        w_perm = _permute_w_in(w_in[l])
        gain = norm_mix[l].reshape(1, D)
        bias = jnp.zeros((1, LANES), F32).at[0, :2 * M_HEADS].set(b_if[l].reshape(-1))
        mg = m_norm[l].reshape(1, M_WIDTH)
        w_cmp = _cmp_weights(w_cmp_k[l], w_cmp_v[l])
        cbias = jnp.stack([jnp.tile(b_cmp_k[l], A_KV), jnp.tile(b_cmp_v[l], A_KV)])
        wo, nf = w_out[l].astype(BF16), norm_ffn[l].reshape(1, D)
        wg, wu, wd = w_gate[l].astype(BF16), w_up[l].astype(BF16), w_down[l].astype(BF16)
        final = l == depth - 1

        qkv, og, q8, rows, win, misc, kk, vst, vwt, kvt = _inproj(xp, gain, w_perm, bias, tabs_p, tm_p,
                                                                    (tk, depth, l, kvt))
        misc3 = misc.reshape(B, T, LANES)
        gt = misc3[:, :, :N_MISC].transpose(0, 2, 1)
        hm, c_new, n_new, m_new = _mlstm(
            qkv.reshape(B, T, -1), og.reshape(B, T, -1), misc3, gt, mg,
            jnp.zeros((B, M_HEADS, M_DIM, M_DIM), F32), jnp.zeros((B, M_HEADS, M_DIM), F32),
            jnp.full((B, M_HEADS, M_DIM), NEG, F32), l_chunk, True)
        fs = _cmp_prompt(rows, w_cmp, tr).reshape(2, B, n_chunk_p, 2 * LANES)
        oa = _nsa_prompt(q8, gt, fs, cbias, kc_tabs_p, ovt_p, kk.reshape(B, T, -1),
                         vst.reshape(B, T // tk, KV_WIDTH, tk), vwt.reshape(B, T // LANES, KV_WIDTH, LANES), B, T, tk)
        xp_new = _ffn(xp, hm.reshape(Mp, -1), oa.reshape(Mp, -1), wo, nf, wg, wu, wd, nfin, final, tm_f, tf)
        if final:
            y_p = xp_new
        xp = xp_new
        win_last = win.reshape(B, T, 2 * KV_WIDTH)[:, T - wb:]
        win_p.append(win_last.reshape(B, wb, 2, A_KV, A_DIM).transpose(0, 2, 1, 3, 4))
        c_p.append(c_new); n_p.append(n_new); m_p.append(m_new[:, :, 0])

        qkv, og, q8, rows, win, misc = _inproj(xs, gain, w_perm, bias, tabs_s, Bs)
        pad_t = lambda a: jnp.pad(a[:, None, :], ((0, 0), (0, ls_pad - 1), (0, 0)))
        inert = jnp.zeros((ls_pad, LANES), F32).at[1:, :M_HEADS].set(NEG)
        misc_pad = pad_t(misc) + inert[None]
        gt = misc_pad[:, :, :2 * M_HEADS].transpose(0, 2, 1)
        m0 = jnp.broadcast_to(state_mlstm_m[:, l, :, None].astype(F32), (Bs, M_HEADS, M_DIM))
        hm, c_new, n_new, m_new = _mlstm(
            pad_t(qkv), pad_t(og), misc_pad, gt, mg,
            state_mlstm_c[:, l].astype(F32), state_mlstm_n[:, l].astype(F32), m0, ls_pad, False)
        hm = hm[:, 0]
        fs = _cmp_sample(cache_t, page_table, l, w_cmp, n_group)
        q8s = q8.transpose(1, 0, 2)
        o_c, idx = _nsa_sample_select(q8s, fs, cbias, kc_tabs_s, ov_s, past, n_sel_s, _pick(Bs, (4, 2, 1)))
        idx = idx[:, :A_KV, :min(N_TOP, n_sel_s)]
        new_rows = jnp.pad(jnp.stack([rows[:, 2 * LANES:3 * LANES], rows[:, 3 * LANES:], win[:, :LANES],
                                      win[:, LANES:]], axis=1), ((0, 0), (0, A_HEADS - 4), (0, 0)))
        gate_rows = jnp.pad(misc[:, 2 * M_HEADS:N_MISC].reshape(Bs, A_HEADS, N_BRANCH),
                            ((0, 0), (0, 0), (0, LANES - N_BRANCH)))
        oa8, nwin = _nsa_sample_attend(page_table, idx, q8s, new_rows, new_rows.transpose(0, 2, 1), o_c, gate_rows,
                                       win_t, cache_t, l, past)
        grp = (jnp.arange(A_HEADS) // A_REP)[None, :, None, None]
        oa = jnp.take_along_axis(oa8.reshape(Bs, A_HEADS, A_KV, A_DIM), jnp.broadcast_to(grp, (Bs, A_HEADS, 1, A_DIM)),
                                 axis=2).reshape(Bs, A_WIDTH).astype(BF16)
        xs_new = _ffn(xs, hm, oa, wo, nf, wg, wu, wd, nfin, final, Bs, tf)
        if final:
            y_s = xs_new
        xs = xs_new
        rows_s.append(rows.reshape(Bs, Ts, n_slots, A_KV, A_DIM))
        win_s.append(nwin.reshape(Bs, 2, A_KV, A_DIM, wb).transpose(0, 1, 4, 2, 3))
        c_s.append(c_new); n_s.append(n_new); m_s.append(m_new[:, :, 0])

    return (y_p.reshape(B, T, D), y_s.reshape(Bs, Ts, D),
            kvt.reshape(B, depth, n_slots, A_KV, A_DIM, T).transpose(0, 5, 1, 2, 3, 4), jnp.stack(rows_s, axis=2),
            jnp.stack(win_p, axis=1), jnp.stack(win_s, axis=1),
            jnp.stack(c_p, axis=1), jnp.stack(c_s, axis=1),
            jnp.stack(n_p, axis=1), jnp.stack(n_s, axis=1),
            jnp.stack(m_p, axis=1), jnp.stack(m_s, axis=1))
```
